```python
import jax, jax.numpy as jnp
from jax import lax
import numpy as np

D_MODEL = 1024
BATCH = 8
SEQ = 4096
DEPTH = 1

GRID_W = 64
CTX_LEN = 256

GLA_HEADS = 4
GLA_DK = 64
GLA_DV = 128
GLA_GATE_RANK = 16
GLA_GATE_NORM = 16.0
GLA_CHUNK = 64
GLA_WIDTH = GLA_HEADS * GLA_DV

MLA_HEADS = 8
MLA_NOPE = 64
MLA_ROPE = 32
MLA_DV = 64
MLA_Q_RANK = 256
MLA_KV_RANK = 128
MLA_WIDTH = MLA_HEADS * MLA_DV
Q_BLOCK = 128
ROPE_BASE = 10000.0
ROPE_AXIS_PAIRS = MLA_ROPE // 4

MIX_WIDTH = GLA_WIDTH + MLA_WIDTH
IN_SPLITS = (GLA_HEADS * GLA_DK,
             GLA_HEADS * GLA_DK,
             GLA_WIDTH,
             GLA_GATE_RANK,
             GLA_GATE_RANK,
             GLA_WIDTH,
             MLA_Q_RANK,
             MLA_KV_RANK,
             MLA_ROPE)
IN_WIDTH = 1984

N_GROUPS = 4
EXPERTS_PER_GROUP = 4
N_EXPERTS = N_GROUPS * EXPERTS_PER_GROUP
TOP_K = 2
D_EXPERT = 256

DEEPNORM_ALPHA = (2.0 * DEPTH) ** 0.25
DEEPNORM_BETA = (8.0 * DEPTH) ** -0.25
EPS = 1e-6

kernel_name = "hymba_gla_mla_hmoe_dit_block"


def layernorm(x, g, b):
    xf = x.astype(jnp.float32)
    mu = xf.mean(-1, keepdims=True)
    var = jnp.square(xf - mu).mean(-1, keepdims=True)
    return ((xf - mu) * lax.rsqrt(var + EPS)).astype(x.dtype) * g + b


def rmsnorm(x, g):
    xf = x.astype(jnp.float32)
    return (xf * lax.rsqrt(jnp.square(xf).mean(-1, keepdims=True) + EPS)).astype(x.dtype) * g


def modulate(x, shift, scale):
    return x * (1.0 + scale) + shift


def rope_half(x, ang):
    h = x.shape[-1] // 2
    cos = jnp.cos(ang).astype(x.dtype)
    sin = jnp.sin(ang).astype(x.dtype)
    x1, x2 = x[..., :h], x[..., h:]
    return jnp.concatenate([x1 * cos - x2 * sin, x2 * cos + x1 * sin], axis=-1)


def rope2d(x, ang_row, ang_col):
    half = x.shape[-1] // 2
    return jnp.concatenate([rope_half(x[..., :half], ang_row),
                            rope_half(x[..., half:], ang_col)], axis=-1)


def gla_chunked(q, k, v, log_g, s0):
    B, T, H, DK = q.shape
    DV = v.shape[-1]
    N = T // GLA_CHUNK

    def to_chunks(a):
        return a.astype(jnp.float32).reshape(B, N, GLA_CHUNK, H, a.shape[-1]).transpose(0, 3, 1, 2, 4)

    qc = to_chunks(q) * (DK ** -0.5)
    kc, vc, gc = to_chunks(k), to_chunks(v), to_chunks(log_g)
    b = jnp.cumsum(gc, axis=3)
    b_last = b[:, :, :, -1:, :]
    q_e = qc * jnp.exp(b)
    k_e = kc * jnp.exp(-b)
    mask = jnp.tril(jnp.ones((GLA_CHUNK, GLA_CHUNK), dtype=bool))
    a = jnp.where(mask, jnp.einsum('bhnid,bhnjd->bhnij', q_e, k_e), 0.0)
    o_intra = jnp.einsum('bhnij,bhnjv->bhniv', a, vc)
    ds = jnp.einsum('bhncd,bhncv->bhndv', kc * jnp.exp(b_last - b), vc)
    decay = jnp.exp(b_last[:, :, :, 0, :])

    def step(s, inp):
        d_n, ds_n = inp
        return d_n[..., None] * s + ds_n, s

    s_final, s_prev = lax.scan(step, s0.astype(jnp.float32),
                               (jnp.moveaxis(decay, 2, 0), jnp.moveaxis(ds, 2, 0)))
    o_inter = jnp.einsum('bhncd,nbhdv->bhncv', q_e, s_prev)
    o = (o_intra + o_inter).transpose(0, 2, 3, 1, 4).reshape(B, T, H, DV)
    return o.astype(v.dtype), s_final


def gla_group(p_lat, p_ctx, w_gk_f, b_gk_f, w_gk_b, b_gk_b, gla_norm_g, need_ctx_out):
    def heads(p):
        q, k, v, gf, gb, r = p
        B, T, _ = q.shape
        sh = lambda a, d: a.reshape(B, T, GLA_HEADS, d)
        lg_f = jax.nn.log_sigmoid((gf @ w_gk_f + b_gk_f).astype(jnp.float32)) / GLA_GATE_NORM
        lg_b = jax.nn.log_sigmoid((gb @ w_gk_b + b_gk_b).astype(jnp.float32)) / GLA_GATE_NORM
        return sh(q, GLA_DK), sh(k, GLA_DK), sh(v, GLA_DV), sh(lg_f, GLA_DK), sh(lg_b, GLA_DK), sh(r, GLA_DV)

    def flip(a):
        return jnp.flip(a, axis=1)

    def out(o, r):
        B, T = o.shape[:2]
        return (rmsnorm(o, gla_norm_g) * jax.nn.silu(r)).reshape(B, T, GLA_WIDTH)

    q, k, v, lgf, lgb, r = heads(p_lat)
    qc, kc, vc, lgf_c, lgb_c, rc = heads(p_ctx)
    zeros = jnp.zeros((q.shape[0], GLA_HEADS, GLA_DK, GLA_DV), jnp.float32)
    o_cf, s_f = gla_chunked(qc, kc, vc, lgf_c, zeros)
    o_cb, s_b = gla_chunked(flip(qc), flip(kc), flip(vc), flip(lgb_c), zeros)
    o_f, _ = gla_chunked(q, k, v, lgf, s_f)
    o_b, _ = gla_chunked(flip(q), flip(k), flip(v), flip(lgb), s_b)
    lat = out(o_f + flip(o_b), r)
    ctx_out = out(o_cf + flip(o_cb), rc) if need_ctx_out else None
    return lat, ctx_out


def mla_group(p_lat, p_ctx, ang_row, ang_col, mla_q_norm_g, w_uq, mla_kv_norm_g, w_ukv, need_ctx_out):
    scale = (MLA_NOPE + MLA_ROPE) ** -0.5

    def project(p, rotate):
        cq, ckv, kr = p
        B, T, _ = cq.shape
        q = (rmsnorm(cq, mla_q_norm_g) @ w_uq).reshape(B, T, MLA_HEADS, MLA_NOPE + MLA_ROPE)
        qn, qr = q[..., :MLA_NOPE], q[..., MLA_NOPE:]
        kv = (rmsnorm(ckv, mla_kv_norm_g) @ w_ukv).reshape(B, T, MLA_HEADS, MLA_NOPE + MLA_DV)
        kn, v = kv[..., :MLA_NOPE], kv[..., MLA_NOPE:]
        if rotate:
            qr = rope2d(qr, ang_row[:, None, :], ang_col[:, None, :])
            kr = rope2d(kr, ang_row, ang_col)
        return qn, qr, kn, kr, v

    qn, qr, kn, kr, v = project(p_lat, True)
    qn_c, qr_c, kn_c, kr_c, v_c = project(p_ctx, False)
    B, S = qn.shape[:2]
    nb = S // Q_BLOCK

    def blocks(a):
        return a.reshape(B, nb, Q_BLOCK, *a.shape[2:]).swapaxes(0, 1)

    def attend_block(args):
        qn_i, qr_i = args
        s_lat = jnp.einsum('bqhd,bkhd->bhqk', qn_i, kn) + jnp.einsum('bqhr,bkr->bhqk', qr_i, kr)
        s_ctx = jnp.einsum('bqhd,bkhd->bhqk', qn_i, kn_c) + jnp.einsum('bqhr,bkr->bhqk', qr_i, kr_c)
        s = jnp.concatenate([s_lat, s_ctx], axis=-1).astype(jnp.float32) * scale
        p = jax.nn.softmax(s, axis=-1).astype(v.dtype)
        return (jnp.einsum('bhqk,bkhv->bqhv', p[..., :S], v)
                + jnp.einsum('bhqk,bkhv->bqhv', p[..., S:], v_c))

    o = lax.map(attend_block, (blocks(qn), blocks(qr)))
    lat = o.swapaxes(0, 1).reshape(B, S, MLA_WIDTH)
    ctx_out = None
    if need_ctx_out:
        L = qn_c.shape[1]
        s = (jnp.einsum('bqhd,bkhd->bhqk', qn_c, kn_c)
             + jnp.einsum('bqhr,bkr->bhqk', qr_c, kr_c)).astype(jnp.float32) * scale
        p = jax.nn.softmax(s, axis=-1).astype(v_c.dtype)
        ctx_out = jnp.einsum('bhqk,bkhv->bqhv', p, v_c).reshape(B, L, MLA_WIDTH)
    return lat, ctx_out


def hybrid_mixer(u_lat, u_ctx, ang_row, ang_col, w_in, w_gk_f, b_gk_f, w_gk_b, b_gk_b, gla_norm_g,
                 mla_q_norm_g, w_uq, mla_kv_norm_g, w_ukv, w_o, need_ctx_out):
    offsets = np.cumsum(IN_SPLITS)[:-1].tolist()
    p_lat = jnp.split(u_lat @ w_in, offsets, axis=-1)
    p_ctx = jnp.split(u_ctx @ w_in, offsets, axis=-1)
    g_lat, g_ctx = gla_group(p_lat[:6], p_ctx[:6], w_gk_f, b_gk_f, w_gk_b, b_gk_b, gla_norm_g, need_ctx_out)
    m_lat, m_ctx = mla_group(p_lat[6:], p_ctx[6:], ang_row, ang_col, mla_q_norm_g, w_uq,
                             mla_kv_norm_g, w_ukv, need_ctx_out)
    y_lat = jnp.concatenate([g_lat, m_lat], axis=-1) @ w_o
    y_ctx = jnp.concatenate([g_ctx, m_ctx], axis=-1) @ w_o if need_ctx_out else None
    return y_lat, y_ctx


def hier_moe(h, w_rg, b_rg, w_re, b_re, w_gate, w_up, w_down):
    shape = h.shape
    hf = h.reshape(-1, shape[-1])
    n = hf.shape[0]
    g_prob = jax.nn.softmax((hf @ w_rg + b_rg).astype(jnp.float32), axis=-1)
    p_g, g_top = lax.top_k(g_prob, 1)
    e_logits = (hf @ w_re + b_re).astype(jnp.float32).reshape(n, N_GROUPS, EXPERTS_PER_GROUP)
    e_in = e_logits[jnp.arange(n), g_top[:, 0]]
    top_p, top_i = lax.top_k(jax.nn.softmax(e_in, axis=-1), TOP_K)
    top_w = top_p / top_p.sum(-1, keepdims=True) * p_g
    expert_id = g_top * EXPERTS_PER_GROUP + top_i
    combine = jnp.einsum('nk,nke->ne', top_w,
                         jax.nn.one_hot(expert_id, N_EXPERTS, dtype=jnp.float32)).astype(h.dtype)
    out = jnp.zeros_like(hf)
    for e in range(N_EXPERTS):
        he = jax.nn.silu(hf @ w_gate[e]) * (hf @ w_up[e])
        out = out + combine[:, e:e + 1] * (he @ w_down[e])
    return out.reshape(shape)


def setup_inputs(seed: int = 0) -> dict:
    key = jax.random.key(seed)
    ks = jax.random.split(key, 32)
    f32 = jnp.float32

    def nrm(k, shape, s):
        return jax.random.normal(k, shape, f32) * s

    def gain(k, shape):
        return 1.0 + 0.02 * jax.random.normal(k, shape, f32)

    D = D_MODEL
    return {
        "x": nrm(ks[0], (BATCH, SEQ, D), 1.0),
        "c": nrm(ks[1], (BATCH, D), 1.0),
        "ctx": nrm(ks[2], (BATCH, CTX_LEN, D), 1.0),
        "c_ctx": nrm(ks[3], (D,), 1.0),
        "w_ada": nrm(ks[4], (DEPTH, D, 6 * D), 0.5 * D ** -0.5),
        "b_ada": nrm(ks[5], (DEPTH, 6 * D), 0.02),
        "w_in": nrm(ks[6], (DEPTH, D, IN_WIDTH), D ** -0.5),
        "w_gk_f": nrm(ks[7], (DEPTH, GLA_GATE_RANK, GLA_HEADS * GLA_DK), GLA_GATE_RANK ** -0.5),
        "b_gk_f": nrm(ks[8], (DEPTH, GLA_HEADS * GLA_DK), 0.1),
        "w_gk_b": nrm(ks[9], (DEPTH, GLA_GATE_RANK, GLA_HEADS * GLA_DK), GLA_GATE_RANK ** -0.5),
        "b_gk_b": nrm(ks[10], (DEPTH, GLA_HEADS * GLA_DK), 0.1),
        "gla_norm_g": gain(ks[11], (DEPTH, GLA_DV)),
        "mla_q_norm_g": gain(ks[12], (DEPTH, MLA_Q_RANK)),
        "w_uq": nrm(ks[13], (DEPTH, MLA_Q_RANK, MLA_HEADS * (MLA_NOPE + MLA_ROPE)), MLA_Q_RANK ** -0.5),
        "mla_kv_norm_g": gain(ks[14], (DEPTH, MLA_KV_RANK)),
        "w_ukv": nrm(ks[15], (DEPTH, MLA_KV_RANK, MLA_HEADS * (MLA_NOPE + MLA_DV)), MLA_KV_RANK ** -0.5),
        "w_o": nrm(ks[16], (DEPTH, MIX_WIDTH, D), DEEPNORM_BETA * MIX_WIDTH ** -0.5),
        "ln1_g": gain(ks[17], (DEPTH, D)),
        "ln1_b": nrm(ks[18], (DEPTH, D), 0.02),
        "w_router_group": nrm(ks[19], (DEPTH, D, N_GROUPS), D ** -0.5),
        "b_router_group": nrm(ks[20], (DEPTH, N_GROUPS), 0.01),
        "w_router_expert": nrm(ks[21], (DEPTH, D, N_EXPERTS), D ** -0.5),
        "b_router_expert": nrm(ks[22], (DEPTH, N_EXPERTS), 0.01),
        "w_expert_gate": nrm(ks[23], (DEPTH, N_EXPERTS, D, D_EXPERT), D ** -0.5),
        "w_expert_up": nrm(ks[24], (DEPTH, N_EXPERTS, D, D_EXPERT), D ** -0.5),
        "w_expert_down": nrm(ks[25], (DEPTH, N_EXPERTS, D_EXPERT, D), DEEPNORM_BETA * D_EXPERT ** -0.5),
        "ln2_g": gain(ks[26], (DEPTH, D)),
        "ln2_b": nrm(ks[27], (DEPTH, D), 0.02),
    }


def reference(x, c, ctx, c_ctx, w_ada, b_ada, w_in, w_gk_f, b_gk_f, w_gk_b, b_gk_b, gla_norm_g,
              mla_q_norm_g, w_uq, mla_kv_norm_g, w_ukv, w_o, ln1_g, ln1_b, w_router_group,
              b_router_group, w_router_expert, b_router_expert, w_expert_gate, w_expert_up,
              w_expert_down, ln2_g, ln2_b):
    seq_len = x.shape[1]
    rows = seq_len // GRID_W
    row, col = jnp.meshgrid(jnp.arange(rows), jnp.arange(GRID_W), indexing='ij')
    inv_freq = ROPE_BASE ** (-jnp.arange(ROPE_AXIS_PAIRS, dtype=jnp.float32) / ROPE_AXIS_PAIRS)
    ang_row = row.reshape(-1)[:, None].astype(jnp.float32) * inv_freq
    ang_col = col.reshape(-1)[:, None].astype(jnp.float32) * inv_freq

    for l in range(DEPTH):
        last = l == DEPTH - 1
        m_lat = jnp.split((jax.nn.silu(c) @ w_ada[l] + b_ada[l])[:, None, :], 6, axis=-1)
        m_ctx = jnp.split(jax.nn.silu(c_ctx) @ w_ada[l] + b_ada[l], 6, axis=-1)

        u_lat = modulate(x, m_lat[0], m_lat[1])
        u_ctx = modulate(ctx, m_ctx[0], m_ctx[1])
        y_lat, y_ctx = hybrid_mixer(u_lat, u_ctx, ang_row, ang_col, w_in[l], w_gk_f[l], b_gk_f[l],
                                    w_gk_b[l], b_gk_b[l], gla_norm_g[l], mla_q_norm_g[l], w_uq[l],
                                    mla_kv_norm_g[l], w_ukv[l], w_o[l], not last)
        x = layernorm(DEEPNORM_ALPHA * x + m_lat[2] * y_lat, ln1_g[l], ln1_b[l])

        h = modulate(x, m_lat[3], m_lat[4])
        y = hier_moe(h, w_router_group[l], b_router_group[l], w_router_expert[l], b_router_expert[l],
                     w_expert_gate[l], w_expert_up[l], w_expert_down[l])
        x = layernorm(DEEPNORM_ALPHA * x + m_lat[5] * y, ln2_g[l], ln2_b[l])

        if not last:
            ctx = layernorm(DEEPNORM_ALPHA * ctx + m_ctx[2] * y_ctx, ln1_g[l], ln1_b[l])
            hc = modulate(ctx, m_ctx[3], m_ctx[4])
            yc = hier_moe(hc, w_router_group[l], b_router_group[l], w_router_expert[l],
                          b_router_expert[l], w_expert_gate[l], w_expert_up[l], w_expert_down[l])
            ctx = layernorm(DEEPNORM_ALPHA * ctx + m_ctx[5] * yc, ln2_g[l], ln2_b[l])
    return x
```

```python
import functools
import math

import jax
import jax.numpy as jnp
import numpy as np
from jax import lax
from jax.experimental import pallas as pl
from jax.experimental.pallas import tpu as pltpu

F32 = jnp.float32
BF16 = jnp.bfloat16

D_MODEL = 1024
GRID_W = 64
GLA_HEADS = 4
GLA_DK = 64
GLA_DV = 128
GLA_RANK = 16
GLA_GATE_NORM = 16.0
GLA_CHUNK = 64
GLA_QK = GLA_HEADS * GLA_DK
GLA_WIDTH = GLA_HEADS * GLA_DV
MLA_HEADS = 8
MLA_NOPE = 64
MLA_ROPE = 32
MLA_DV = 64
MLA_Q_RANK = 256
MLA_KV_RANK = 128
MLA_PAIRS = MLA_HEADS // 2
ROPE_BASE = 10000.0
N_GROUPS = 4
EXPERTS_PER_GROUP = 4
N_EXPERTS = 16
D_EXPERT = 256
DEPTH = 1
DEEPNORM_ALPHA = (2.0 * DEPTH) ** 0.25
EPS = 1e-6

LANES = 128
COL_Q, COL_K, COL_V, COL_R, COL_CQ, COL_CKV = 0, 256, 512, 1024, 1536, 1792
MAIN_W = 1920
MISC_W = 256
ROPE_LANE0 = 64
ROUTER_LANE0 = N_GROUPS
VMEM_LIMIT = 48 * 1024 * 1024


def _cparams(sem):
    return pltpu.CompilerParams(dimension_semantics=sem, vmem_limit_bytes=VMEM_LIMIT)


def _dot(a, b):
    return jnp.dot(a, b, preferred_element_type=F32)


def _dot_nt(a, b):
    return lax.dot_general(a, b, (((1,), (1,)), ((), ())), preferred_element_type=F32)


def _dot_tn(a, b):
    return lax.dot_general(a, b, (((0,), (0,)), ((), ())), preferred_element_type=F32)


def _sigmoid(x):
    return 1.0 / (1.0 + jnp.exp(-x))


def _ada_kernel(c_ref, w_ref, b_ref, o_ref):
    a = c_ref[...]
    a = a * _sigmoid(a)
    o_ref[...] = _dot(a.astype(BF16), w_ref[...].astype(BF16)) + b_ref[...]


def _ada(c_all, w, b):
    rows, d = c_all.shape
    n = w.shape[1]
    bn = 1536
    return pl.pallas_call(
        _ada_kernel,
        grid=(n // bn,),
        in_specs=[pl.BlockSpec((rows, d), lambda j: (0, 0)),
                  pl.BlockSpec((d, bn), lambda j: (0, j)),
                  pl.BlockSpec((1, bn), lambda j: (0, j))],
        out_specs=pl.BlockSpec((rows, bn), lambda j: (0, j)),
        out_shape=jax.ShapeDtypeStruct((rows, n), F32),
        compiler_params=_cparams(("parallel",)),
        name="ada",
    )(c_all, w, b.reshape(1, n))


def _rope_tab_kernel(cos_ref, sin_ref):
    shape = cos_ref.shape
    t = lax.broadcasted_iota(jnp.int32, shape, 0) + pl.program_id(0) * shape[0]
    lane = lax.broadcasted_iota(jnp.int32, shape, 1)
    j = lane - ROPE_LANE0
    valid = (j >= 0) & (j < MLA_ROPE)
    f = (j & 7).astype(F32)
    inv_freq = jnp.exp(f * (-math.log(ROPE_BASE) / 8.0))
    pos = jnp.where(j >= 16, t & (GRID_W - 1), jnp.right_shift(t, int(math.log2(GRID_W)))).astype(F32)
    ang = pos * inv_freq
    sign = jnp.where((j & 15) < 8, -1.0, 1.0)
    cos_ref[...] = jnp.where(valid, jnp.cos(ang), 0.0)
    sin_ref[...] = jnp.where(valid, sign * jnp.sin(ang), 0.0)


def _rope_tables(seq):
    tm = 512
    spec = pl.BlockSpec((tm, LANES), lambda i: (i, 0))
    return pl.pallas_call(
        _rope_tab_kernel,
        grid=(seq // tm,),
        out_specs=[spec, spec],
        out_shape=(jax.ShapeDtypeStruct((seq, LANES), F32), jax.ShapeDtypeStruct((seq, LANES), F32)),
        compiler_params=_cparams(("parallel",)),
        name="rope_tab",
    )()


def _inproj_kernel(x_ref, mod_ref, w_ref, main_ref, misc_ref):
    shift = mod_ref[0:1, :]
    scale = mod_ref[1:2, :]
    u = (x_ref[...] * (1.0 + scale) + shift).astype(BF16)
    for c0 in range(0, MAIN_W, 384):
        main_ref[:, c0:c0 + 384] = _dot(u, w_ref[:, c0:c0 + 384]).astype(BF16)
    misc_ref[...] = _dot(u, w_ref[:, MAIN_W:MAIN_W + MISC_W])


def _inproj(x, mod, w, per_batch, tm):
    bsz, t, d = x.shape
    mod_map = (lambda b, i: (b, 0, 0)) if per_batch else (lambda b, i: (0, 0, 0))
    return pl.pallas_call(
        _inproj_kernel,
        grid=(bsz, t // tm),
        in_specs=[pl.BlockSpec((None, tm, d), lambda b, i: (b, i, 0)),
                  pl.BlockSpec((None, 6, d), mod_map),
                  pl.BlockSpec(w.shape, lambda b, i: (0, 0))],
        out_specs=[pl.BlockSpec((None, tm, MAIN_W), lambda b, i: (b, i, 0)),
                   pl.BlockSpec((None, tm, MISC_W), lambda b, i: (b, i, 0))],
        out_shape=(jax.ShapeDtypeStruct((bsz, t, MAIN_W), BF16),
                   jax.ShapeDtypeStruct((bsz, t, MISC_W), F32)),
        compiler_params=_cparams(("parallel", "parallel")),
        name="inproj",
    )(x, mod, w)


def _gla_kernel(qkf_ref, vf_ref, mf_ref, qkb_ref, vb_ref, mb_ref, wgf_ref, bgf_ref, wgb_ref, bgb_ref,
                s0_ref, of_ref, ob_ref, sfin_ref, st_f, st_b, zf_scr, zb_scr, *, n_chunks):
    i = pl.program_id(1)
    nblk = pl.num_programs(1)
    C = GLA_CHUNK

    @pl.when(i == 0)
    def _():
        st_f[...] = s0_ref[0]
        st_b[...] = s0_ref[1]

    zf_scr[...] = _dot(mf_ref[...].astype(BF16), wgf_ref[...]) + bgf_ref[...]
    zb_scr[...] = _dot(mb_ref[...].astype(BF16), wgb_ref[...]) + bgb_ref[...]

    def one_chunk(qk_ref, v_ref, z_scr, o_ref, st, r0, forward):
        rows = pl.ds(r0, C)
        r64 = lax.broadcasted_iota(jnp.int32, (C, C), 0)
        c64 = lax.broadcasted_iota(jnp.int32, (C, C), 1)
        ra = lax.broadcasted_iota(jnp.int32, (GLA_HEADS * C, C), 0) & (C - 1)
        ca = lax.broadcasted_iota(jnp.int32, (GLA_HEADS * C, C), 1)
        if forward:
            tri = jnp.where(c64 <= r64, 1.0, 0.0).astype(BF16)
            causal = ca <= ra
            last_row = C - 1
        else:
            tri = jnp.where(c64 >= r64, 1.0, 0.0).astype(BF16)
            causal = ca >= ra
            last_row = 0
        lane_head = jnp.right_shift(lax.broadcasted_iota(jnp.int32, (C, GLA_QK), 1), 6)
        head_masks = [jnp.where(lane_head == h, 1.0, 0.0) for h in range(GLA_HEADS)]
        z = z_scr[rows, :]
        lg = (jnp.minimum(z, 0.0) - jnp.log(1.0 + jnp.exp(-jnp.abs(z)))) * (1.0 / GLA_GATE_NORM)
        lg_hi = lg.astype(BF16)
        lg_lo = (lg - lg_hi.astype(F32)).astype(BF16)
        b = _dot(tri, lg_hi) + _dot(tri, lg_lo)
        tot = b[last_row:last_row + 1, :]
        q = qk_ref[rows, 0:GLA_QK].astype(F32)
        k = qk_ref[rows, GLA_QK:2 * GLA_QK].astype(F32)
        v = v_ref[rows, :]
        q_e = q * (jnp.exp(b) * (GLA_DK ** -0.5))
        k_e = (k * jnp.exp(-b)).astype(BF16)
        k_dec = (k * jnp.exp(tot - b)).astype(BF16)
        dec = jnp.exp(tot)
        qm = jnp.concatenate([(q_e * head_masks[h]).astype(BF16) for h in range(GLA_HEADS)], axis=0)
        a = _dot_nt(qm, k_e)
        a = jnp.where(causal, a, 0.0).astype(BF16)
        st_b16 = st[...].astype(BF16)
        outs = []
        for h in range(GLA_HEADS):
            o_h = _dot(a[h * C:(h + 1) * C, :], v[:, h * GLA_DV:(h + 1) * GLA_DV])
            o_h = o_h + _dot_nt(qm[h * C:(h + 1) * C, :], st_b16[h * GLA_DV:(h + 1) * GLA_DV, :])
            outs.append(o_h)
        o_ref[rows, :] = jnp.concatenate(outs, axis=1)
        st[...] = st[...] * dec + _dot_tn(v, k_dec)

    def body(c, carry):
        r_f = pl.multiple_of(c * C, C)
        r_b = pl.multiple_of((n_chunks - 1 - c) * C, C)
        one_chunk(qkf_ref, vf_ref, zf_scr, of_ref, st_f, r_f, True)
        one_chunk(qkb_ref, vb_ref, zb_scr, ob_ref, st_b, r_b, False)
        return carry

    lax.fori_loop(0, n_chunks, body, 0)

    @pl.when(i == nblk - 1)
    def _():
        sfin_ref[0] = st_f[...]
        sfin_ref[1] = st_b[...]


def _gla(main, misc, wgf, bgf, wgb, bgb, s0, tm):
    bsz, t, _ = main.shape
    nblk = t // tm
    fwd = lambda b, i: (b, i, 0)
    bwd = lambda b, i: (b, nblk - 1 - i, 0)
    const2 = lambda b, i: (0, 0)
    kern = functools.partial(_gla_kernel, n_chunks=tm // GLA_CHUNK)
    return pl.pallas_call(
        kern,
        grid=(bsz, nblk),
        in_specs=[pl.BlockSpec((None, tm, 2 * GLA_QK), fwd),
                  pl.BlockSpec((None, tm, GLA_WIDTH), lambda b, i: (b, i, 1)),
                  pl.BlockSpec((None, tm, LANES), fwd),
                  pl.BlockSpec((None, tm, 2 * GLA_QK), bwd),
                  pl.BlockSpec((None, tm, GLA_WIDTH), lambda b, i: (b, nblk - 1 - i, 1)),
                  pl.BlockSpec((None, tm, LANES), bwd),
                  pl.BlockSpec(wgf.shape, const2), pl.BlockSpec(bgf.shape, const2),
                  pl.BlockSpec(wgb.shape, const2), pl.BlockSpec(bgb.shape, const2),
                  pl.BlockSpec((None, 2, GLA_WIDTH, GLA_QK), lambda b, i: (b, 0, 0, 0))],
        out_specs=[pl.BlockSpec((None, tm, GLA_WIDTH), fwd),
                   pl.BlockSpec((None, tm, GLA_WIDTH), bwd),
                   pl.BlockSpec((None, 2, GLA_WIDTH, GLA_QK), lambda b, i: (b, 0, 0, 0))],
        out_shape=(jax.ShapeDtypeStruct((bsz, t, GLA_WIDTH), F32),
                   jax.ShapeDtypeStruct((bsz, t, GLA_WIDTH), F32),
                   jax.ShapeDtypeStruct((bsz, 2, GLA_WIDTH, GLA_QK), F32)),
        scratch_shapes=[pltpu.VMEM((GLA_WIDTH, GLA_QK), F32), pltpu.VMEM((GLA_WIDTH, GLA_QK), F32),
                        pltpu.VMEM((tm, GLA_QK), F32), pltpu.VMEM((tm, GLA_QK), F32)],
        compiler_params=_cparams(("parallel", "arbitrary")),
        name="gla",
    )(main, main, misc, main, main, misc, wgf, bgf, wgb, bgb, s0)


def _rmsnorm_rows(x, g):
    xf = x.astype(F32)
    ms = jnp.mean(xf * xf, axis=-1, keepdims=True)
    return (xf * lax.rsqrt(ms + EPS)) * g


def _mla_proj_kernel(*refs, rotate, with_q):
    if with_q:
        (cq_ref, ckv_ref, m0_ref, m1_ref, cos_ref, sin_ref, qg_ref, kvg_ref, wq_ref, wkv_ref,
         q_out, k_out, v_out) = refs
    else:
        ckv_ref, m0_ref, kvg_ref, wkv_ref, k_out, v_out = refs
    hw = MLA_HEADS * LANES
    lane = lax.broadcasted_iota(jnp.int32, m0_ref.shape, 1)
    rope_lanes = (lane >= ROPE_LANE0) & (lane < ROPE_LANE0 + MLA_ROPE)
    if rotate:
        cos = cos_ref[...]
        sin = sin_ref[...]
        kr = m0_ref[...] * cos + m1_ref[...] * sin
    else:
        kr = jnp.where(rope_lanes, m0_ref[...], 0.0)
    kv = _dot(_rmsnorm_rows(ckv_ref[...], kvg_ref[...]).astype(BF16), wkv_ref[...])
    for h in range(MLA_HEADS):
        k_out[h] = (kv[:, h * LANES:(h + 1) * LANES] + kr).astype(BF16)
    for p in range(MLA_PAIRS):
        v_out[p] = kv[:, hw + p * LANES:hw + (p + 1) * LANES].astype(BF16)
    if with_q:
        qs = (MLA_NOPE + MLA_ROPE) ** -0.5 * math.log2(math.e)
        cq_tab = jnp.where(lane < MLA_NOPE, qs, cos * qs)
        sq_tab = sin * qs
        qq = _dot(_rmsnorm_rows(cq_ref[...], qg_ref[...]).astype(BF16), wq_ref[...])
        for h in range(MLA_HEADS):
            qa = qq[:, h * LANES:(h + 1) * LANES]
            qb = qq[:, hw + h * LANES:hw + (h + 1) * LANES]
            q_out[h] = (qa * cq_tab + qb * sq_tab).astype(BF16)


def _mla_proj_lat(main, misc, cos_t, sin_t, qg, kvg, wq, wkv, tm):
    bsz, t, _ = main.shape
    c2 = lambda b, i: (0, 0)
    kern = functools.partial(_mla_proj_kernel, rotate=True, with_q=True)
    return pl.pallas_call(
        kern,
        grid=(bsz, t // tm),
        in_specs=[pl.BlockSpec((None, tm, MLA_Q_RANK), lambda b, i: (b, i, COL_CQ // MLA_Q_RANK)),
                  pl.BlockSpec((None, tm, MLA_KV_RANK), lambda b, i: (b, i, COL_CKV // MLA_KV_RANK)),
                  pl.BlockSpec((None, tm, LANES), lambda b, i: (b, i, 0)),
                  pl.BlockSpec((None, tm, LANES), lambda b, i: (b, i, 1)),
                  pl.BlockSpec((tm, LANES), lambda b, i: (i, 0)),
                  pl.BlockSpec((tm, LANES), lambda b, i: (i, 0)),
                  pl.BlockSpec(qg.shape, c2), pl.BlockSpec(kvg.shape, c2),
                  pl.BlockSpec(wq.shape, c2), pl.BlockSpec(wkv.shape, c2)],
        out_specs=[pl.BlockSpec((None, MLA_HEADS, tm, LANES), lambda b, i: (b, 0, i, 0)),
                   pl.BlockSpec((None, MLA_HEADS, tm, LANES), lambda b, i: (b, 0, i, 0)),
                   pl.BlockSpec((None, MLA_PAIRS, tm, LANES), lambda b, i: (b, 0, i, 0))],
        out_shape=(jax.ShapeDtypeStruct((bsz, MLA_HEADS, t, LANES), BF16),
                   jax.ShapeDtypeStruct((bsz, MLA_HEADS, t, LANES), BF16),
                   jax.ShapeDtypeStruct((bsz, MLA_PAIRS, t, LANES), BF16)),
        compiler_params=_cparams(("parallel", "parallel")),
        name="mla_proj_lat",
    )(main, main, misc, misc, cos_t, sin_t, qg, kvg, wq, wkv)


def _mla_proj_ctx(main, misc, kvg, wkv):
    bsz, t, _ = main.shape
    c2 = lambda b: (0, 0)
    kern = functools.partial(_mla_proj_kernel, rotate=False, with_q=False)
    return pl.pallas_call(
        kern,
        grid=(bsz,),
        in_specs=[pl.BlockSpec((None, t, MLA_KV_RANK), lambda b: (b, 0, COL_CKV // MLA_KV_RANK)),
                  pl.BlockSpec((None, t, LANES), lambda b: (b, 0, 0)),
                  pl.BlockSpec(kvg.shape, c2), pl.BlockSpec(wkv.shape, c2)],
        out_specs=[pl.BlockSpec((None, MLA_HEADS, t, LANES), lambda b: (b, 0, 0, 0)),
                   pl.BlockSpec((None, MLA_PAIRS, t, LANES), lambda b: (b, 0, 0, 0))],
        out_shape=(jax.ShapeDtypeStruct((bsz, MLA_HEADS, t, LANES), BF16),
                   jax.ShapeDtypeStruct((bsz, MLA_PAIRS, t, LANES), BF16)),
        compiler_params=_cparams(("parallel",)),
        name="mla_proj_ctx",
    )(main, misc, kvg, wkv)


def _attn_kernel(q_ref, kl_ref, kc_ref, vl_ref, vc_ref, o_ref, s_scr, *, tk):
    tq = q_ref.shape[1]
    s_len = kl_ref.shape[1]
    c_len = kc_ref.shape[1]
    chunks = [(kl_ref, vl_ref, c0, tk, c0) for c0 in range(0, s_len, tk)]
    chunks.append((kc_ref, vc_ref, 0, c_len, s_len))
    outs = []
    for hh in range(2):
        q = q_ref[hh]
        mrun = jnp.full((tq, LANES), -jnp.inf, F32)
        for k_ref, _, r0, n, col in chunks:
            s = _dot_nt(q, k_ref[hh, r0:r0 + n, :])
            s_scr[:, col:col + n] = s
            for t0 in range(0, n, LANES):
                mrun = jnp.maximum(mrun, s[:, t0:t0 + LANES])
        m = jnp.max(mrun, axis=-1, keepdims=True)
        lrun = jnp.zeros((tq, LANES), F32)
        acc = jnp.zeros((tq, LANES), F32)
        for _, v_ref, r0, n, col in chunks:
            p = jnp.exp2(s_scr[:, col:col + n] - m)
            for t0 in range(0, n, LANES):
                lrun = lrun + p[:, t0:t0 + LANES]
            acc = acc + _dot(p.astype(BF16), v_ref[r0:r0 + n, :])
        l = jnp.sum(lrun, axis=-1, keepdims=True)
        outs.append(acc * (1.0 / l))
    lane = lax.broadcasted_iota(jnp.int32, (tq, LANES), 1)
    o_ref[...] = jnp.where(lane < MLA_DV, outs[0], outs[1]).astype(BF16)


def _attn(q, k_lat, k_ctx, v_lat, v_ctx, tq, tk):
    bsz, _, s_len, _ = q.shape
    c_len = k_ctx.shape[2]
    kern = functools.partial(_attn_kernel, tk=tk)
    return pl.pallas_call(
        kern,
        grid=(bsz, MLA_PAIRS, s_len // tq),
        in_specs=[pl.BlockSpec((None, 2, tq, LANES), lambda b, p, i: (b, p, i, 0)),
                  pl.BlockSpec((None, 2, s_len, LANES), lambda b, p, i: (b, p, 0, 0)),
                  pl.BlockSpec((None, 2, c_len, LANES), lambda b, p, i: (b, p, 0, 0)),
                  pl.BlockSpec((None, None, s_len, LANES), lambda b, p, i: (b, p, 0, 0)),
                  pl.BlockSpec((None, None, c_len, LANES), lambda b, p, i: (b, p, 0, 0))],
        out_specs=pl.BlockSpec((None, None, tq, LANES), lambda b, p, i: (b, p, i, 0)),
        out_shape=jax.ShapeDtypeStruct((bsz, MLA_PAIRS, s_len, LANES), BF16),
        scratch_shapes=[pltpu.VMEM((tq, s_len + c_len), F32)],
        compiler_params=_cparams(("parallel", "parallel", "arbitrary")),
        name="attn",
    )(q, k_lat, k_ctx, v_lat, v_ctx)


def _layernorm_rows(z, g, b):
    mu = jnp.mean(z, axis=-1, keepdims=True)
    zc = z - mu
    var = jnp.mean(zc * zc, axis=-1, keepdims=True)
    return (zc * lax.rsqrt(var + EPS)) * g + b


def _outproj_kernel(of_ref, ob_ref, r_ref, ml_ref, x_ref, mod_ref, gg_ref, wo_ref, l1g_ref, l1b_ref,
                    wr_ref, br_ref, x1_ref, h_ref, comb_ref):
    tm = x_ref.shape[0]
    o = of_ref[...] + ob_ref[...]
    r = r_ref[...].astype(F32)
    gg = gg_ref[...]
    y = jnp.zeros((tm, D_MODEL), F32)
    for h in range(GLA_HEADS):
        sl = slice(h * GLA_DV, (h + 1) * GLA_DV)
        oh = o[:, sl]
        ms = jnp.mean(oh * oh, axis=-1, keepdims=True)
        rh = r[:, sl]
        gh = (oh * lax.rsqrt(ms + EPS)) * gg * (rh * _sigmoid(rh))
        y = y + _dot(gh.astype(BF16), wo_ref[sl, :])
    for p in range(MLA_PAIRS):
        y = y + _dot(ml_ref[p], wo_ref[GLA_WIDTH + p * LANES:GLA_WIDTH + (p + 1) * LANES, :])
    gate1 = mod_ref[2:3, :]
    x1 = _layernorm_rows(DEEPNORM_ALPHA * x_ref[...] + gate1 * y, l1g_ref[...], l1b_ref[...])
    x1_ref[...] = x1
    hmod = x1 * (1.0 + mod_ref[4:5, :]) + mod_ref[3:4, :]
    h_ref[...] = hmod.astype(BF16)

    h_hi = hmod.astype(BF16)
    h_lo = (hmod - h_hi.astype(F32)).astype(BF16)
    wr = wr_ref[...]
    w_hi = wr.astype(BF16)
    w_lo = (wr - w_hi.astype(F32)).astype(BF16)
    logits = _dot(h_hi, w_hi) + _dot(h_hi, w_lo) + _dot(h_lo, w_hi) + br_ref[...]

    lane = lax.broadcasted_iota(jnp.int32, (tm, LANES), 1).astype(F32)
    neg = -jnp.inf
    far = float(LANES)
    gl = jnp.where(lane < N_GROUPS, logits, neg)
    gmax = jnp.max(gl, axis=-1, keepdims=True)
    gsum = jnp.sum(jnp.exp(gl - gmax), axis=-1, keepdims=True)
    p_g = 1.0 / gsum
    g_top = jnp.min(jnp.where(gl == gmax, lane, far), axis=-1, keepdims=True)
    e0 = ROUTER_LANE0 + g_top * EXPERTS_PER_GROUP
    el = jnp.where((lane >= e0) & (lane < e0 + EXPERTS_PER_GROUP), logits, neg)
    e1max = jnp.max(el, axis=-1, keepdims=True)
    i1 = jnp.min(jnp.where(el == e1max, lane, far), axis=-1, keepdims=True)
    el2 = jnp.where(lane == i1, neg, el)
    e2max = jnp.max(el2, axis=-1, keepdims=True)
    i2 = jnp.min(jnp.where(el2 == e2max, lane, far), axis=-1, keepdims=True)
    t = jnp.exp(e2max - e1max)
    w1 = p_g / (1.0 + t)
    w2 = w1 * t
    comb_ref[...] = jnp.where(lane == i1, w1, jnp.where(lane == i2, w2, 0.0))


def _outproj(o_f, o_b, main, mlat, x, mod, gg, wo, l1g, l1b, wr, br, tm):
    bsz, t, d = x.shape
    c2 = lambda b, i: (0, 0)
    row = lambda b, i: (b, i, 0)
    return pl.pallas_call(
        _outproj_kernel,
        grid=(bsz, t // tm),
        in_specs=[pl.BlockSpec((None, tm, GLA_WIDTH), row),
                  pl.BlockSpec((None, tm, GLA_WIDTH), row),
                  pl.BlockSpec((None, tm, GLA_WIDTH), lambda b, i: (b, i, COL_R // GLA_WIDTH)),
                  pl.BlockSpec((None, MLA_PAIRS, tm, LANES), lambda b, i: (b, 0, i, 0)),
                  pl.BlockSpec((None, tm, d), row),
                  pl.BlockSpec((None, 6, d), lambda b, i: (b, 0, 0)),
                  pl.BlockSpec(gg.shape, c2), pl.BlockSpec(wo.shape, c2),
                  pl.BlockSpec(l1g.shape, c2), pl.BlockSpec(l1b.shape, c2),
                  pl.BlockSpec(wr.shape, c2), pl.BlockSpec(br.shape, c2)],
        out_specs=[pl.BlockSpec((None, tm, d), row),
                   pl.BlockSpec((None, tm, d), row),
                   pl.BlockSpec((None, tm, LANES), row)],
        out_shape=(jax.ShapeDtypeStruct((bsz, t, d), F32),
                   jax.ShapeDtypeStruct((bsz, t, d), BF16),
                   jax.ShapeDtypeStruct((bsz, t, LANES), F32)),
        compiler_params=_cparams(("parallel", "parallel")),
        name="outproj",
    )(o_f, o_b, main, mlat, x, mod, gg, wo, l1g, l1b, wr, br)


def _moe_kernel(h_ref, comb_ref, x1_ref, mod_ref, wgu_ref, wd_ref, l2g_ref, l2b_ref, o_ref, acc):
    e = pl.program_id(2)

    @pl.when(e == 0)
    def _():
        acc[...] = jnp.zeros_like(acc)

    gu = _dot(h_ref[...], wgu_ref[...])
    g = gu[:, :D_EXPERT]
    he = (g * _sigmoid(g)) * gu[:, D_EXPERT:]
    lane = lax.broadcasted_iota(jnp.int32, comb_ref.shape, 1)
    w = jnp.sum(jnp.where(lane == e + ROUTER_LANE0, comb_ref[...], 0.0), axis=-1, keepdims=True)
    acc[...] += _dot((he * w).astype(BF16), wd_ref[...])

    @pl.when(e == pl.num_programs(2) - 1)
    def _():
        gate2 = mod_ref[5:6, :]
        z = DEEPNORM_ALPHA * x1_ref[...] + gate2 * acc[...]
        o_ref[...] = _layernorm_rows(z, l2g_ref[...], l2b_ref[...])


def _moe(h, comb, x1, mod, wgu, wd, l2g, l2b, tm):
    bsz, t, d = x1.shape
    row = lambda b, i, e: (b, i, 0)
    c2 = lambda b, i, e: (0, 0)
    return pl.pallas_call(
        _moe_kernel,
        grid=(bsz, t // tm, N_EXPERTS),
        in_specs=[pl.BlockSpec((None, tm, d), row),
                  pl.BlockSpec((None, tm, LANES), row),
                  pl.BlockSpec((None, tm, d), row),
                  pl.BlockSpec((None, 6, d), lambda b, i, e: (b, 0, 0)),
                  pl.BlockSpec((None, d, 2 * D_EXPERT), lambda b, i, e: (e, 0, 0)),
                  pl.BlockSpec((None, D_EXPERT, d), lambda b, i, e: (e, 0, 0)),
                  pl.BlockSpec(l2g.shape, c2), pl.BlockSpec(l2b.shape, c2)],
        out_specs=pl.BlockSpec((None, tm, d), row),
        out_shape=jax.ShapeDtypeStruct((bsz, t, d), F32),
        scratch_shapes=[pltpu.VMEM((tm, d), F32)],
        compiler_params=_cparams(("parallel", "parallel", "arbitrary")),
        name="moe",
    )(h, comb, x1, mod, wgu, wd, l2g, l2b)


_ROPE_SWAP = np.concatenate([np.arange(8, 16), np.arange(0, 8), np.arange(24, 32), np.arange(16, 24)])


def _pack_w_in(w_in):
    d = w_in.shape[0]
    o_q, o_k, o_v, o_gf, o_gb, o_r, o_cq, o_ckv, o_kr = 0, 256, 512, 1024, 1040, 1056, 1568, 1824, 1952
    z = lambda n: jnp.zeros((d, n), w_in.dtype)
    kr = w_in[:, o_kr:o_kr + MLA_ROPE]
    cols = [w_in[:, o_q:o_k], w_in[:, o_k:o_v], w_in[:, o_v:o_gf], w_in[:, o_r:o_cq], w_in[:, o_cq:o_ckv],
            w_in[:, o_ckv:o_kr],
            w_in[:, o_gf:o_gb], w_in[:, o_gb:o_r], z(32), kr, z(32),
            z(64), kr[:, _ROPE_SWAP], z(32)]
    return jnp.concatenate(cols, axis=1).astype(BF16)


def _pack_gate_w(w_gk, lane0):
    out = jnp.zeros((LANES, w_gk.shape[1]), w_gk.dtype)
    return out.at[lane0:lane0 + GLA_RANK].set(w_gk).astype(BF16)


def _pack_w_uq(w_uq):
    r = w_uq.shape[0]
    w = w_uq.reshape(r, MLA_HEADS, MLA_NOPE + MLA_ROPE)
    nope, rope = w[..., :MLA_NOPE], w[..., MLA_NOPE:]
    z32 = jnp.zeros((r, MLA_HEADS, 32), w_uq.dtype)
    z64 = jnp.zeros((r, MLA_HEADS, 64), w_uq.dtype)
    a = jnp.concatenate([nope, rope, z32], axis=-1).reshape(r, MLA_HEADS * LANES)
    b = jnp.concatenate([z64, rope[..., _ROPE_SWAP], z32], axis=-1).reshape(r, MLA_HEADS * LANES)
    return jnp.concatenate([a, b], axis=1).astype(BF16)


def _pack_w_ukv(w_ukv):
    r = w_ukv.shape[0]
    w = w_ukv.reshape(r, MLA_HEADS, MLA_NOPE + MLA_DV)
    kn, v = w[..., :MLA_NOPE], w[..., MLA_NOPE:]
    k_t = jnp.concatenate([kn, jnp.zeros_like(kn)], axis=-1).reshape(r, MLA_HEADS * LANES)
    v_t = v.reshape(r, MLA_HEADS * MLA_DV)
    return jnp.concatenate([k_t, v_t], axis=1).astype(BF16)


def _pack_router(w_rg, b_rg, w_re, b_re):
    d = w_rg.shape[0]
    pad = LANES - N_GROUPS - N_EXPERTS
    w = jnp.concatenate([w_rg, w_re, jnp.zeros((d, pad), w_rg.dtype)], axis=1)
    b = jnp.concatenate([b_rg, b_re, jnp.zeros((pad,), b_rg.dtype)]).reshape(1, LANES)
    return w, b


def kernel(x, c, ctx, c_ctx, w_ada, b_ada, w_in, w_gk_f, b_gk_f, w_gk_b, b_gk_b, gla_norm_g, mla_q_norm_g, w_uq, mla_kv_norm_g, w_ukv, w_o, ln1_g, ln1_b, w_router_group, b_router_group, w_router_expert, b_router_expert, w_expert_gate, w_expert_up, w_expert_down, ln2_g, ln2_b):
    bsz, seq, d = x.shape
    l = 0

    c_all = jnp.concatenate([c, c_ctx[None, :], jnp.zeros((16 - bsz - 1, d), c.dtype)], axis=0)
    mod = _ada(c_all, w_ada[l], b_ada[l])
    mod_lat = mod[:bsz].reshape(bsz, 6, d)
    mod_ctx = mod[bsz:bsz + 1].reshape(1, 6, d)

    w_in_p = _pack_w_in(w_in[l])
    main_lat, misc_lat = _inproj(x, mod_lat, w_in_p, True, 512)
    main_ctx, misc_ctx = _inproj(ctx, mod_ctx, w_in_p, False, ctx.shape[1])

    wgf = _pack_gate_w(w_gk_f[l], 0)
    wgb = _pack_gate_w(w_gk_b[l], GLA_RANK)
    bgf = b_gk_f[l].reshape(1, GLA_QK)
    bgb = b_gk_b[l].reshape(1, GLA_QK)
    s_zero = jnp.zeros((bsz, 2, GLA_WIDTH, GLA_QK), F32)
    _, _, s_ctx = _gla(main_ctx, misc_ctx, wgf, bgf, wgb, bgb, s_zero, ctx.shape[1])
    o_f, o_b, _ = _gla(main_lat, misc_lat, wgf, bgf, wgb, bgb, s_ctx, 512)

    cos_t, sin_t = _rope_tables(seq)
    qg = mla_q_norm_g[l].reshape(1, MLA_Q_RANK)
    kvg = mla_kv_norm_g[l].reshape(1, MLA_KV_RANK)
    wq_p = _pack_w_uq(w_uq[l])
    wkv_p = _pack_w_ukv(w_ukv[l])
    q_lat, k_lat, v_lat = _mla_proj_lat(main_lat, misc_lat, cos_t, sin_t, qg, kvg, wq_p, wkv_p, 512)
    k_ctx, v_ctx = _mla_proj_ctx(main_ctx, misc_ctx, kvg, wkv_p)
    m_lat = _attn(q_lat, k_lat, k_ctx, v_lat, v_ctx, 256, 512)

    gg = gla_norm_g[l].reshape(1, GLA_DV)
    w_r, b_r = _pack_router(w_router_group[l], b_router_group[l], w_router_expert[l], b_router_expert[l])
    x1, h, comb = _outproj(o_f, o_b, main_lat, m_lat, x, mod_lat, gg, w_o[l].astype(BF16),
                           ln1_g[l].reshape(1, d), ln1_b[l].reshape(1, d), w_r, b_r, 256)

    wgu = jnp.concatenate([w_expert_gate[l], w_expert_up[l]], axis=-1).astype(BF16)
    wd = w_expert_down[l].astype(BF16)
    return _moe(h, comb, x1, mod_lat, wgu, wd, ln2_g[l].reshape(1, d), ln2_b[l].reshape(1, d), 512)
```

```python
import functools
import math

import jax
import jax.numpy as jnp
import numpy as np
from jax import lax
from jax.experimental import pallas as pl
from jax.experimental.pallas import tpu as pltpu

F32 = jnp.float32
BF16 = jnp.bfloat16

D_MODEL = 1024
GRID_W = 64
GLA_HEADS = 4
GLA_DK = 64
GLA_DV = 128
GLA_RANK = 16
GLA_GATE_NORM = 16.0
GLA_CHUNK = 64
GLA_QK = GLA_HEADS * GLA_DK
GLA_WIDTH = GLA_HEADS * GLA_DV
MLA_HEADS = 8
MLA_NOPE = 64
MLA_ROPE = 32
MLA_DV = 64
MLA_Q_RANK = 256
MLA_KV_RANK = 128
MLA_PAIRS = MLA_HEADS // 2
ROPE_BASE = 10000.0
N_GROUPS = 4
EXPERTS_PER_GROUP = 4
N_EXPERTS = 16
D_EXPERT = 256
DEPTH = 1
DEEPNORM_ALPHA = (2.0 * DEPTH) ** 0.25
EPS = 1e-6

LANES = 128
COL_Q, COL_K, COL_V, COL_R, COL_CQ, COL_CKV = 0, 256, 512, 1024, 1536, 1792
MAIN_W = 1920
MISC_W = 256
ROPE_LANE0 = 64
V_TILE = 2 * LANES
ROUTER_LANE0 = N_GROUPS
VMEM_LIMIT = 48 * 1024 * 1024
MOE_VMEM_LIMIT = 56 * 1024 * 1024
BF16_SUBLANES = 16
MOE_ROW_BLOCK = 128


def _cparams(sem):
    return pltpu.CompilerParams(dimension_semantics=sem, vmem_limit_bytes=VMEM_LIMIT)


def _dot(a, b):
    return jnp.dot(a, b, preferred_element_type=F32)


def _dot_nt(a, b):
    return lax.dot_general(a, b, (((1,), (1,)), ((), ())), preferred_element_type=F32)


def _dot_tn(a, b):
    return lax.dot_general(a, b, (((0,), (0,)), ((), ())), preferred_element_type=F32)


def _sigmoid(x):
    return 1.0 / (1.0 + jnp.exp(-x))


def _ada_kernel(c_ref, w_ref, b_ref, o_ref):
    a = c_ref[...]
    a = a * _sigmoid(a)
    o_ref[...] = _dot(a.astype(BF16), w_ref[...].astype(BF16)) + b_ref[...]


def _ada(c_all, w, b):
    rows, d = c_all.shape
    n = w.shape[1]
    bn = 1536
    return pl.pallas_call(
        _ada_kernel,
        grid=(n // bn,),
        in_specs=[pl.BlockSpec((rows, d), lambda j: (0, 0)),
                  pl.BlockSpec((d, bn), lambda j: (0, j)),
                  pl.BlockSpec((1, bn), lambda j: (0, j))],
        out_specs=pl.BlockSpec((rows, bn), lambda j: (0, j)),
        out_shape=jax.ShapeDtypeStruct((rows, n), F32),
        compiler_params=_cparams(("parallel",)),
        name="ada",
    )(c_all, w, b.reshape(1, n))


def _rope_tab_kernel(cos_ref, sin_ref):
    shape = cos_ref.shape
    t = lax.broadcasted_iota(jnp.int32, shape, 0) + pl.program_id(0) * shape[0]
    lane = lax.broadcasted_iota(jnp.int32, shape, 1)
    j = lane - ROPE_LANE0
    valid = (j >= 0) & (j < MLA_ROPE)
    f = (j & 7).astype(F32)
    inv_freq = jnp.exp(f * (-math.log(ROPE_BASE) / 8.0))
    pos = jnp.where(j >= 16, t & (GRID_W - 1), jnp.right_shift(t, int(math.log2(GRID_W)))).astype(F32)
    ang = pos * inv_freq
    sign = jnp.where((j & 15) < 8, -1.0, 1.0)
    cos_ref[...] = jnp.where(valid, jnp.cos(ang), 0.0)
    sin_ref[...] = jnp.where(valid, sign * jnp.sin(ang), 0.0)


def _rope_tables(seq):
    tm = 512
    spec = pl.BlockSpec((tm, LANES), lambda i: (i, 0))
    return pl.pallas_call(
        _rope_tab_kernel,
        grid=(seq // tm,),
        out_specs=[spec, spec],
        out_shape=(jax.ShapeDtypeStruct((seq, LANES), F32), jax.ShapeDtypeStruct((seq, LANES), F32)),
        compiler_params=_cparams(("parallel",)),
        name="rope_tab",
    )()


def _inproj_kernel(x_ref, mod_ref, w_ref, main_ref, misc_ref):
    shift = mod_ref[0:1, :]
    scale = mod_ref[1:2, :]
    u = (x_ref[...] * (1.0 + scale) + shift).astype(BF16)
    for c0 in range(0, MAIN_W, 384):
        main_ref[:, c0:c0 + 384] = _dot(u, w_ref[:, c0:c0 + 384]).astype(BF16)
    misc_ref[...] = _dot(u, w_ref[:, MAIN_W:MAIN_W + MISC_W])


def _inproj(x, mod, w, per_batch, tm):
    bsz, t, d = x.shape
    mod_map = (lambda b, i: (b, 0, 0)) if per_batch else (lambda b, i: (0, 0, 0))
    return pl.pallas_call(
        _inproj_kernel,
        grid=(bsz, t // tm),
        in_specs=[pl.BlockSpec((None, tm, d), lambda b, i: (b, i, 0)),
                  pl.BlockSpec((None, 6, d), mod_map),
                  pl.BlockSpec(w.shape, lambda b, i: (0, 0))],
        out_specs=[pl.BlockSpec((None, tm, MAIN_W), lambda b, i: (b, i, 0)),
                   pl.BlockSpec((None, tm, MISC_W), lambda b, i: (b, i, 0))],
        out_shape=(jax.ShapeDtypeStruct((bsz, t, MAIN_W), BF16),
                   jax.ShapeDtypeStruct((bsz, t, MISC_W), F32)),
        compiler_params=_cparams(("parallel", "parallel")),
        name="inproj",
    )(x, mod, w)


def _gla_kernel(qkf_ref, vf_ref, mf_ref, qkb_ref, vb_ref, mb_ref, wgf_ref, bgf_ref, wgb_ref, bgb_ref,
                s0_ref, of_ref, ob_ref, sfin_ref, st_f, st_b, zf_scr, zb_scr, *, n_chunks):
    i = pl.program_id(1)
    nblk = pl.num_programs(1)
    C = GLA_CHUNK

    @pl.when(i == 0)
    def _():
        st_f[...] = s0_ref[0]
        st_b[...] = s0_ref[1]

    zf_scr[...] = _dot(mf_ref[...].astype(BF16), wgf_ref[...]) + bgf_ref[...]
    zb_scr[...] = _dot(mb_ref[...].astype(BF16), wgb_ref[...]) + bgb_ref[...]

    def one_chunk(qk_ref, v_ref, z_scr, o_ref, st, r0, forward):
        rows = pl.ds(r0, C)
        r64 = lax.broadcasted_iota(jnp.int32, (C, C), 0)
        c64 = lax.broadcasted_iota(jnp.int32, (C, C), 1)
        ra = lax.broadcasted_iota(jnp.int32, (GLA_HEADS * C, C), 0) & (C - 1)
        ca = lax.broadcasted_iota(jnp.int32, (GLA_HEADS * C, C), 1)
        if forward:
            tri = jnp.where(c64 <= r64, 1.0, 0.0).astype(BF16)
            causal = ca <= ra
            last_row = C - 1
        else:
            tri = jnp.where(c64 >= r64, 1.0, 0.0).astype(BF16)
            causal = ca >= ra
            last_row = 0
        lane_head = jnp.right_shift(lax.broadcasted_iota(jnp.int32, (C, GLA_QK), 1), 6)
        head_masks = [jnp.where(lane_head == h, 1.0, 0.0) for h in range(GLA_HEADS)]
        z = z_scr[rows, :]
        lg = (jnp.minimum(z, 0.0) - jnp.log(1.0 + jnp.exp(-jnp.abs(z)))) * (1.0 / GLA_GATE_NORM)
        lg_hi = lg.astype(BF16)
        lg_lo = (lg - lg_hi.astype(F32)).astype(BF16)
        b = _dot(tri, lg_hi) + _dot(tri, lg_lo)
        tot = b[last_row:last_row + 1, :]
        q = qk_ref[rows, 0:GLA_QK].astype(F32)
        k = qk_ref[rows, GLA_QK:2 * GLA_QK].astype(F32)
        v = v_ref[rows, :]
        q_e = q * (jnp.exp(b) * (GLA_DK ** -0.5))
        k_e = (k * jnp.exp(-b)).astype(BF16)
        k_dec = (k * jnp.exp(tot - b)).astype(BF16)
        dec = jnp.exp(tot)
        qm = jnp.concatenate([(q_e * head_masks[h]).astype(BF16) for h in range(GLA_HEADS)], axis=0)
        a = _dot_nt(qm, k_e)
        a = jnp.where(causal, a, 0.0).astype(BF16)
        st_b16 = st[...].astype(BF16)
        outs = []
        for h in range(GLA_HEADS):
            o_h = _dot(a[h * C:(h + 1) * C, :], v[:, h * GLA_DV:(h + 1) * GLA_DV])
            o_h = o_h + _dot_nt(qm[h * C:(h + 1) * C, :], st_b16[h * GLA_DV:(h + 1) * GLA_DV, :])
            outs.append(o_h)
        o_ref[rows, :] = jnp.concatenate(outs, axis=1)
        st[...] = st[...] * dec + _dot_tn(v, k_dec)

    def body(c, carry):
        r_f = pl.multiple_of(c * C, C)
        r_b = pl.multiple_of((n_chunks - 1 - c) * C, C)
        one_chunk(qkf_ref, vf_ref, zf_scr, of_ref, st_f, r_f, True)
        one_chunk(qkb_ref, vb_ref, zb_scr, ob_ref, st_b, r_b, False)
        return carry

    lax.fori_loop(0, n_chunks, body, 0)

    @pl.when(i == nblk - 1)
    def _():
        sfin_ref[0] = st_f[...]
        sfin_ref[1] = st_b[...]


def _gla(main, misc, wgf, bgf, wgb, bgb, s0, tm):
    bsz, t, _ = main.shape
    nblk = t // tm
    fwd = lambda b, i: (b, i, 0)
    bwd = lambda b, i: (b, nblk - 1 - i, 0)
    const2 = lambda b, i: (0, 0)
    kern = functools.partial(_gla_kernel, n_chunks=tm // GLA_CHUNK)
    return pl.pallas_call(
        kern,
        grid=(bsz, nblk),
        in_specs=[pl.BlockSpec((None, tm, 2 * GLA_QK), fwd),
                  pl.BlockSpec((None, tm, GLA_WIDTH), lambda b, i: (b, i, 1)),
                  pl.BlockSpec((None, tm, LANES), fwd),
                  pl.BlockSpec((None, tm, 2 * GLA_QK), bwd),
                  pl.BlockSpec((None, tm, GLA_WIDTH), lambda b, i: (b, nblk - 1 - i, 1)),
                  pl.BlockSpec((None, tm, LANES), bwd),
                  pl.BlockSpec(wgf.shape, const2), pl.BlockSpec(bgf.shape, const2),
                  pl.BlockSpec(wgb.shape, const2), pl.BlockSpec(bgb.shape, const2),
                  pl.BlockSpec((None, 2, GLA_WIDTH, GLA_QK), lambda b, i: (b, 0, 0, 0))],
        out_specs=[pl.BlockSpec((None, tm, GLA_WIDTH), fwd),
                   pl.BlockSpec((None, tm, GLA_WIDTH), bwd),
                   pl.BlockSpec((None, 2, GLA_WIDTH, GLA_QK), lambda b, i: (b, 0, 0, 0))],
        out_shape=(jax.ShapeDtypeStruct((bsz, t, GLA_WIDTH), F32),
                   jax.ShapeDtypeStruct((bsz, t, GLA_WIDTH), F32),
                   jax.ShapeDtypeStruct((bsz, 2, GLA_WIDTH, GLA_QK), F32)),
        scratch_shapes=[pltpu.VMEM((GLA_WIDTH, GLA_QK), F32), pltpu.VMEM((GLA_WIDTH, GLA_QK), F32),
                        pltpu.VMEM((tm, GLA_QK), F32), pltpu.VMEM((tm, GLA_QK), F32)],
        compiler_params=_cparams(("parallel", "arbitrary")),
        name="gla",
    )(main, main, misc, main, main, misc, wgf, bgf, wgb, bgb, s0)


def _rmsnorm_rows(x, g):
    xf = x.astype(F32)
    ms = jnp.mean(xf * xf, axis=-1, keepdims=True)
    return (xf * lax.rsqrt(ms + EPS)) * g


def _mla_proj_kernel(*refs, rotate, with_q):
    if with_q:
        (cq_ref, ckv_ref, m0_ref, m1_ref, cos_ref, sin_ref, qg_ref, kvg_ref, wq_ref, wkv_ref,
         q_out, k_out, v_out) = refs
    else:
        ckv_ref, m0_ref, kvg_ref, wkv_ref, k_out, v_out = refs
    hw = MLA_HEADS * LANES
    lane = lax.broadcasted_iota(jnp.int32, m0_ref.shape, 1)
    rope_lanes = (lane >= ROPE_LANE0) & (lane < ROPE_LANE0 + MLA_ROPE)
    if rotate:
        cos = cos_ref[...]
        sin = sin_ref[...]
        kr = m0_ref[...] * cos + m1_ref[...] * sin
    else:
        kr = jnp.where(rope_lanes, m0_ref[...], 0.0)
    kv = _dot(_rmsnorm_rows(ckv_ref[...], kvg_ref[...]).astype(BF16), wkv_ref[...])
    for h in range(MLA_HEADS):
        k_out[h] = (kv[:, h * LANES:(h + 1) * LANES] + kr).astype(BF16)
    ones = jnp.ones((kv.shape[0], LANES), BF16)
    for p in range(MLA_PAIRS):
        v_out[p, :, 0:LANES] = kv[:, hw + p * LANES:hw + (p + 1) * LANES].astype(BF16)
        v_out[p, :, LANES:V_TILE] = ones
    if with_q:
        qs = (MLA_NOPE + MLA_ROPE) ** -0.5 * math.log2(math.e)
        cq_tab = jnp.where(lane < MLA_NOPE, qs, cos * qs)
        sq_tab = sin * qs
        qq = _dot(_rmsnorm_rows(cq_ref[...], qg_ref[...]).astype(BF16), wq_ref[...])
        for h in range(MLA_HEADS):
            qa = qq[:, h * LANES:(h + 1) * LANES]
            qb = qq[:, hw + h * LANES:hw + (h + 1) * LANES]
            q_out[h] = (qa * cq_tab + qb * sq_tab).astype(BF16)


def _mla_proj_lat(main, misc, cos_t, sin_t, qg, kvg, wq, wkv, tm):
    bsz, t, _ = main.shape
    c2 = lambda b, i: (0, 0)
    kern = functools.partial(_mla_proj_kernel, rotate=True, with_q=True)
    return pl.pallas_call(
        kern,
        grid=(bsz, t // tm),
        in_specs=[pl.BlockSpec((None, tm, MLA_Q_RANK), lambda b, i: (b, i, COL_CQ // MLA_Q_RANK)),
                  pl.BlockSpec((None, tm, MLA_KV_RANK), lambda b, i: (b, i, COL_CKV // MLA_KV_RANK)),
                  pl.BlockSpec((None, tm, LANES), lambda b, i: (b, i, 0)),
                  pl.BlockSpec((None, tm, LANES), lambda b, i: (b, i, 1)),
                  pl.BlockSpec((tm, LANES), lambda b, i: (i, 0)),
                  pl.BlockSpec((tm, LANES), lambda b, i: (i, 0)),
                  pl.BlockSpec(qg.shape, c2), pl.BlockSpec(kvg.shape, c2),
                  pl.BlockSpec(wq.shape, c2), pl.BlockSpec(wkv.shape, c2)],
        out_specs=[pl.BlockSpec((None, MLA_HEADS, tm, LANES), lambda b, i: (b, 0, i, 0)),
                   pl.BlockSpec((None, MLA_HEADS, tm, LANES), lambda b, i: (b, 0, i, 0)),
                   pl.BlockSpec((None, MLA_PAIRS, tm, V_TILE), lambda b, i: (b, 0, i, 0))],
        out_shape=(jax.ShapeDtypeStruct((bsz, MLA_HEADS, t, LANES), BF16),
                   jax.ShapeDtypeStruct((bsz, MLA_HEADS, t, LANES), BF16),
                   jax.ShapeDtypeStruct((bsz, MLA_PAIRS, t, V_TILE), BF16)),
        compiler_params=_cparams(("parallel", "parallel")),
        name="mla_proj_lat",
    )(main, main, misc, misc, cos_t, sin_t, qg, kvg, wq, wkv)


def _mla_proj_ctx(main, misc, kvg, wkv):
    bsz, t, _ = main.shape
    c2 = lambda b: (0, 0)
    kern = functools.partial(_mla_proj_kernel, rotate=False, with_q=False)
    return pl.pallas_call(
        kern,
        grid=(bsz,),
        in_specs=[pl.BlockSpec((None, t, MLA_KV_RANK), lambda b: (b, 0, COL_CKV // MLA_KV_RANK)),
                  pl.BlockSpec((None, t, LANES), lambda b: (b, 0, 0)),
                  pl.BlockSpec(kvg.shape, c2), pl.BlockSpec(wkv.shape, c2)],
        out_specs=[pl.BlockSpec((None, MLA_HEADS, t, LANES), lambda b: (b, 0, 0, 0)),
                   pl.BlockSpec((None, MLA_PAIRS, t, V_TILE), lambda b: (b, 0, 0, 0))],
        out_shape=(jax.ShapeDtypeStruct((bsz, MLA_HEADS, t, LANES), BF16),
                   jax.ShapeDtypeStruct((bsz, MLA_PAIRS, t, V_TILE), BF16)),
        compiler_params=_cparams(("parallel",)),
        name="mla_proj_ctx",
    )(main, misc, kvg, wkv)


def _attn_kernel(q_ref, kl_ref, kc_ref, vl_ref, vc_ref, o_ref, s_scr, m_scr, p_scr, o_scr, *, tq, tk):
    s_len = kl_ref.shape[1]
    c_len = kc_ref.shape[1]
    nq = s_len // tq
    n_items = 2 * nq

    def item_coords(i):
        hh = i // nq
        r0 = pl.multiple_of((i - hh * nq) * tq, tq)
        return hh, r0, i & 1

    def pass1(i):
        hh, r0, buf = item_coords(i)
        q = q_ref[hh, pl.ds(r0, tq), :]
        mrun = jnp.full((tq, LANES), -jnp.inf, F32)
        for k_ref, c0, n, col in ([(kl_ref, c0, tk, c0) for c0 in range(0, s_len, tk)]
                                  + [(kc_ref, 0, c_len, s_len)]):
            s = _dot_nt(q, k_ref[hh, c0:c0 + n, :])
            s_scr[buf, :, col:col + n] = s
            for t0 in range(0, n, LANES):
                mrun = jnp.maximum(mrun, s[:, t0:t0 + LANES])
        m = jnp.max(mrun, axis=-1, keepdims=True)
        m_scr[buf] = jnp.broadcast_to(m, (tq, LANES))

    def pass2(i):
        hh, r0, buf = item_coords(i)
        m = m_scr[buf]
        for t0 in range(0, s_len + c_len, LANES):
            p_scr[:, t0:t0 + LANES] = jnp.exp2(s_scr[buf, :, t0:t0 + LANES] - m).astype(BF16)
        acc = _dot(p_scr[:, 0:s_len], vl_ref[...]) + _dot(p_scr[:, s_len:s_len + c_len], vc_ref[...])
        o_scr[hh, pl.ds(r0, tq), :] = acc[:, 0:LANES] * (1.0 / acc[:, LANES:V_TILE])

    pass1(0)

    def body(i, carry):
        pass2(i - 1)
        pass1(i)
        return carry

    lax.fori_loop(1, n_items, body, 0)
    pass2(n_items - 1)
    lane = lax.broadcasted_iota(jnp.int32, o_ref.shape, 1)
    o_ref[...] = jnp.where(lane < MLA_DV, o_scr[0], o_scr[1]).astype(BF16)


def _attn(q, k_lat, k_ctx, v_lat, v_ctx, tq, tk):
    bsz, _, s_len, _ = q.shape
    c_len = k_ctx.shape[2]
    kern = functools.partial(_attn_kernel, tq=tq, tk=tk)
    return pl.pallas_call(
        kern,
        grid=(bsz, MLA_PAIRS),
        in_specs=[pl.BlockSpec((None, 2, s_len, LANES), lambda b, p: (b, p, 0, 0)),
                  pl.BlockSpec((None, 2, s_len, LANES), lambda b, p: (b, p, 0, 0)),
                  pl.BlockSpec((None, 2, c_len, LANES), lambda b, p: (b, p, 0, 0)),
                  pl.BlockSpec((None, None, s_len, V_TILE), lambda b, p: (b, p, 0, 0)),
                  pl.BlockSpec((None, None, c_len, V_TILE), lambda b, p: (b, p, 0, 0))],
        out_specs=pl.BlockSpec((None, None, s_len, LANES), lambda b, p: (b, p, 0, 0)),
        out_shape=jax.ShapeDtypeStruct((bsz, MLA_PAIRS, s_len, LANES), BF16),
        scratch_shapes=[pltpu.VMEM((2, tq, s_len + c_len), F32),
                        pltpu.VMEM((2, tq, LANES), F32),
                        pltpu.VMEM((tq, s_len + c_len), BF16),
                        pltpu.VMEM((2, s_len, LANES), F32)],
        compiler_params=_cparams(("parallel", "parallel")),
        name="attn",
    )(q, k_lat, k_ctx, v_lat, v_ctx)


def _layernorm_rows(z, g, b):
    mu = jnp.mean(z, axis=-1, keepdims=True)
    zc = z - mu
    var = jnp.mean(zc * zc, axis=-1, keepdims=True)
    return (zc * lax.rsqrt(var + EPS)) * g + b


def _outproj_kernel(of_ref, ob_ref, r_ref, ml_ref, x_ref, mod_ref, gg_ref, wo_ref, l1g_ref, l1b_ref,
                    wr_ref, br_ref, x1_ref, h_ref, comb_ref):
    tm = x_ref.shape[0]
    o = of_ref[...] + ob_ref[...]
    r = r_ref[...].astype(F32)
    gg = gg_ref[...]
    y = jnp.zeros((tm, D_MODEL), F32)
    for h in range(GLA_HEADS):
        sl = slice(h * GLA_DV, (h + 1) * GLA_DV)
        oh = o[:, sl]
        ms = jnp.mean(oh * oh, axis=-1, keepdims=True)
        rh = r[:, sl]
        gh = (oh * lax.rsqrt(ms + EPS)) * gg * (rh * _sigmoid(rh))
        y = y + _dot(gh.astype(BF16), wo_ref[sl, :])
    for p in range(MLA_PAIRS):
        y = y + _dot(ml_ref[p], wo_ref[GLA_WIDTH + p * LANES:GLA_WIDTH + (p + 1) * LANES, :])
    gate1 = mod_ref[2:3, :]
    x1 = _layernorm_rows(DEEPNORM_ALPHA * x_ref[...] + gate1 * y, l1g_ref[...], l1b_ref[...])
    x1_ref[...] = x1
    hmod = x1 * (1.0 + mod_ref[4:5, :]) + mod_ref[3:4, :]
    h_ref[...] = hmod.astype(BF16)

    h_hi = hmod.astype(BF16)
    h_lo = (hmod - h_hi.astype(F32)).astype(BF16)
    wr = wr_ref[...]
    w_hi = wr.astype(BF16)
    w_lo = (wr - w_hi.astype(F32)).astype(BF16)
    logits = _dot(h_hi, w_hi) + _dot(h_hi, w_lo) + _dot(h_lo, w_hi) + br_ref[...]

    lane = lax.broadcasted_iota(jnp.int32, (tm, LANES), 1).astype(F32)
    neg = -jnp.inf
    far = float(LANES)
    gl = jnp.where(lane < N_GROUPS, logits, neg)
    gmax = jnp.max(gl, axis=-1, keepdims=True)
    gsum = jnp.sum(jnp.exp(gl - gmax), axis=-1, keepdims=True)
    p_g = 1.0 / gsum
    g_top = jnp.min(jnp.where(gl == gmax, lane, far), axis=-1, keepdims=True)
    e0 = ROUTER_LANE0 + g_top * EXPERTS_PER_GROUP
    el = jnp.where((lane >= e0) & (lane < e0 + EXPERTS_PER_GROUP), logits, neg)
    e1max = jnp.max(el, axis=-1, keepdims=True)
    i1 = jnp.min(jnp.where(el == e1max, lane, far), axis=-1, keepdims=True)
    el2 = jnp.where(lane == i1, neg, el)
    e2max = jnp.max(el2, axis=-1, keepdims=True)
    i2 = jnp.min(jnp.where(el2 == e2max, lane, far), axis=-1, keepdims=True)
    t = jnp.exp(e2max - e1max)
    w1 = p_g / (1.0 + t)
    w2 = w1 * t
    comb_ref[...] = jnp.where(lane == i1, w1, jnp.where(lane == i2, w2, jnp.where(lane == g_top, 1.0, 0.0)))


def _outproj(o_f, o_b, main, mlat, x, mod, gg, wo, l1g, l1b, wr, br, tm):
    bsz, t, d = x.shape
    c2 = lambda b, i: (0, 0)
    row = lambda b, i: (b, i, 0)
    return pl.pallas_call(
        _outproj_kernel,
        grid=(bsz, t // tm),
        in_specs=[pl.BlockSpec((None, tm, GLA_WIDTH), row),
                  pl.BlockSpec((None, tm, GLA_WIDTH), row),
                  pl.BlockSpec((None, tm, GLA_WIDTH), lambda b, i: (b, i, COL_R // GLA_WIDTH)),
                  pl.BlockSpec((None, MLA_PAIRS, tm, LANES), lambda b, i: (b, 0, i, 0)),
                  pl.BlockSpec((None, tm, d), row),
                  pl.BlockSpec((None, 6, d), lambda b, i: (b, 0, 0)),
                  pl.BlockSpec(gg.shape, c2), pl.BlockSpec(wo.shape, c2),
                  pl.BlockSpec(l1g.shape, c2), pl.BlockSpec(l1b.shape, c2),
                  pl.BlockSpec(wr.shape, c2), pl.BlockSpec(br.shape, c2)],
        out_specs=[pl.BlockSpec((None, tm, d), row),
                   pl.BlockSpec((None, tm, d), row),
                   pl.BlockSpec((None, tm, LANES), row)],
        out_shape=(jax.ShapeDtypeStruct((bsz, t, d), F32),
                   jax.ShapeDtypeStruct((bsz, t, d), BF16),
                   jax.ShapeDtypeStruct((bsz, t, LANES), F32)),
        compiler_params=_cparams(("parallel", "parallel")),
        name="outproj",
    )(o_f, o_b, main, mlat, x, mod, gg, wo, l1g, l1b, wr, br)


def _moe_kernel(h_ref, comb_ref, x1_ref, mod_ref, wgu_ref, wd_ref, l2g_ref, l2b_ref, o_ref,
                hs_scr, cs_scr, acc_scr):
    tm = h_ref.shape[0]
    rb = MOE_ROW_BLOCK
    gw = EXPERTS_PER_GROUP * D_EXPERT
    comb = comb_ref[...]
    lane = lax.broadcasted_iota(jnp.int32, (tm, LANES), 1)
    onehot = jnp.where(lane < N_GROUPS, comb, 0.0)

    ri = lax.broadcasted_iota(jnp.int32, (tm, tm), 0)
    ci = lax.broadcasted_iota(jnp.int32, (tm, tm), 1)
    lower = jnp.where(ci < ri, 1.0, 0.0).astype(BF16)
    before = _dot(lower, onehot.astype(BF16))
    rank = jnp.sum(before * onehot, axis=-1, keepdims=True)
    totals = jnp.broadcast_to(jnp.sum(onehot, axis=0, keepdims=True), (8, LANES))
    offs = pltpu.roll(totals, 1, 1) + pltpu.roll(totals, 2, 1) + pltpu.roll(totals, 3, 1)
    pos = jnp.sum(onehot * offs[0:1, :], axis=-1, keepdims=True) + rank
    pt = jnp.where(ci.astype(F32) == pos, 1.0, 0.0).astype(BF16)

    hs_scr[0:tm, :] = _dot_tn(pt, h_ref[...]).astype(BF16)
    hs_scr[tm:tm + rb, :] = jnp.zeros((rb, hs_scr.shape[1]), BF16)
    c_hi = comb.astype(BF16)
    c_lo = (comb - c_hi.astype(F32)).astype(BF16)
    cs_scr[0:tm, :] = _dot_tn(pt, c_hi) + _dot_tn(pt, c_lo)
    cs_scr[tm:tm + rb, :] = jnp.zeros((rb, LANES), F32)
    acc_scr[...] = jnp.zeros_like(acc_scr)

    tot_i = totals.astype(jnp.int32)
    off_i = offs.astype(jnp.int32)
    lane_r = lax.broadcasted_iota(jnp.int32, (rb, LANES), 1)
    for g in range(N_GROUPS):
        n_g = tot_i[0, g]
        start = off_i[0, g]
        first = (start // BF16_SUBLANES) * BF16_SUBLANES
        n_blocks = jnp.where(n_g > 0, (start + n_g - first + rb - 1) // rb, 0)

        def block(k, carry, g=g, first=first):
            rows = pl.ds(pl.multiple_of(first + k * rb, BF16_SUBLANES), rb)
            gu = _dot(hs_scr[rows, :], wgu_ref[g])
            cb = cs_scr[rows, :]
            parts = []
            for j in range(EXPERTS_PER_GROUP):
                e_lane = ROUTER_LANE0 + g * EXPERTS_PER_GROUP + j
                w = jnp.sum(jnp.where(lane_r == e_lane, cb, 0.0), axis=-1, keepdims=True)
                gj = gu[:, j * D_EXPERT:(j + 1) * D_EXPERT]
                uj = gu[:, gw + j * D_EXPERT:gw + (j + 1) * D_EXPERT]
                parts.append(((gj * _sigmoid(gj)) * uj * w).astype(BF16))
            acc_scr[rows, :] += _dot(jnp.concatenate(parts, axis=1), wd_ref[g])
            return carry

        lax.fori_loop(0, n_blocks, block, 0)

    y = _dot(pt, acc_scr[0:tm, :].astype(BF16))
    gate2 = mod_ref[5:6, :]
    z = DEEPNORM_ALPHA * x1_ref[...] + gate2 * y
    o_ref[...] = _layernorm_rows(z, l2g_ref[...], l2b_ref[...])


def _moe(h, comb, x1, mod, wgu, wd, l2g, l2b, tm):
    bsz, t, d = x1.shape
    row = lambda b, i: (b, i, 0)
    c2 = lambda b, i: (0, 0)
    c3 = lambda b, i: (0, 0, 0)
    resident = pl.Buffered(1)
    return pl.pallas_call(
        _moe_kernel,
        grid=(bsz, t // tm),
        in_specs=[pl.BlockSpec((None, tm, d), row),
                  pl.BlockSpec((None, tm, LANES), row),
                  pl.BlockSpec((None, tm, d), row),
                  pl.BlockSpec((None, 6, d), lambda b, i: (b, 0, 0)),
                  pl.BlockSpec(wgu.shape, c3, pipeline_mode=resident),
                  pl.BlockSpec(wd.shape, c3, pipeline_mode=resident),
                  pl.BlockSpec(l2g.shape, c2), pl.BlockSpec(l2b.shape, c2)],
        out_specs=pl.BlockSpec((None, tm, d), row),
        out_shape=jax.ShapeDtypeStruct((bsz, t, d), F32),
        scratch_shapes=[pltpu.VMEM((tm + MOE_ROW_BLOCK, d), BF16),
                        pltpu.VMEM((tm + MOE_ROW_BLOCK, LANES), F32),
                        pltpu.VMEM((tm + MOE_ROW_BLOCK, d), F32)],
        compiler_params=pltpu.CompilerParams(dimension_semantics=("parallel", "parallel"),
                                             vmem_limit_bytes=MOE_VMEM_LIMIT),
        name="moe",
    )(h, comb, x1, mod, wgu, wd, l2g, l2b)


_ROPE_SWAP = np.concatenate([np.arange(8, 16), np.arange(0, 8), np.arange(24, 32), np.arange(16, 24)])


def _pack_w_in(w_in):
    d = w_in.shape[0]
    o_q, o_k, o_v, o_gf, o_gb, o_r, o_cq, o_ckv, o_kr = 0, 256, 512, 1024, 1040, 1056, 1568, 1824, 1952
    z = lambda n: jnp.zeros((d, n), w_in.dtype)
    kr = w_in[:, o_kr:o_kr + MLA_ROPE]
    cols = [w_in[:, o_q:o_k], w_in[:, o_k:o_v], w_in[:, o_v:o_gf], w_in[:, o_r:o_cq], w_in[:, o_cq:o_ckv],
            w_in[:, o_ckv:o_kr],
            w_in[:, o_gf:o_gb], w_in[:, o_gb:o_r], z(32), kr, z(32),
            z(64), kr[:, _ROPE_SWAP], z(32)]
    return jnp.concatenate(cols, axis=1).astype(BF16)


def _pack_gate_w(w_gk, lane0):
    out = jnp.zeros((LANES, w_gk.shape[1]), w_gk.dtype)
    return out.at[lane0:lane0 + GLA_RANK].set(w_gk).astype(BF16)


def _pack_w_uq(w_uq):
    r = w_uq.shape[0]
    w = w_uq.reshape(r, MLA_HEADS, MLA_NOPE + MLA_ROPE)
    nope, rope = w[..., :MLA_NOPE], w[..., MLA_NOPE:]
    z32 = jnp.zeros((r, MLA_HEADS, 32), w_uq.dtype)
    z64 = jnp.zeros((r, MLA_HEADS, 64), w_uq.dtype)
    a = jnp.concatenate([nope, rope, z32], axis=-1).reshape(r, MLA_HEADS * LANES)
    b = jnp.concatenate([z64, rope[..., _ROPE_SWAP], z32], axis=-1).reshape(r, MLA_HEADS * LANES)
    return jnp.concatenate([a, b], axis=1).astype(BF16)


def _pack_w_ukv(w_ukv):
    r = w_ukv.shape[0]
    w = w_ukv.reshape(r, MLA_HEADS, MLA_NOPE + MLA_DV)
    kn, v = w[..., :MLA_NOPE], w[..., MLA_NOPE:]
    k_t = jnp.concatenate([kn, jnp.zeros_like(kn)], axis=-1).reshape(r, MLA_HEADS * LANES)
    v_t = v.reshape(r, MLA_HEADS * MLA_DV)
    return jnp.concatenate([k_t, v_t], axis=1).astype(BF16)


def _pack_experts(w_gate, w_up, w_down):
    n_e, d, de = w_gate.shape
    by_group = lambda w: (w.reshape(N_GROUPS, EXPERTS_PER_GROUP, d, de).transpose(0, 2, 1, 3)
                          .reshape(N_GROUPS, d, EXPERTS_PER_GROUP * de))
    wgu = jnp.concatenate([by_group(w_gate), by_group(w_up)], axis=-1).astype(BF16)
    wd = w_down.reshape(N_GROUPS, EXPERTS_PER_GROUP * de, d).astype(BF16)
    return wgu, wd


def _pack_router(w_rg, b_rg, w_re, b_re):
    d = w_rg.shape[0]
    pad = LANES - N_GROUPS - N_EXPERTS
    w = jnp.concatenate([w_rg, w_re, jnp.zeros((d, pad), w_rg.dtype)], axis=1)
    b = jnp.concatenate([b_rg, b_re, jnp.zeros((pad,), b_rg.dtype)]).reshape(1, LANES)
    return w, b


def kernel(x, c, ctx, c_ctx, w_ada, b_ada, w_in, w_gk_f, b_gk_f, w_gk_b, b_gk_b, gla_norm_g, mla_q_norm_g, w_uq, mla_kv_norm_g, w_ukv, w_o, ln1_g, ln1_b, w_router_group, b_router_group, w_router_expert, b_router_expert, w_expert_gate, w_expert_up, w_expert_down, ln2_g, ln2_b):
    bsz, seq, d = x.shape
    l = 0

    c_all = jnp.concatenate([c, c_ctx[None, :], jnp.zeros((16 - bsz - 1, d), c.dtype)], axis=0)
    mod = _ada(c_all, w_ada[l], b_ada[l])
    mod_lat = mod[:bsz].reshape(bsz, 6, d)
    mod_ctx = mod[bsz:bsz + 1].reshape(1, 6, d)

    w_in_p = _pack_w_in(w_in[l])
    main_lat, misc_lat = _inproj(x, mod_lat, w_in_p, True, 512)
    main_ctx, misc_ctx = _inproj(ctx, mod_ctx, w_in_p, False, ctx.shape[1])

    wgf = _pack_gate_w(w_gk_f[l], 0)
    wgb = _pack_gate_w(w_gk_b[l], GLA_RANK)
    bgf = b_gk_f[l].reshape(1, GLA_QK)
    bgb = b_gk_b[l].reshape(1, GLA_QK)
    s_zero = jnp.zeros((bsz, 2, GLA_WIDTH, GLA_QK), F32)
    _, _, s_ctx = _gla(main_ctx, misc_ctx, wgf, bgf, wgb, bgb, s_zero, ctx.shape[1])
    o_f, o_b, _ = _gla(main_lat, misc_lat, wgf, bgf, wgb, bgb, s_ctx, 512)

    cos_t, sin_t = _rope_tables(seq)
    qg = mla_q_norm_g[l].reshape(1, MLA_Q_RANK)
    kvg = mla_kv_norm_g[l].reshape(1, MLA_KV_RANK)
    wq_p = _pack_w_uq(w_uq[l])
    wkv_p = _pack_w_ukv(w_ukv[l])
    q_lat, k_lat, v_lat = _mla_proj_lat(main_lat, misc_lat, cos_t, sin_t, qg, kvg, wq_p, wkv_p, 512)
    k_ctx, v_ctx = _mla_proj_ctx(main_ctx, misc_ctx, kvg, wkv_p)
    m_lat = _attn(q_lat, k_lat, k_ctx, v_lat, v_ctx, 256, 512)

    gg = gla_norm_g[l].reshape(1, GLA_DV)
    w_r, b_r = _pack_router(w_router_group[l], b_router_group[l], w_router_expert[l], b_router_expert[l])
    x1, h, comb = _outproj(o_f, o_b, main_lat, m_lat, x, mod_lat, gg, w_o[l].astype(BF16),
                           ln1_g[l].reshape(1, d), ln1_b[l].reshape(1, d), w_r, b_r, 256)

    wgu, wd = _pack_experts(w_expert_gate[l], w_expert_up[l], w_expert_down[l])
    return _moe(h, comb, x1, mod_lat, wgu, wd, ln2_g[l].reshape(1, d), ln2_b[l].reshape(1, d), 512)
```

```python
import functools
import math

import jax
import jax.numpy as jnp
import numpy as np
from jax import lax
from jax.experimental import pallas as pl
from jax.experimental.pallas import tpu as pltpu

F32 = jnp.float32
BF16 = jnp.bfloat16

D_MODEL = 1024
GRID_W = 64
GLA_HEADS = 4
GLA_DK = 64
GLA_DV = 128
GLA_RANK = 16
GLA_GATE_NORM = 16.0
GLA_CHUNK = 64
GLA_QK = GLA_HEADS * GLA_DK
GLA_WIDTH = GLA_HEADS * GLA_DV
MLA_HEADS = 8
MLA_NOPE = 64
MLA_ROPE = 32
MLA_DV = 64
MLA_Q_RANK = 256
MLA_KV_RANK = 128
MLA_PAIRS = MLA_HEADS // 2
ROPE_BASE = 10000.0
N_GROUPS = 4
EXPERTS_PER_GROUP = 4
N_EXPERTS = 16
D_EXPERT = 256
DEPTH = 1
DEEPNORM_ALPHA = (2.0 * DEPTH) ** 0.25
EPS = 1e-6

LANES = 128
COL_Q, COL_K, COL_V, COL_R, COL_CQ, COL_CKV = 0, 256, 512, 1024, 1536, 1792
MAIN_W = 1920
MISC_W = 256
INPROJ_CHUNK = 768
ROPE_LANE0 = 64
V_TILE = LANES
ROUTER_LANE0 = N_GROUPS
VMEM_LIMIT = 48 * 1024 * 1024
MOE_VMEM_LIMIT = 56 * 1024 * 1024
BF16_SUBLANES = 16
MOE_ROW_BLOCK = 128
ATTN_BUFS = 2
ATTN_TB = 1024
ATTN_TQ = 256


def _cparams(sem):
    return pltpu.CompilerParams(dimension_semantics=sem, vmem_limit_bytes=VMEM_LIMIT)


def _dot(a, b):
    return jnp.dot(a, b, preferred_element_type=F32)


def _dot_nt(a, b):
    return lax.dot_general(a, b, (((1,), (1,)), ((), ())), preferred_element_type=F32)


def _dot_tn(a, b):
    return lax.dot_general(a, b, (((0,), (0,)), ((), ())), preferred_element_type=F32)


def _sigmoid(x):
    return 1.0 / (1.0 + jnp.exp(-x))


def _ada_kernel(c_ref, w_ref, b_ref, o_ref):
    a = c_ref[...]
    a = a * _sigmoid(a)
    o_ref[...] = _dot(a.astype(BF16), w_ref[...].astype(BF16)) + b_ref[...]


def _ada(c_all, w, b):
    rows, d = c_all.shape
    n = w.shape[1]
    bn = 1536
    return pl.pallas_call(
        _ada_kernel,
        grid=(n // bn,),
        in_specs=[pl.BlockSpec((rows, d), lambda j: (0, 0)),
                  pl.BlockSpec((d, bn), lambda j: (0, j)),
                  pl.BlockSpec((1, bn), lambda j: (0, j))],
        out_specs=pl.BlockSpec((rows, bn), lambda j: (0, j)),
        out_shape=jax.ShapeDtypeStruct((rows, n), F32),
        compiler_params=_cparams(("parallel",)),
        name="ada",
    )(c_all, w, b.reshape(1, n))


def _rope_tab_kernel(cos_ref, sin_ref):
    shape = cos_ref.shape
    t = lax.broadcasted_iota(jnp.int32, shape, 0) + pl.program_id(0) * shape[0]
    lane = lax.broadcasted_iota(jnp.int32, shape, 1)
    j = lane - ROPE_LANE0
    valid = (j >= 0) & (j < MLA_ROPE)
    f = (j & 7).astype(F32)
    inv_freq = jnp.exp(f * (-math.log(ROPE_BASE) / 8.0))
    pos = jnp.where(j >= 16, t & (GRID_W - 1), jnp.right_shift(t, int(math.log2(GRID_W)))).astype(F32)
    ang = pos * inv_freq
    sign = jnp.where((j & 15) < 8, -1.0, 1.0)
    cos_ref[...] = jnp.where(valid, jnp.cos(ang), 0.0)
    sin_ref[...] = jnp.where(valid, sign * jnp.sin(ang), 0.0)


def _rope_tables(seq):
    tm = 512
    spec = pl.BlockSpec((tm, LANES), lambda i: (i, 0))
    return pl.pallas_call(
        _rope_tab_kernel,
        grid=(seq // tm,),
        out_specs=[spec, spec],
        out_shape=(jax.ShapeDtypeStruct((seq, LANES), F32), jax.ShapeDtypeStruct((seq, LANES), F32)),
        compiler_params=_cparams(("parallel",)),
        name="rope_tab",
    )()


def _inproj_kernel(x_ref, mod_ref, w_ref, main_ref, misc_ref):
    shift = mod_ref[0:1, :]
    scale = mod_ref[1:2, :]
    u = (x_ref[...] * (1.0 + scale) + shift).astype(BF16)
    for c0 in range(0, MAIN_W, INPROJ_CHUNK):
        c1 = min(c0 + INPROJ_CHUNK, MAIN_W + MISC_W)
        y = _dot(u, w_ref[:, c0:c1])
        if c1 <= MAIN_W:
            main_ref[:, c0:c1] = y.astype(BF16)
        else:
            main_ref[:, c0:MAIN_W] = y[:, 0:MAIN_W - c0].astype(BF16)
            misc_ref[...] = y[:, MAIN_W - c0:c1 - c0]


def _inproj(x, mod, w, per_batch, tm):
    bsz, t, d = x.shape
    mod_map = (lambda b, i: (b, 0, 0)) if per_batch else (lambda b, i: (0, 0, 0))
    return pl.pallas_call(
        _inproj_kernel,
        grid=(bsz, t // tm),
        in_specs=[pl.BlockSpec((None, tm, d), lambda b, i: (b, i, 0)),
                  pl.BlockSpec((None, 6, d), mod_map),
                  pl.BlockSpec(w.shape, lambda b, i: (0, 0))],
        out_specs=[pl.BlockSpec((None, tm, MAIN_W), lambda b, i: (b, i, 0)),
                   pl.BlockSpec((None, tm, MISC_W), lambda b, i: (b, i, 0))],
        out_shape=(jax.ShapeDtypeStruct((bsz, t, MAIN_W), BF16),
                   jax.ShapeDtypeStruct((bsz, t, MISC_W), F32)),
        compiler_params=_cparams(("parallel", "parallel")),
        name="inproj",
    )(x, mod, w)


def _gla_kernel(qkf_ref, vf_ref, mf_ref, qkb_ref, vb_ref, mb_ref, wgf_ref, bgf_ref, wgb_ref, bgb_ref,
                s0_ref, of_ref, ob_ref, sfin_ref, st_f, st_b, dsf_scr, dsb_scr, *, n_chunks):
    i = pl.program_id(1)
    nblk = pl.num_programs(1)
    C = GLA_CHUNK

    @pl.when(i == 0)
    def _():
        st_f[...] = s0_ref[0]
        st_b[...] = s0_ref[1]

    r64 = lax.broadcasted_iota(jnp.int32, (C, C), 0)
    c64 = lax.broadcasted_iota(jnp.int32, (C, C), 1)
    ra = lax.broadcasted_iota(jnp.int32, (GLA_HEADS * C, C), 0) & (C - 1)
    ca = lax.broadcasted_iota(jnp.int32, (GLA_HEADS * C, C), 1)
    lane_head = lax.broadcasted_iota(jnp.int32, (C, GLA_QK), 1) // GLA_DK
    head_masks = [jnp.where(lane_head == h, 1.0, 0.0) for h in range(GLA_HEADS)]

    def local_part(qk_ref, v_ref, m_ref, wg_ref, bg_ref, o_ref, ds_scr, forward):
        if forward:
            tri = jnp.where(c64 <= r64, 1.0, 0.0).astype(BF16)
            causal = ca <= ra
            last_row = C - 1
        else:
            tri = jnp.where(c64 >= r64, 1.0, 0.0).astype(BF16)
            causal = ca >= ra
            last_row = 0
        z_all = _dot(m_ref[...].astype(BF16), wg_ref[...]) + bg_ref[...]
        lg_all = (jnp.minimum(z_all, 0.0) - jnp.log(1.0 + jnp.exp(-jnp.abs(z_all)))) * (1.0 / GLA_GATE_NORM)
        per_chunk = []
        for c in range(n_chunks):
            rows = slice(c * C, (c + 1) * C)
            lg = lg_all[rows, :]
            lg_hi = lg.astype(BF16)
            lg_lo = (lg - lg_hi.astype(F32)).astype(BF16)
            b = _dot(tri, lg_hi) + _dot(tri, lg_lo)
            tot = b[last_row:last_row + 1, :]
            q = qk_ref[rows, 0:GLA_QK].astype(F32)
            k = qk_ref[rows, GLA_QK:2 * GLA_QK].astype(F32)
            v = v_ref[rows, :]
            q_e = q * (jnp.exp(b) * (GLA_DK ** -0.5))
            k_e = (k * jnp.exp(-b)).astype(BF16)
            k_dec = (k * jnp.exp(tot - b)).astype(BF16)
            qm = jnp.concatenate([(q_e * head_masks[h]).astype(BF16) for h in range(GLA_HEADS)], axis=0)
            a = _dot_nt(qm, k_e)
            a = jnp.where(causal, a, 0.0).astype(BF16)
            o_ref[rows, :] = jnp.concatenate(
                [_dot(a[h * C:(h + 1) * C, :], v[:, h * GLA_DV:(h + 1) * GLA_DV]) for h in range(GLA_HEADS)], axis=1)
            ds_scr[c] = _dot_tn(v, k_dec)
            per_chunk.append((qm, jnp.exp(tot)))
        return per_chunk

    def state_step(c, qm, dec, o_ref, ds_scr, st):
        rows = slice(c * C, (c + 1) * C)
        st_b16 = st[...].astype(BF16)
        o_inter = jnp.concatenate(
            [_dot_nt(qm[h * C:(h + 1) * C, :], st_b16[h * GLA_DV:(h + 1) * GLA_DV, :]) for h in range(GLA_HEADS)],
            axis=1)
        o_ref[rows, :] += o_inter
        st[...] = st[...] * dec + ds_scr[c]

    loc_f = local_part(qkf_ref, vf_ref, mf_ref, wgf_ref, bgf_ref, of_ref, dsf_scr, True)
    loc_b = local_part(qkb_ref, vb_ref, mb_ref, wgb_ref, bgb_ref, ob_ref, dsb_scr, False)
    for c in range(n_chunks):
        cb = n_chunks - 1 - c
        state_step(c, *loc_f[c], of_ref, dsf_scr, st_f)
        state_step(cb, *loc_b[cb], ob_ref, dsb_scr, st_b)

    @pl.when(i == nblk - 1)
    def _():
        sfin_ref[0] = st_f[...]
        sfin_ref[1] = st_b[...]


def _gla(main, misc, wgf, bgf, wgb, bgb, s0, tm):
    bsz, t, _ = main.shape
    nblk = t // tm
    fwd = lambda b, i: (b, i, 0)
    bwd = lambda b, i: (b, nblk - 1 - i, 0)
    const2 = lambda b, i: (0, 0)
    kern = functools.partial(_gla_kernel, n_chunks=tm // GLA_CHUNK)
    return pl.pallas_call(
        kern,
        grid=(bsz, nblk),
        in_specs=[pl.BlockSpec((None, tm, 2 * GLA_QK), fwd),
                  pl.BlockSpec((None, tm, GLA_WIDTH), lambda b, i: (b, i, 1)),
                  pl.BlockSpec((None, tm, LANES), fwd),
                  pl.BlockSpec((None, tm, 2 * GLA_QK), bwd),
                  pl.BlockSpec((None, tm, GLA_WIDTH), lambda b, i: (b, nblk - 1 - i, 1)),
                  pl.BlockSpec((None, tm, LANES), bwd),
                  pl.BlockSpec(wgf.shape, const2), pl.BlockSpec(bgf.shape, const2),
                  pl.BlockSpec(wgb.shape, const2), pl.BlockSpec(bgb.shape, const2),
                  pl.BlockSpec((None, 2, GLA_WIDTH, GLA_QK), lambda b, i: (b, 0, 0, 0))],
        out_specs=[pl.BlockSpec((None, tm, GLA_WIDTH), fwd),
                   pl.BlockSpec((None, tm, GLA_WIDTH), bwd),
                   pl.BlockSpec((None, 2, GLA_WIDTH, GLA_QK), lambda b, i: (b, 0, 0, 0))],
        out_shape=(jax.ShapeDtypeStruct((bsz, t, GLA_WIDTH), F32),
                   jax.ShapeDtypeStruct((bsz, t, GLA_WIDTH), F32),
                   jax.ShapeDtypeStruct((bsz, 2, GLA_WIDTH, GLA_QK), F32)),
        scratch_shapes=[pltpu.VMEM((GLA_WIDTH, GLA_QK), F32), pltpu.VMEM((GLA_WIDTH, GLA_QK), F32),
                        pltpu.VMEM((tm // GLA_CHUNK, GLA_WIDTH, GLA_QK), F32),
                        pltpu.VMEM((tm // GLA_CHUNK, GLA_WIDTH, GLA_QK), F32)],
        compiler_params=_cparams(("parallel", "arbitrary")),
        name="gla",
    )(main, main, misc, main, main, misc, wgf, bgf, wgb, bgb, s0)


def _rmsnorm_rows(x, g):
    xf = x.astype(F32)
    ms = jnp.mean(xf * xf, axis=-1, keepdims=True)
    return (xf * lax.rsqrt(ms + EPS)) * g


def _mla_proj_kernel(*refs, rotate, with_q):
    if with_q:
        (cq_ref, ckv_ref, m0_ref, m1_ref, cos_ref, sin_ref, qg_ref, kvg_ref, wq_ref, wkv_ref,
         q_out, k_out, v_out) = refs
    else:
        ckv_ref, m0_ref, kvg_ref, wkv_ref, k_out, v_out = refs
    hw = MLA_HEADS * LANES
    lane = lax.broadcasted_iota(jnp.int32, m0_ref.shape, 1)
    rope_lanes = (lane >= ROPE_LANE0) & (lane < ROPE_LANE0 + MLA_ROPE)
    if rotate:
        cos = cos_ref[...]
        sin = sin_ref[...]
        kr = m0_ref[...] * cos + m1_ref[...] * sin
    else:
        kr = jnp.where(rope_lanes, m0_ref[...], 0.0)
    kv = _dot(_rmsnorm_rows(ckv_ref[...], kvg_ref[...]).astype(BF16), wkv_ref[...])
    for h in range(MLA_HEADS):
        k_out[h] = (kv[:, h * LANES:(h + 1) * LANES] + kr).astype(BF16)
    for p in range(MLA_PAIRS):
        v_out[p] = kv[:, hw + p * LANES:hw + (p + 1) * LANES].astype(BF16)
    if with_q:
        qs = (MLA_NOPE + MLA_ROPE) ** -0.5 * math.log2(math.e)
        cq_tab = jnp.where(lane < MLA_NOPE, qs, cos * qs)
        sq_tab = sin * qs
        qq = _dot(_rmsnorm_rows(cq_ref[...], qg_ref[...]).astype(BF16), wq_ref[...])
        for h in range(MLA_HEADS):
            qa = qq[:, h * LANES:(h + 1) * LANES]
            qb = qq[:, hw + h * LANES:hw + (h + 1) * LANES]
            q_out[h] = (qa * cq_tab + qb * sq_tab).astype(BF16)


def _mla_proj_lat(main, misc, cos_t, sin_t, qg, kvg, wq, wkv, tm):
    bsz, t, _ = main.shape
    c2 = lambda b, i: (0, 0)
    kern = functools.partial(_mla_proj_kernel, rotate=True, with_q=True)
    return pl.pallas_call(
        kern,
        grid=(bsz, t // tm),
        in_specs=[pl.BlockSpec((None, tm, MLA_Q_RANK), lambda b, i: (b, i, COL_CQ // MLA_Q_RANK)),
                  pl.BlockSpec((None, tm, MLA_KV_RANK), lambda b, i: (b, i, COL_CKV // MLA_KV_RANK)),
                  pl.BlockSpec((None, tm, LANES), lambda b, i: (b, i, 0)),
                  pl.BlockSpec((None, tm, LANES), lambda b, i: (b, i, 1)),
                  pl.BlockSpec((tm, LANES), lambda b, i: (i, 0)),
                  pl.BlockSpec((tm, LANES), lambda b, i: (i, 0)),
                  pl.BlockSpec(qg.shape, c2), pl.BlockSpec(kvg.shape, c2),
                  pl.BlockSpec(wq.shape, c2), pl.BlockSpec(wkv.shape, c2)],
        out_specs=[pl.BlockSpec((None, MLA_HEADS, tm, LANES), lambda b, i: (b, 0, i, 0)),
                   pl.BlockSpec((None, MLA_HEADS, tm, LANES), lambda b, i: (b, 0, i, 0)),
                   pl.BlockSpec((None, MLA_PAIRS, tm, V_TILE), lambda b, i: (b, 0, i, 0))],
        out_shape=(jax.ShapeDtypeStruct((bsz, MLA_HEADS, t, LANES), BF16),
                   jax.ShapeDtypeStruct((bsz, MLA_HEADS, t, LANES), BF16),
                   jax.ShapeDtypeStruct((bsz, MLA_PAIRS, t, V_TILE), BF16)),
        compiler_params=_cparams(("parallel", "parallel")),
        name="mla_proj_lat",
    )(main, main, misc, misc, cos_t, sin_t, qg, kvg, wq, wkv)


def _mla_proj_ctx(main, misc, kvg, wkv):
    bsz, t, _ = main.shape
    c2 = lambda b: (0, 0)
    kern = functools.partial(_mla_proj_kernel, rotate=False, with_q=False)
    return pl.pallas_call(
        kern,
        grid=(bsz,),
        in_specs=[pl.BlockSpec((None, t, MLA_KV_RANK), lambda b: (b, 0, COL_CKV // MLA_KV_RANK)),
                  pl.BlockSpec((None, t, LANES), lambda b: (b, 0, 0)),
                  pl.BlockSpec(kvg.shape, c2), pl.BlockSpec(wkv.shape, c2)],
        out_specs=[pl.BlockSpec((None, MLA_HEADS, t, LANES), lambda b: (b, 0, 0, 0)),
                   pl.BlockSpec((None, MLA_PAIRS, t, V_TILE), lambda b: (b, 0, 0, 0))],
        out_shape=(jax.ShapeDtypeStruct((bsz, MLA_HEADS, t, LANES), BF16),
                   jax.ShapeDtypeStruct((bsz, MLA_PAIRS, t, V_TILE), BF16)),
        compiler_params=_cparams(("parallel",)),
        name="mla_proj_ctx",
    )(main, misc, kvg, wkv)


def _attn_kernel(q_ref, kl_ref, kc_ref, vl_ref, vc_ref, o_ref, s_scr, *, tq, tk):
    tb = q_ref.shape[1]
    s_len = kl_ref.shape[1]
    c_len = kc_ref.shape[1]
    items = [(hh, sub) for sub in range(tb // tq) for hh in range(2)]
    chunks = [(kl_ref, vl_ref, c0, tk, c0) for c0 in range(0, s_len, tk)] + [(kc_ref, vc_ref, 0, c_len, s_len)]

    def pass1(idx):
        hh, sub = items[idx]
        q = q_ref[hh, sub * tq:(sub + 1) * tq, :]
        mrun = jnp.full((tq, LANES), -jnp.inf, F32)
        for k_ref, _, r0, n, col in chunks:
            s = _dot_nt(q, k_ref[hh, r0:r0 + n, :])
            s_scr[idx % ATTN_BUFS, :, col:col + n] = s
            for t0 in range(0, n, LANES):
                mrun = jnp.maximum(mrun, s[:, t0:t0 + LANES])
        return jnp.max(mrun, axis=-1, keepdims=True)

    def pass2(idx, m):
        lrun = jnp.zeros((tq, LANES), F32)
        acc = jnp.zeros((tq, LANES), F32)
        for _, v_ref, r0, n, col in chunks:
            p = jnp.exp2(s_scr[idx % ATTN_BUFS, :, col:col + n] - m)
            for t0 in range(0, n, LANES):
                lrun = lrun + p[:, t0:t0 + LANES]
            acc = acc + _dot(p.astype(BF16), v_ref[r0:r0 + n, :])
        return acc * (1.0 / jnp.sum(lrun, axis=-1, keepdims=True))

    outs = {}
    m_prev = pass1(0)
    for idx in range(1, len(items)):
        outs[items[idx - 1]] = pass2(idx - 1, m_prev)
        m_prev = pass1(idx)
    outs[items[-1]] = pass2(len(items) - 1, m_prev)
    lane = lax.broadcasted_iota(jnp.int32, (tq, LANES), 1)
    for sub in range(tb // tq):
        o_ref[sub * tq:(sub + 1) * tq, :] = jnp.where(lane < MLA_DV, outs[(0, sub)], outs[(1, sub)]).astype(BF16)


def _attn(q, k_lat, k_ctx, v_lat, v_ctx, tb, tq, tk):
    bsz, _, s_len, _ = q.shape
    c_len = k_ctx.shape[2]
    kern = functools.partial(_attn_kernel, tq=tq, tk=tk)
    return pl.pallas_call(
        kern,
        grid=(bsz, MLA_PAIRS, s_len // tb),
        in_specs=[pl.BlockSpec((None, 2, tb, LANES), lambda b, p, i: (b, p, i, 0)),
                  pl.BlockSpec((None, 2, s_len, LANES), lambda b, p, i: (b, p, 0, 0)),
                  pl.BlockSpec((None, 2, c_len, LANES), lambda b, p, i: (b, p, 0, 0)),
                  pl.BlockSpec((None, None, s_len, V_TILE), lambda b, p, i: (b, p, 0, 0)),
                  pl.BlockSpec((None, None, c_len, V_TILE), lambda b, p, i: (b, p, 0, 0))],
        out_specs=pl.BlockSpec((None, None, tb, LANES), lambda b, p, i: (b, p, i, 0)),
        out_shape=jax.ShapeDtypeStruct((bsz, MLA_PAIRS, s_len, LANES), BF16),
        scratch_shapes=[pltpu.VMEM((ATTN_BUFS, tq, s_len + c_len), F32)],
        compiler_params=_cparams(("parallel", "parallel", "arbitrary")),
        name="attn",
    )(q, k_lat, k_ctx, v_lat, v_ctx)


def _layernorm_rows(z, g, b):
    mu = jnp.mean(z, axis=-1, keepdims=True)
    zc = z - mu
    var = jnp.mean(zc * zc, axis=-1, keepdims=True)
    return (zc * lax.rsqrt(var + EPS)) * g + b


def _outproj_kernel(of_ref, ob_ref, r_ref, ml_ref, x_ref, mod_ref, gg_ref, wo_ref, l1g_ref, l1b_ref,
                    wr_ref, br_ref, x1_ref, h_ref, comb_ref):
    tm = x_ref.shape[0]
    o = of_ref[...] + ob_ref[...]
    r = r_ref[...].astype(F32)
    gg = gg_ref[...]
    mix = []
    for h in range(GLA_HEADS):
        sl = slice(h * GLA_DV, (h + 1) * GLA_DV)
        oh = o[:, sl]
        ms = jnp.mean(oh * oh, axis=-1, keepdims=True)
        rh = r[:, sl]
        mix.append(((oh * lax.rsqrt(ms + EPS)) * gg * (rh * _sigmoid(rh))).astype(BF16))
    mix += [ml_ref[p] for p in range(MLA_PAIRS)]
    y = _dot(jnp.concatenate(mix, axis=1), wo_ref[...])
    gate1 = mod_ref[2:3, :]
    x1 = _layernorm_rows(DEEPNORM_ALPHA * x_ref[...] + gate1 * y, l1g_ref[...], l1b_ref[...])
    x1_ref[...] = x1
    hmod = x1 * (1.0 + mod_ref[4:5, :]) + mod_ref[3:4, :]
    h_ref[...] = hmod.astype(BF16)

    h_hi = hmod.astype(BF16)
    h_lo = (hmod - h_hi.astype(F32)).astype(BF16)
    wr = wr_ref[...]
    w_hi = wr.astype(BF16)
    w_lo = (wr - w_hi.astype(F32)).astype(BF16)
    pp = _dot(jnp.concatenate([h_hi, h_lo], axis=0), jnp.concatenate([w_hi, w_lo], axis=1))
    logits = ((pp[0:tm, 0:LANES] + pp[0:tm, LANES:2 * LANES])
              + (pp[tm:2 * tm, 0:LANES] + pp[tm:2 * tm, LANES:2 * LANES]) + br_ref[...])

    lane = lax.broadcasted_iota(jnp.int32, (tm, LANES), 1).astype(F32)
    neg = -jnp.inf
    far = float(LANES)
    gl = jnp.where(lane < N_GROUPS, logits, neg)
    gmax = jnp.max(gl, axis=-1, keepdims=True)
    gsum = jnp.sum(jnp.exp(gl - gmax), axis=-1, keepdims=True)
    p_g = 1.0 / gsum
    g_top = jnp.min(jnp.where(gl == gmax, lane, far), axis=-1, keepdims=True)
    e0 = ROUTER_LANE0 + g_top * EXPERTS_PER_GROUP
    el = jnp.where((lane >= e0) & (lane < e0 + EXPERTS_PER_GROUP), logits, neg)
    e1max = jnp.max(el, axis=-1, keepdims=True)
    i1 = jnp.min(jnp.where(el == e1max, lane, far), axis=-1, keepdims=True)
    el2 = jnp.where(lane == i1, neg, el)
    e2max = jnp.max(el2, axis=-1, keepdims=True)
    i2 = jnp.min(jnp.where(el2 == e2max, lane, far), axis=-1, keepdims=True)
    t = jnp.exp(e2max - e1max)
    w1 = p_g / (1.0 + t)
    w2 = w1 * t
    comb_ref[...] = jnp.where(lane == i1, w1, jnp.where(lane == i2, w2, jnp.where(lane == g_top, 1.0, 0.0)))


def _outproj(o_f, o_b, main, mlat, x, mod, gg, wo, l1g, l1b, wr, br, tm):
    bsz, t, d = x.shape
    c2 = lambda b, i: (0, 0)
    row = lambda b, i: (b, i, 0)
    return pl.pallas_call(
        _outproj_kernel,
        grid=(bsz, t // tm),
        in_specs=[pl.BlockSpec((None, tm, GLA_WIDTH), row),
                  pl.BlockSpec((None, tm, GLA_WIDTH), row),
                  pl.BlockSpec((None, tm, GLA_WIDTH), lambda b, i: (b, i, COL_R // GLA_WIDTH)),
                  pl.BlockSpec((None, MLA_PAIRS, tm, LANES), lambda b, i: (b, 0, i, 0)),
                  pl.BlockSpec((None, tm, d), row),
                  pl.BlockSpec((None, 6, d), lambda b, i: (b, 0, 0)),
                  pl.BlockSpec(gg.shape, c2), pl.BlockSpec(wo.shape, c2),
                  pl.BlockSpec(l1g.shape, c2), pl.BlockSpec(l1b.shape, c2),
                  pl.BlockSpec(wr.shape, c2), pl.BlockSpec(br.shape, c2)],
        out_specs=[pl.BlockSpec((None, tm, d), row),
                   pl.BlockSpec((None, tm, d), row),
                   pl.BlockSpec((None, tm, LANES), row)],
        out_shape=(jax.ShapeDtypeStruct((bsz, t, d), F32),
                   jax.ShapeDtypeStruct((bsz, t, d), BF16),
                   jax.ShapeDtypeStruct((bsz, t, LANES), F32)),
        compiler_params=_cparams(("parallel", "parallel")),
        name="outproj",
    )(o_f, o_b, main, mlat, x, mod, gg, wo, l1g, l1b, wr, br)


def _moe_kernel(h_ref, comb_ref, x1_ref, mod_ref, wgu_ref, wd_ref, l2g_ref, l2b_ref, o_ref,
                hs_scr, cs_scr, acc_scr):
    tm = h_ref.shape[0]
    rb = MOE_ROW_BLOCK
    gw = EXPERTS_PER_GROUP * D_EXPERT
    comb = comb_ref[...]
    lane = lax.broadcasted_iota(jnp.int32, (tm, LANES), 1)
    onehot = jnp.where(lane < N_GROUPS, comb, 0.0)

    ri = lax.broadcasted_iota(jnp.int32, (tm, tm), 0)
    ci = lax.broadcasted_iota(jnp.int32, (tm, tm), 1)
    lower = jnp.where(ci < ri, 1.0, 0.0).astype(BF16)
    before = _dot(lower, onehot.astype(BF16))
    rank = jnp.sum(before * onehot, axis=-1, keepdims=True)
    totals = jnp.broadcast_to(jnp.sum(onehot, axis=0, keepdims=True), (8, LANES))
    offs = pltpu.roll(totals, 1, 1) + pltpu.roll(totals, 2, 1) + pltpu.roll(totals, 3, 1)
    pos = jnp.sum(onehot * offs[0:1, :], axis=-1, keepdims=True) + rank
    pt = jnp.where(ci.astype(F32) == pos, 1.0, 0.0).astype(BF16)

    hs_scr[0:tm, :] = _dot_tn(pt, h_ref[...]).astype(BF16)
    hs_scr[tm:tm + rb, :] = jnp.zeros((rb, hs_scr.shape[1]), BF16)
    c_hi = comb.astype(BF16)
    c_lo = (comb - c_hi.astype(F32)).astype(BF16)
    cs_scr[0:tm, :] = _dot_tn(pt, c_hi) + _dot_tn(pt, c_lo)
    cs_scr[tm:tm + rb, :] = jnp.zeros((rb, LANES), F32)
    acc_scr[...] = jnp.zeros_like(acc_scr)

    tot_i = totals.astype(jnp.int32)
    off_i = offs.astype(jnp.int32)
    lane_r = lax.broadcasted_iota(jnp.int32, (rb, LANES), 1)
    for g in range(N_GROUPS):
        n_g = tot_i[0, g]
        start = off_i[0, g]
        first = (start // BF16_SUBLANES) * BF16_SUBLANES
        n_blocks = jnp.where(n_g > 0, (start + n_g - first + rb - 1) // rb, 0)

        def block(k, carry, g=g, first=first):
            rows = pl.ds(pl.multiple_of(first + k * rb, BF16_SUBLANES), rb)
            hb = hs_scr[rows, :]
            cb = cs_scr[rows, :]
            parts = []
            for j in range(EXPERTS_PER_GROUP):
                e = g * EXPERTS_PER_GROUP + j
                w = jnp.sum(jnp.where(lane_r == ROUTER_LANE0 + e, cb, 0.0), axis=-1, keepdims=True)
                gu = _dot(hb, wgu_ref[e])
                gj = gu[:, 0:D_EXPERT]
                uj = gu[:, D_EXPERT:2 * D_EXPERT]
                parts.append(((gj * _sigmoid(gj)) * uj * w).astype(BF16))
            acc_scr[rows, :] += _dot(jnp.concatenate(parts, axis=1), wd_ref[g])
            return carry

        lax.fori_loop(0, n_blocks, block, 0)

    y = _dot(pt, acc_scr[0:tm, :].astype(BF16))
    gate2 = mod_ref[5:6, :]
    z = DEEPNORM_ALPHA * x1_ref[...] + gate2 * y
    o_ref[...] = _layernorm_rows(z, l2g_ref[...], l2b_ref[...])


def _moe(h, comb, x1, mod, wgu, wd, l2g, l2b, tm):
    bsz, t, d = x1.shape
    row = lambda b, i: (b, i, 0)
    c2 = lambda b, i: (0, 0)
    c3 = lambda b, i: (0, 0, 0)
    resident = pl.Buffered(1)
    return pl.pallas_call(
        _moe_kernel,
        grid=(bsz, t // tm),
        in_specs=[pl.BlockSpec((None, tm, d), row),
                  pl.BlockSpec((None, tm, LANES), row),
                  pl.BlockSpec((None, tm, d), row),
                  pl.BlockSpec((None, 6, d), lambda b, i: (b, 0, 0)),
                  pl.BlockSpec(wgu.shape, c3, pipeline_mode=resident),
                  pl.BlockSpec(wd.shape, c3, pipeline_mode=resident),
                  pl.BlockSpec(l2g.shape, c2), pl.BlockSpec(l2b.shape, c2)],
        out_specs=pl.BlockSpec((None, tm, d), row),
        out_shape=jax.ShapeDtypeStruct((bsz, t, d), F32),
        scratch_shapes=[pltpu.VMEM((tm + MOE_ROW_BLOCK, d), BF16),
                        pltpu.VMEM((tm + MOE_ROW_BLOCK, LANES), F32),
                        pltpu.VMEM((tm + MOE_ROW_BLOCK, d), F32)],
        compiler_params=pltpu.CompilerParams(dimension_semantics=("parallel", "parallel"),
                                             vmem_limit_bytes=MOE_VMEM_LIMIT),
        name="moe",
    )(h, comb, x1, mod, wgu, wd, l2g, l2b)


_ROPE_SWAP = np.concatenate([np.arange(8, 16), np.arange(0, 8), np.arange(24, 32), np.arange(16, 24)])


def _pack_w_in(w_in):
    d = w_in.shape[0]
    o_q, o_k, o_v, o_gf, o_gb, o_r, o_cq, o_ckv, o_kr = 0, 256, 512, 1024, 1040, 1056, 1568, 1824, 1952
    z = lambda n: jnp.zeros((d, n), w_in.dtype)
    kr = w_in[:, o_kr:o_kr + MLA_ROPE]
    cols = [w_in[:, o_q:o_k], w_in[:, o_k:o_v], w_in[:, o_v:o_gf], w_in[:, o_r:o_cq], w_in[:, o_cq:o_ckv],
            w_in[:, o_ckv:o_kr],
            w_in[:, o_gf:o_gb], w_in[:, o_gb:o_r], z(32), kr, z(32),
            z(64), kr[:, _ROPE_SWAP], z(32)]
    return jnp.concatenate(cols, axis=1).astype(BF16)


def _pack_gate_w(w_gk, lane0):
    out = jnp.zeros((LANES, w_gk.shape[1]), w_gk.dtype)
    return out.at[lane0:lane0 + GLA_RANK].set(w_gk).astype(BF16)


def _pack_w_uq(w_uq):
    r = w_uq.shape[0]
    w = w_uq.reshape(r, MLA_HEADS, MLA_NOPE + MLA_ROPE)
    nope, rope = w[..., :MLA_NOPE], w[..., MLA_NOPE:]
    z32 = jnp.zeros((r, MLA_HEADS, 32), w_uq.dtype)
    z64 = jnp.zeros((r, MLA_HEADS, 64), w_uq.dtype)
    a = jnp.concatenate([nope, rope, z32], axis=-1).reshape(r, MLA_HEADS * LANES)
    b = jnp.concatenate([z64, rope[..., _ROPE_SWAP], z32], axis=-1).reshape(r, MLA_HEADS * LANES)
    return jnp.concatenate([a, b], axis=1).astype(BF16)


def _pack_w_ukv(w_ukv):
    r = w_ukv.shape[0]
    w = w_ukv.reshape(r, MLA_HEADS, MLA_NOPE + MLA_DV)
    kn, v = w[..., :MLA_NOPE], w[..., MLA_NOPE:]
    k_t = jnp.concatenate([kn, jnp.zeros_like(kn)], axis=-1).reshape(r, MLA_HEADS * LANES)
    v_t = v.reshape(r, MLA_HEADS * MLA_DV)
    return jnp.concatenate([k_t, v_t], axis=1).astype(BF16)


def _pack_experts(w_gate, w_up, w_down):
    n_e, d, de = w_gate.shape
    wgu = jnp.concatenate([w_gate, w_up], axis=-1).astype(BF16)
    wd = w_down.reshape(N_GROUPS, EXPERTS_PER_GROUP * de, d).astype(BF16)
    return wgu, wd


def _pack_router(w_rg, b_rg, w_re, b_re):
    d = w_rg.shape[0]
    pad = LANES - N_GROUPS - N_EXPERTS
    w = jnp.concatenate([w_rg, w_re, jnp.zeros((d, pad), w_rg.dtype)], axis=1)
    b = jnp.concatenate([b_rg, b_re, jnp.zeros((pad,), b_rg.dtype)]).reshape(1, LANES)
    return w, b


def kernel(x, c, ctx, c_ctx, w_ada, b_ada, w_in, w_gk_f, b_gk_f, w_gk_b, b_gk_b, gla_norm_g, mla_q_norm_g, w_uq, mla_kv_norm_g, w_ukv, w_o, ln1_g, ln1_b, w_router_group, b_router_group, w_router_expert, b_router_expert, w_expert_gate, w_expert_up, w_expert_down, ln2_g, ln2_b):
    bsz, seq, d = x.shape
    l = 0

    c_all = jnp.concatenate([c, c_ctx[None, :], jnp.zeros((16 - bsz - 1, d), c.dtype)], axis=0)
    mod = _ada(c_all, w_ada[l], b_ada[l])
    mod_lat = mod[:bsz].reshape(bsz, 6, d)
    mod_ctx = mod[bsz:bsz + 1].reshape(1, 6, d)

    w_in_p = _pack_w_in(w_in[l])
    main_lat, misc_lat = _inproj(x, mod_lat, w_in_p, True, 512)
    main_ctx, misc_ctx = _inproj(ctx, mod_ctx, w_in_p, False, ctx.shape[1])

    wgf = _pack_gate_w(w_gk_f[l], 0)
    wgb = _pack_gate_w(w_gk_b[l], GLA_RANK)
    bgf = b_gk_f[l].reshape(1, GLA_QK)
    bgb = b_gk_b[l].reshape(1, GLA_QK)
    s_zero = jnp.zeros((bsz, 2, GLA_WIDTH, GLA_QK), F32)
    _, _, s_ctx = _gla(main_ctx, misc_ctx, wgf, bgf, wgb, bgb, s_zero, ctx.shape[1])
    o_f, o_b, _ = _gla(main_lat, misc_lat, wgf, bgf, wgb, bgb, s_ctx, 512)

    cos_t, sin_t = _rope_tables(seq)
    qg = mla_q_norm_g[l].reshape(1, MLA_Q_RANK)
    kvg = mla_kv_norm_g[l].reshape(1, MLA_KV_RANK)
    wq_p = _pack_w_uq(w_uq[l])
    wkv_p = _pack_w_ukv(w_ukv[l])
    q_lat, k_lat, v_lat = _mla_proj_lat(main_lat, misc_lat, cos_t, sin_t, qg, kvg, wq_p, wkv_p, 512)
    k_ctx, v_ctx = _mla_proj_ctx(main_ctx, misc_ctx, kvg, wkv_p)
    m_lat = _attn(q_lat, k_lat, k_ctx, v_lat, v_ctx, ATTN_TB, ATTN_TQ, 512)

    gg = gla_norm_g[l].reshape(1, GLA_DV)
    w_r, b_r = _pack_router(w_router_group[l], b_router_group[l], w_router_expert[l], b_router_expert[l])
    x1, h, comb = _outproj(o_f, o_b, main_lat, m_lat, x, mod_lat, gg, w_o[l].astype(BF16),
                           ln1_g[l].reshape(1, d), ln1_b[l].reshape(1, d), w_r, b_r, 256)

    wgu, wd = _pack_experts(w_expert_gate[l], w_expert_up[l], w_expert_down[l])
    return _moe(h, comb, x1, mod_lat, wgu, wd, ln2_g[l].reshape(1, d), ln2_b[l].reshape(1, d), 512)
```

```python
import functools
import math

import jax
import jax.numpy as jnp
import numpy as np
from jax import lax
from jax.experimental import pallas as pl
from jax.experimental.pallas import tpu as pltpu

F32 = jnp.float32
BF16 = jnp.bfloat16

D_MODEL = 1024
GRID_W = 64
GLA_HEADS = 4
GLA_DK = 64
GLA_DV = 128
GLA_RANK = 16
GLA_GATE_NORM = 16.0
GLA_CHUNK = 64
GLA_QK = GLA_HEADS * GLA_DK
GLA_WIDTH = GLA_HEADS * GLA_DV
MLA_HEADS = 8
MLA_NOPE = 64
MLA_ROPE = 32
MLA_DV = 64
MLA_Q_RANK = 256
MLA_KV_RANK = 128
MLA_PAIRS = MLA_HEADS // 2
ROPE_BASE = 10000.0
N_GROUPS = 4
EXPERTS_PER_GROUP = 4
N_EXPERTS = 16
D_EXPERT = 256
DEPTH = 1
DEEPNORM_ALPHA = (2.0 * DEPTH) ** 0.25
EPS = 1e-6

LANES = 128
COL_Q, COL_K, COL_V, COL_R, COL_CQ, COL_CKV = 0, 256, 512, 1024, 1536, 1792
MAIN_W = 1920
MISC_W = 256
INPROJ_CHUNK = 768
ROPE_LANE0 = 64
V_TILE = LANES
ROUTER_LANE0 = N_GROUPS
VMEM_LIMIT = 48 * 1024 * 1024
MOE_VMEM_LIMIT = 56 * 1024 * 1024
BF16_SUBLANES = 16
MOE_ROW_BLOCK = 128
ATTN_BUFS = 4
ATTN_TB = 1024
ATTN_TQ = 256
ATTN_TK = 512


def _cparams(sem):
    return pltpu.CompilerParams(dimension_semantics=sem, vmem_limit_bytes=VMEM_LIMIT)


def _dot(a, b):
    return jnp.dot(a, b, preferred_element_type=F32)


def _dot_nt(a, b):
    return lax.dot_general(a, b, (((1,), (1,)), ((), ())), preferred_element_type=F32)


def _dot_tn(a, b):
    return lax.dot_general(a, b, (((0,), (0,)), ((), ())), preferred_element_type=F32)


def _sigmoid(x):
    return 1.0 / (1.0 + jnp.exp(-x))


def _ada_kernel(c_ref, w_ref, b_ref, o_ref):
    a = c_ref[...]
    a = a * _sigmoid(a)
    o_ref[...] = _dot(a.astype(BF16), w_ref[...].astype(BF16)) + b_ref[...]


def _ada(c_all, w, b):
    rows, d = c_all.shape
    n = w.shape[1]
    bn = 1536
    return pl.pallas_call(
        _ada_kernel,
        grid=(n // bn,),
        in_specs=[pl.BlockSpec((rows, d), lambda j: (0, 0)),
                  pl.BlockSpec((d, bn), lambda j: (0, j)),
                  pl.BlockSpec((1, bn), lambda j: (0, j))],
        out_specs=pl.BlockSpec((rows, bn), lambda j: (0, j)),
        out_shape=jax.ShapeDtypeStruct((rows, n), F32),
        compiler_params=_cparams(("parallel",)),
        name="ada",
    )(c_all, w, b.reshape(1, n))


def _rope_tab_kernel(cos_ref, sin_ref):
    shape = cos_ref.shape
    t = lax.broadcasted_iota(jnp.int32, shape, 0) + pl.program_id(0) * shape[0]
    lane = lax.broadcasted_iota(jnp.int32, shape, 1)
    j = lane - ROPE_LANE0
    valid = (j >= 0) & (j < MLA_ROPE)
    f = (j & 7).astype(F32)
    inv_freq = jnp.exp(f * (-math.log(ROPE_BASE) / 8.0))
    pos = jnp.where(j >= 16, t & (GRID_W - 1), jnp.right_shift(t, int(math.log2(GRID_W)))).astype(F32)
    ang = pos * inv_freq
    sign = jnp.where((j & 15) < 8, -1.0, 1.0)
    cos_ref[...] = jnp.where(valid, jnp.cos(ang), 0.0)
    sin_ref[...] = jnp.where(valid, sign * jnp.sin(ang), 0.0)


def _rope_tables(seq):
    tm = 512
    spec = pl.BlockSpec((tm, LANES), lambda i: (i, 0))
    return pl.pallas_call(
        _rope_tab_kernel,
        grid=(seq // tm,),
        out_specs=[spec, spec],
        out_shape=(jax.ShapeDtypeStruct((seq, LANES), F32), jax.ShapeDtypeStruct((seq, LANES), F32)),
        compiler_params=_cparams(("parallel",)),
        name="rope_tab",
    )()


def _inproj_kernel(x_ref, mod_ref, w_ref, main_ref, misc_ref):
    shift = mod_ref[0:1, :]
    scale = mod_ref[1:2, :]
    u = (x_ref[...] * (1.0 + scale) + shift).astype(BF16)
    for c0 in range(0, MAIN_W, INPROJ_CHUNK):
        c1 = min(c0 + INPROJ_CHUNK, MAIN_W + MISC_W)
        y = _dot(u, w_ref[:, c0:c1])
        if c1 <= MAIN_W:
            main_ref[:, c0:c1] = y.astype(BF16)
        else:
            main_ref[:, c0:MAIN_W] = y[:, 0:MAIN_W - c0].astype(BF16)
            misc_ref[...] = y[:, MAIN_W - c0:c1 - c0]


def _inproj(x, mod, w, per_batch, tm):
    bsz, t, d = x.shape
    mod_map = (lambda b, i: (b, 0, 0)) if per_batch else (lambda b, i: (0, 0, 0))
    return pl.pallas_call(
        _inproj_kernel,
        grid=(bsz, t // tm),
        in_specs=[pl.BlockSpec((None, tm, d), lambda b, i: (b, i, 0)),
                  pl.BlockSpec((None, 6, d), mod_map),
                  pl.BlockSpec(w.shape, lambda b, i: (0, 0))],
        out_specs=[pl.BlockSpec((None, tm, MAIN_W), lambda b, i: (b, i, 0)),
                   pl.BlockSpec((None, tm, MISC_W), lambda b, i: (b, i, 0))],
        out_shape=(jax.ShapeDtypeStruct((bsz, t, MAIN_W), BF16),
                   jax.ShapeDtypeStruct((bsz, t, MISC_W), F32)),
        compiler_params=_cparams(("parallel", "parallel")),
        name="inproj",
    )(x, mod, w)


def _gla_kernel(qkf_ref, vf_ref, mf_ref, qkb_ref, vb_ref, mb_ref, wgf_ref, bgf_ref, wgb_ref, bgb_ref,
                s0_ref, of_ref, ob_ref, sfin_ref, st_f, st_b, dsf_scr, dsb_scr, *, n_chunks):
    i = pl.program_id(1)
    nblk = pl.num_programs(1)
    C = GLA_CHUNK

    @pl.when(i == 0)
    def _():
        st_f[...] = s0_ref[0]
        st_b[...] = s0_ref[1]

    r64 = lax.broadcasted_iota(jnp.int32, (C, C), 0)
    c64 = lax.broadcasted_iota(jnp.int32, (C, C), 1)
    ra = lax.broadcasted_iota(jnp.int32, (GLA_HEADS * C, C), 0) & (C - 1)
    ca = lax.broadcasted_iota(jnp.int32, (GLA_HEADS * C, C), 1)
    lane_head = lax.broadcasted_iota(jnp.int32, (C, GLA_QK), 1) // GLA_DK
    head_masks = [jnp.where(lane_head == h, 1.0, 0.0) for h in range(GLA_HEADS)]

    def local_part(qk_ref, v_ref, m_ref, wg_ref, bg_ref, o_ref, ds_scr, forward):
        if forward:
            tri = jnp.where(c64 <= r64, 1.0, 0.0).astype(BF16)
            causal = ca <= ra
            last_row = C - 1
        else:
            tri = jnp.where(c64 >= r64, 1.0, 0.0).astype(BF16)
            causal = ca >= ra
            last_row = 0
        z_all = _dot(m_ref[...].astype(BF16), wg_ref[...]) + bg_ref[...]
        lg_all = (jnp.minimum(z_all, 0.0) - jnp.log(1.0 + jnp.exp(-jnp.abs(z_all)))) * (1.0 / GLA_GATE_NORM)
        chunk_rows = [slice(c * C, (c + 1) * C) for c in range(n_chunks)]
        lg_hi_all = lg_all.astype(BF16)
        lg_lo_all = (lg_all - lg_hi_all.astype(F32)).astype(BF16)
        bs = [_dot(tri, lg_hi_all[r, :]) + _dot(tri, lg_lo_all[r, :]) for r in chunk_rows]
        tots = [b[last_row:last_row + 1, :] for b in bs]
        qs = [qk_ref[r, 0:GLA_QK].astype(F32) for r in chunk_rows]
        ks = [qk_ref[r, GLA_QK:2 * GLA_QK].astype(F32) for r in chunk_rows]
        vs = [v_ref[r, :] for r in chunk_rows]
        q_es = [q * (jnp.exp(b) * (GLA_DK ** -0.5)) for q, b in zip(qs, bs)]
        k_es = [(k * jnp.exp(-b)).astype(BF16) for k, b in zip(ks, bs)]
        k_decs = [(k * jnp.exp(t - b)).astype(BF16) for k, b, t in zip(ks, bs, tots)]
        qms = [jnp.concatenate([(q_e * head_masks[h]).astype(BF16) for h in range(GLA_HEADS)], axis=0) for q_e in q_es]
        a_s = [jnp.where(causal, _dot_nt(qm, k_e), 0.0).astype(BF16) for qm, k_e in zip(qms, k_es)]
        for c in range(n_chunks):
            o_ref[chunk_rows[c], :] = jnp.concatenate(
                [_dot(a_s[c][h * C:(h + 1) * C, :], vs[c][:, h * GLA_DV:(h + 1) * GLA_DV]) for h in range(GLA_HEADS)],
                axis=1)
        for c in range(n_chunks):
            ds_scr[c] = _dot_tn(vs[c], k_decs[c])
        return [(qm, jnp.exp(t)) for qm, t in zip(qms, tots)]

    def state_step(c, qm, dec, o_ref, ds_scr, st):
        rows = slice(c * C, (c + 1) * C)
        st_b16 = st[...].astype(BF16)
        o_inter = jnp.concatenate(
            [_dot_nt(qm[h * C:(h + 1) * C, :], st_b16[h * GLA_DV:(h + 1) * GLA_DV, :]) for h in range(GLA_HEADS)],
            axis=1)
        o_ref[rows, :] += o_inter
        st[...] = st[...] * dec + ds_scr[c]

    loc_f = local_part(qkf_ref, vf_ref, mf_ref, wgf_ref, bgf_ref, of_ref, dsf_scr, True)
    loc_b = local_part(qkb_ref, vb_ref, mb_ref, wgb_ref, bgb_ref, ob_ref, dsb_scr, False)
    for c in range(n_chunks):
        cb = n_chunks - 1 - c
        state_step(c, *loc_f[c], of_ref, dsf_scr, st_f)
        state_step(cb, *loc_b[cb], ob_ref, dsb_scr, st_b)

    @pl.when(i == nblk - 1)
    def _():
        sfin_ref[0] = st_f[...]
        sfin_ref[1] = st_b[...]


def _gla(main, misc, wgf, bgf, wgb, bgb, s0, tm):
    bsz, t, _ = main.shape
    nblk = t // tm
    fwd = lambda b, i: (b, i, 0)
    bwd = lambda b, i: (b, nblk - 1 - i, 0)
    const2 = lambda b, i: (0, 0)
    kern = functools.partial(_gla_kernel, n_chunks=tm // GLA_CHUNK)
    return pl.pallas_call(
        kern,
        grid=(bsz, nblk),
        in_specs=[pl.BlockSpec((None, tm, 2 * GLA_QK), fwd),
                  pl.BlockSpec((None, tm, GLA_WIDTH), lambda b, i: (b, i, 1)),
                  pl.BlockSpec((None, tm, LANES), fwd),
                  pl.BlockSpec((None, tm, 2 * GLA_QK), bwd),
                  pl.BlockSpec((None, tm, GLA_WIDTH), lambda b, i: (b, nblk - 1 - i, 1)),
                  pl.BlockSpec((None, tm, LANES), bwd),
                  pl.BlockSpec(wgf.shape, const2), pl.BlockSpec(bgf.shape, const2),
                  pl.BlockSpec(wgb.shape, const2), pl.BlockSpec(bgb.shape, const2),
                  pl.BlockSpec((None, 2, GLA_WIDTH, GLA_QK), lambda b, i: (b, 0, 0, 0))],
        out_specs=[pl.BlockSpec((None, tm, GLA_WIDTH), fwd),
                   pl.BlockSpec((None, tm, GLA_WIDTH), bwd),
                   pl.BlockSpec((None, 2, GLA_WIDTH, GLA_QK), lambda b, i: (b, 0, 0, 0))],
        out_shape=(jax.ShapeDtypeStruct((bsz, t, GLA_WIDTH), F32),
                   jax.ShapeDtypeStruct((bsz, t, GLA_WIDTH), F32),
                   jax.ShapeDtypeStruct((bsz, 2, GLA_WIDTH, GLA_QK), F32)),
        scratch_shapes=[pltpu.VMEM((GLA_WIDTH, GLA_QK), F32), pltpu.VMEM((GLA_WIDTH, GLA_QK), F32),
                        pltpu.VMEM((tm // GLA_CHUNK, GLA_WIDTH, GLA_QK), F32),
                        pltpu.VMEM((tm // GLA_CHUNK, GLA_WIDTH, GLA_QK), F32)],
        compiler_params=_cparams(("parallel", "arbitrary")),
        name="gla",
    )(main, main, misc, main, main, misc, wgf, bgf, wgb, bgb, s0)


def _rmsnorm_rows(x, g):
    xf = x.astype(F32)
    ms = jnp.mean(xf * xf, axis=-1, keepdims=True)
    return (xf * lax.rsqrt(ms + EPS)) * g


def _mla_proj_kernel(*refs, rotate, with_q):
    if with_q:
        (cq_ref, ckv_ref, m0_ref, m1_ref, cos_ref, sin_ref, qg_ref, kvg_ref, wq_ref, wkv_ref,
         q_out, k_out, v_out) = refs
    else:
        ckv_ref, m0_ref, kvg_ref, wkv_ref, k_out, v_out = refs
    hw = MLA_HEADS * LANES
    lane = lax.broadcasted_iota(jnp.int32, m0_ref.shape, 1)
    rope_lanes = (lane >= ROPE_LANE0) & (lane < ROPE_LANE0 + MLA_ROPE)
    if rotate:
        cos = cos_ref[...]
        sin = sin_ref[...]
        kr = m0_ref[...] * cos + m1_ref[...] * sin
    else:
        kr = jnp.where(rope_lanes, m0_ref[...], 0.0)
    kv = _dot(_rmsnorm_rows(ckv_ref[...], kvg_ref[...]).astype(BF16), wkv_ref[...])
    for h in range(MLA_HEADS):
        k_out[h] = (kv[:, h * LANES:(h + 1) * LANES] + kr).astype(BF16)
    for p in range(MLA_PAIRS):
        v_out[p] = jnp.transpose(kv[:, hw + p * LANES:hw + (p + 1) * LANES]).astype(BF16)
    if with_q:
        qs = (MLA_NOPE + MLA_ROPE) ** -0.5 * math.log2(math.e)
        cq_tab = jnp.where(lane < MLA_NOPE, qs, cos * qs)
        sq_tab = sin * qs
        qq = _dot(_rmsnorm_rows(cq_ref[...], qg_ref[...]).astype(BF16), wq_ref[...])
        for h in range(MLA_HEADS):
            qa = qq[:, h * LANES:(h + 1) * LANES]
            qb = qq[:, hw + h * LANES:hw + (h + 1) * LANES]
            q_out[h] = (qa * cq_tab + qb * sq_tab).astype(BF16)


def _mla_proj_lat(main, misc, cos_t, sin_t, qg, kvg, wq, wkv, tm):
    bsz, t, _ = main.shape
    c2 = lambda b, i: (0, 0)
    kern = functools.partial(_mla_proj_kernel, rotate=True, with_q=True)
    return pl.pallas_call(
        kern,
        grid=(bsz, t // tm),
        in_specs=[pl.BlockSpec((None, tm, MLA_Q_RANK), lambda b, i: (b, i, COL_CQ // MLA_Q_RANK)),
                  pl.BlockSpec((None, tm, MLA_KV_RANK), lambda b, i: (b, i, COL_CKV // MLA_KV_RANK)),
                  pl.BlockSpec((None, tm, LANES), lambda b, i: (b, i, 0)),
                  pl.BlockSpec((None, tm, LANES), lambda b, i: (b, i, 1)),
                  pl.BlockSpec((tm, LANES), lambda b, i: (i, 0)),
                  pl.BlockSpec((tm, LANES), lambda b, i: (i, 0)),
                  pl.BlockSpec(qg.shape, c2), pl.BlockSpec(kvg.shape, c2),
                  pl.BlockSpec(wq.shape, c2), pl.BlockSpec(wkv.shape, c2)],
        out_specs=[pl.BlockSpec((None, MLA_HEADS, tm, LANES), lambda b, i: (b, 0, i, 0)),
                   pl.BlockSpec((None, MLA_HEADS, tm, LANES), lambda b, i: (b, 0, i, 0)),
                   pl.BlockSpec((None, MLA_PAIRS, V_TILE, tm), lambda b, i: (b, 0, 0, i))],
        out_shape=(jax.ShapeDtypeStruct((bsz, MLA_HEADS, t, LANES), BF16),
                   jax.ShapeDtypeStruct((bsz, MLA_HEADS, t, LANES), BF16),
                   jax.ShapeDtypeStruct((bsz, MLA_PAIRS, V_TILE, t), BF16)),
        compiler_params=_cparams(("parallel", "parallel")),
        name="mla_proj_lat",
    )(main, main, misc, misc, cos_t, sin_t, qg, kvg, wq, wkv)


def _mla_proj_ctx(main, misc, kvg, wkv):
    bsz, t, _ = main.shape
    c2 = lambda b: (0, 0)
    kern = functools.partial(_mla_proj_kernel, rotate=False, with_q=False)
    return pl.pallas_call(
        kern,
        grid=(bsz,),
        in_specs=[pl.BlockSpec((None, t, MLA_KV_RANK), lambda b: (b, 0, COL_CKV // MLA_KV_RANK)),
                  pl.BlockSpec((None, t, LANES), lambda b: (b, 0, 0)),
                  pl.BlockSpec(kvg.shape, c2), pl.BlockSpec(wkv.shape, c2)],
        out_specs=[pl.BlockSpec((None, MLA_HEADS, t, LANES), lambda b: (b, 0, 0, 0)),
                   pl.BlockSpec((None, MLA_PAIRS, V_TILE, t), lambda b: (b, 0, 0, 0))],
        out_shape=(jax.ShapeDtypeStruct((bsz, MLA_HEADS, t, LANES), BF16),
                   jax.ShapeDtypeStruct((bsz, MLA_PAIRS, V_TILE, t), BF16)),
        compiler_params=_cparams(("parallel",)),
        name="mla_proj_ctx",
    )(main, misc, kvg, wkv)


def _attn_t_kernel(q_ref, kl_ref, kc_ref, vtl_ref, vtc_ref, o_ref, *s_bufs, tq, tk):
    tb = q_ref.shape[1]
    s_len = kl_ref.shape[1]
    c_len = kc_ref.shape[1]
    items = [(hh, sub) for sub in range(tb // tq) for hh in range(2)]
    chunks = [(kl_ref, vtl_ref, c0, tk, c0) for c0 in range(0, s_len, tk)] + [(kc_ref, vtc_ref, 0, c_len, s_len)]

    def pass1_chunk(idx, c, m):
        hh, sub = items[idx]
        k_ref, _, r0, n, row = chunks[c]
        q = q_ref[hh, sub * tq:(sub + 1) * tq, :]
        s_t = _dot_nt(k_ref[hh, r0:r0 + n, :], q)
        s_bufs[idx % ATTN_BUFS][row:row + n, :] = s_t
        return jnp.maximum(m, jnp.max(s_t, axis=0, keepdims=True))

    def pass2_chunk(idx, c, m, l, acc):
        hh, _ = items[idx]
        _, vt_ref, r0, n, row = chunks[c]
        p_t = jnp.exp2(s_bufs[idx % ATTN_BUFS][row:row + n, :] - m)
        l = l + jnp.sum(p_t, axis=0, keepdims=True)
        acc = acc + _dot(vt_ref[hh * MLA_DV:(hh + 1) * MLA_DV, r0:r0 + n], p_t.astype(BF16))
        return l, acc

    outs = {}
    m_prev = None
    for idx in range(len(items) + 1):
        m_new = jnp.full((1, tq), -jnp.inf, F32)
        l = jnp.zeros((1, tq), F32)
        acc = jnp.zeros((MLA_DV, tq), F32)
        for c in range(len(chunks)):
            if idx > 0:
                l, acc = pass2_chunk(idx - 1, c, m_prev, l, acc)
            if idx < len(items):
                m_new = pass1_chunk(idx, c, m_new)
        if idx > 0:
            outs[items[idx - 1]] = acc * (1.0 / l)
        m_prev = m_new
    for sub in range(tb // tq):
        o_t = jnp.concatenate([outs[(0, sub)], outs[(1, sub)]], axis=0)
        o_ref[sub * tq:(sub + 1) * tq, :] = jnp.transpose(o_t).astype(BF16)


def _attn(q, k_lat, k_ctx, vt_lat, vt_ctx, tb, tq, tk):
    bsz, _, s_len, _ = q.shape
    c_len = k_ctx.shape[2]
    kern = functools.partial(_attn_t_kernel, tq=tq, tk=tk)
    return pl.pallas_call(
        kern,
        grid=(bsz, MLA_PAIRS, s_len // tb),
        in_specs=[pl.BlockSpec((None, 2, tb, LANES), lambda b, p, i: (b, p, i, 0)),
                  pl.BlockSpec((None, 2, s_len, LANES), lambda b, p, i: (b, p, 0, 0)),
                  pl.BlockSpec((None, 2, c_len, LANES), lambda b, p, i: (b, p, 0, 0)),
                  pl.BlockSpec((None, None, V_TILE, s_len), lambda b, p, i: (b, p, 0, 0)),
                  pl.BlockSpec((None, None, V_TILE, c_len), lambda b, p, i: (b, p, 0, 0))],
        out_specs=pl.BlockSpec((None, None, tb, LANES), lambda b, p, i: (b, p, i, 0)),
        out_shape=jax.ShapeDtypeStruct((bsz, MLA_PAIRS, s_len, LANES), BF16),
        scratch_shapes=[pltpu.VMEM((s_len + c_len, tq), F32) for _ in range(ATTN_BUFS)],
        compiler_params=_cparams(("parallel", "parallel", "arbitrary")),
        name="attn",
    )(q, k_lat, k_ctx, vt_lat, vt_ctx)


def _layernorm_rows(z, g, b):
    mu = jnp.mean(z, axis=-1, keepdims=True)
    zc = z - mu
    var = jnp.mean(zc * zc, axis=-1, keepdims=True)
    return (zc * lax.rsqrt(var + EPS)) * g + b


def _outproj_kernel(of_ref, ob_ref, r_ref, ml_ref, x_ref, mod_ref, gg_ref, wo_ref, l1g_ref, l1b_ref,
                    wr_ref, br_ref, x1_ref, h_ref, comb_ref):
    tm = x_ref.shape[0]
    o = of_ref[...] + ob_ref[...]
    r = r_ref[...].astype(F32)
    gg = gg_ref[...]
    mix = []
    for h in range(GLA_HEADS):
        sl = slice(h * GLA_DV, (h + 1) * GLA_DV)
        oh = o[:, sl]
        ms = jnp.mean(oh * oh, axis=-1, keepdims=True)
        rh = r[:, sl]
        mix.append(((oh * lax.rsqrt(ms + EPS)) * gg * (rh * _sigmoid(rh))).astype(BF16))
    mix += [ml_ref[p] for p in range(MLA_PAIRS)]
    y = _dot(jnp.concatenate(mix, axis=1), wo_ref[...])
    gate1 = mod_ref[2:3, :]
    x1 = _layernorm_rows(DEEPNORM_ALPHA * x_ref[...] + gate1 * y, l1g_ref[...], l1b_ref[...])
    x1_ref[...] = x1
    hmod = x1 * (1.0 + mod_ref[4:5, :]) + mod_ref[3:4, :]
    h_ref[...] = hmod.astype(BF16)

    h_hi = hmod.astype(BF16)
    h_lo = (hmod - h_hi.astype(F32)).astype(BF16)
    wr = wr_ref[...]
    w_hi = wr.astype(BF16)
    w_lo = (wr - w_hi.astype(F32)).astype(BF16)
    pp = _dot(jnp.concatenate([h_hi, h_lo], axis=0), jnp.concatenate([w_hi, w_lo], axis=1))
    logits = ((pp[0:tm, 0:LANES] + pp[0:tm, LANES:2 * LANES])
              + (pp[tm:2 * tm, 0:LANES] + pp[tm:2 * tm, LANES:2 * LANES]) + br_ref[...])

    lane = lax.broadcasted_iota(jnp.int32, (tm, LANES), 1).astype(F32)
    neg = -jnp.inf
    far = float(LANES)
    gl = jnp.where(lane < N_GROUPS, logits, neg)
    gmax = jnp.max(gl, axis=-1, keepdims=True)
    gsum = jnp.sum(jnp.exp(gl - gmax), axis=-1, keepdims=True)
    p_g = 1.0 / gsum
    g_top = jnp.min(jnp.where(gl == gmax, lane, far), axis=-1, keepdims=True)
    e0 = ROUTER_LANE0 + g_top * EXPERTS_PER_GROUP
    el = jnp.where((lane >= e0) & (lane < e0 + EXPERTS_PER_GROUP), logits, neg)
    e1max = jnp.max(el, axis=-1, keepdims=True)
    i1 = jnp.min(jnp.where(el == e1max, lane, far), axis=-1, keepdims=True)
    el2 = jnp.where(lane == i1, neg, el)
    e2max = jnp.max(el2, axis=-1, keepdims=True)
    i2 = jnp.min(jnp.where(el2 == e2max, lane, far), axis=-1, keepdims=True)
    t = jnp.exp(e2max - e1max)
    w1 = p_g / (1.0 + t)
    w2 = w1 * t
    comb_ref[...] = jnp.where(lane == i1, w1, jnp.where(lane == i2, w2, jnp.where(lane == g_top, 1.0, 0.0)))


def _outproj(o_f, o_b, main, mlat, x, mod, gg, wo, l1g, l1b, wr, br, tm):
    bsz, t, d = x.shape
    c2 = lambda b, i: (0, 0)
    row = lambda b, i: (b, i, 0)
    return pl.pallas_call(
        _outproj_kernel,
        grid=(bsz, t // tm),
        in_specs=[pl.BlockSpec((None, tm, GLA_WIDTH), row),
                  pl.BlockSpec((None, tm, GLA_WIDTH), row),
                  pl.BlockSpec((None, tm, GLA_WIDTH), lambda b, i: (b, i, COL_R // GLA_WIDTH)),
                  pl.BlockSpec((None, MLA_PAIRS, tm, LANES), lambda b, i: (b, 0, i, 0)),
                  pl.BlockSpec((None, tm, d), row),
                  pl.BlockSpec((None, 6, d), lambda b, i: (b, 0, 0)),
                  pl.BlockSpec(gg.shape, c2), pl.BlockSpec(wo.shape, c2),
                  pl.BlockSpec(l1g.shape, c2), pl.BlockSpec(l1b.shape, c2),
                  pl.BlockSpec(wr.shape, c2), pl.BlockSpec(br.shape, c2)],
        out_specs=[pl.BlockSpec((None, tm, d), row),
                   pl.BlockSpec((None, tm, d), row),
                   pl.BlockSpec((None, tm, LANES), row)],
        out_shape=(jax.ShapeDtypeStruct((bsz, t, d), F32),
                   jax.ShapeDtypeStruct((bsz, t, d), BF16),
                   jax.ShapeDtypeStruct((bsz, t, LANES), F32)),
        compiler_params=_cparams(("parallel", "parallel")),
        name="outproj",
    )(o_f, o_b, main, mlat, x, mod, gg, wo, l1g, l1b, wr, br)


def _moe_kernel(h_ref, comb_ref, x1_ref, mod_ref, wgu_ref, wd_ref, l2g_ref, l2b_ref, o_ref,
                hs_scr, cs_scr, acc_scr):
    tm = h_ref.shape[0]
    rb = MOE_ROW_BLOCK
    gw = EXPERTS_PER_GROUP * D_EXPERT
    comb = comb_ref[...]
    lane = lax.broadcasted_iota(jnp.int32, (tm, LANES), 1)
    onehot = jnp.where(lane < N_GROUPS, comb, 0.0)

    ri = lax.broadcasted_iota(jnp.int32, (tm, tm), 0)
    ci = lax.broadcasted_iota(jnp.int32, (tm, tm), 1)
    lower = jnp.where(ci < ri, 1.0, 0.0).astype(BF16)
    before = _dot(lower, onehot.astype(BF16))
    rank = jnp.sum(before * onehot, axis=-1, keepdims=True)
    totals = jnp.broadcast_to(jnp.sum(onehot, axis=0, keepdims=True), (8, LANES))
    offs = pltpu.roll(totals, 1, 1) + pltpu.roll(totals, 2, 1) + pltpu.roll(totals, 3, 1)
    pos = jnp.sum(onehot * offs[0:1, :], axis=-1, keepdims=True) + rank
    pt = jnp.where(ci.astype(F32) == pos, 1.0, 0.0).astype(BF16)

    hs_scr[0:tm, :] = _dot_tn(pt, h_ref[...]).astype(BF16)
    hs_scr[tm:tm + rb, :] = jnp.zeros((rb, hs_scr.shape[1]), BF16)
    c_hi = comb.astype(BF16)
    c_lo = (comb - c_hi.astype(F32)).astype(BF16)
    cs_scr[0:tm, :] = _dot_tn(pt, c_hi) + _dot_tn(pt, c_lo)
    cs_scr[tm:tm + rb, :] = jnp.zeros((rb, LANES), F32)
    acc_scr[...] = jnp.zeros_like(acc_scr)

    tot_i = totals.astype(jnp.int32)
    off_i = offs.astype(jnp.int32)
    lane_r = lax.broadcasted_iota(jnp.int32, (rb, LANES), 1)
    for g in range(N_GROUPS):
        n_g = tot_i[0, g]
        start = off_i[0, g]
        first = (start // BF16_SUBLANES) * BF16_SUBLANES
        n_blocks = jnp.where(n_g > 0, (start + n_g - first + rb - 1) // rb, 0)

        def block(k, carry, g=g, first=first):
            rows = pl.ds(pl.multiple_of(first + k * rb, BF16_SUBLANES), rb)
            hb = hs_scr[rows, :]
            cb = cs_scr[rows, :]
            parts = []
            for j in range(EXPERTS_PER_GROUP):
                e = g * EXPERTS_PER_GROUP + j
                w = jnp.sum(jnp.where(lane_r == ROUTER_LANE0 + e, cb, 0.0), axis=-1, keepdims=True)
                gu = _dot(hb, wgu_ref[e])
                gj = gu[:, 0:D_EXPERT]
                uj = gu[:, D_EXPERT:2 * D_EXPERT]
                parts.append(((gj * _sigmoid(gj)) * uj * w).astype(BF16))
            acc_scr[rows, :] += _dot(jnp.concatenate(parts, axis=1), wd_ref[g])
            return carry

        lax.fori_loop(0, n_blocks, block, 0)

    y = _dot(pt, acc_scr[0:tm, :].astype(BF16))
    gate2 = mod_ref[5:6, :]
    z = DEEPNORM_ALPHA * x1_ref[...] + gate2 * y
    o_ref[...] = _layernorm_rows(z, l2g_ref[...], l2b_ref[...])


def _moe(h, comb, x1, mod, wgu, wd, l2g, l2b, tm):
    bsz, t, d = x1.shape
    row = lambda b, i: (b, i, 0)
    c2 = lambda b, i: (0, 0)
    c3 = lambda b, i: (0, 0, 0)
    resident = pl.Buffered(1)
    return pl.pallas_call(
        _moe_kernel,
        grid=(bsz, t // tm),
        in_specs=[pl.BlockSpec((None, tm, d), row),
                  pl.BlockSpec((None, tm, LANES), row),
                  pl.BlockSpec((None, tm, d), row),
                  pl.BlockSpec((None, 6, d), lambda b, i: (b, 0, 0)),
                  pl.BlockSpec(wgu.shape, c3, pipeline_mode=resident),
                  pl.BlockSpec(wd.shape, c3, pipeline_mode=resident),
                  pl.BlockSpec(l2g.shape, c2), pl.BlockSpec(l2b.shape, c2)],
        out_specs=pl.BlockSpec((None, tm, d), row),
        out_shape=jax.ShapeDtypeStruct((bsz, t, d), F32),
        scratch_shapes=[pltpu.VMEM((tm + MOE_ROW_BLOCK, d), BF16),
                        pltpu.VMEM((tm + MOE_ROW_BLOCK, LANES), F32),
                        pltpu.VMEM((tm + MOE_ROW_BLOCK, d), F32)],
        compiler_params=pltpu.CompilerParams(dimension_semantics=("parallel", "parallel"),
                                             vmem_limit_bytes=MOE_VMEM_LIMIT),
        name="moe",
    )(h, comb, x1, mod, wgu, wd, l2g, l2b)


_ROPE_SWAP = np.concatenate([np.arange(8, 16), np.arange(0, 8), np.arange(24, 32), np.arange(16, 24)])


def _pack_w_in(w_in):
    d = w_in.shape[0]
    o_q, o_k, o_v, o_gf, o_gb, o_r, o_cq, o_ckv, o_kr = 0, 256, 512, 1024, 1040, 1056, 1568, 1824, 1952
    z = lambda n: jnp.zeros((d, n), w_in.dtype)
    kr = w_in[:, o_kr:o_kr + MLA_ROPE]
    cols = [w_in[:, o_q:o_k], w_in[:, o_k:o_v], w_in[:, o_v:o_gf], w_in[:, o_r:o_cq], w_in[:, o_cq:o_ckv],
            w_in[:, o_ckv:o_kr],
            w_in[:, o_gf:o_gb], w_in[:, o_gb:o_r], z(32), kr, z(32),
            z(64), kr[:, _ROPE_SWAP], z(32)]
    return jnp.concatenate(cols, axis=1).astype(BF16)


def _pack_gate_w(w_gk, lane0):
    out = jnp.zeros((LANES, w_gk.shape[1]), w_gk.dtype)
    return out.at[lane0:lane0 + GLA_RANK].set(w_gk).astype(BF16)


def _pack_w_uq(w_uq):
    r = w_uq.shape[0]
    w = w_uq.reshape(r, MLA_HEADS, MLA_NOPE + MLA_ROPE)
    nope, rope = w[..., :MLA_NOPE], w[..., MLA_NOPE:]
    z32 = jnp.zeros((r, MLA_HEADS, 32), w_uq.dtype)
    z64 = jnp.zeros((r, MLA_HEADS, 64), w_uq.dtype)
    a = jnp.concatenate([nope, rope, z32], axis=-1).reshape(r, MLA_HEADS * LANES)
    b = jnp.concatenate([z64, rope[..., _ROPE_SWAP], z32], axis=-1).reshape(r, MLA_HEADS * LANES)
    return jnp.concatenate([a, b], axis=1).astype(BF16)


def _pack_w_ukv(w_ukv):
    r = w_ukv.shape[0]
    w = w_ukv.reshape(r, MLA_HEADS, MLA_NOPE + MLA_DV)
    kn, v = w[..., :MLA_NOPE], w[..., MLA_NOPE:]
    k_t = jnp.concatenate([kn, jnp.zeros_like(kn)], axis=-1).reshape(r, MLA_HEADS * LANES)
    v_t = v.reshape(r, MLA_HEADS * MLA_DV)
    return jnp.concatenate([k_t, v_t], axis=1).astype(BF16)


def _pack_experts(w_gate, w_up, w_down):
    n_e, d, de = w_gate.shape
    wgu = jnp.concatenate([w_gate, w_up], axis=-1).astype(BF16)
    wd = w_down.reshape(N_GROUPS, EXPERTS_PER_GROUP * de, d).astype(BF16)
    return wgu, wd


def _pack_router(w_rg, b_rg, w_re, b_re):
    d = w_rg.shape[0]
    pad = LANES - N_GROUPS - N_EXPERTS
    w = jnp.concatenate([w_rg, w_re, jnp.zeros((d, pad), w_rg.dtype)], axis=1)
    b = jnp.concatenate([b_rg, b_re, jnp.zeros((pad,), b_rg.dtype)]).reshape(1, LANES)
    return w, b


def kernel(x, c, ctx, c_ctx, w_ada, b_ada, w_in, w_gk_f, b_gk_f, w_gk_b, b_gk_b, gla_norm_g, mla_q_norm_g, w_uq, mla_kv_norm_g, w_ukv, w_o, ln1_g, ln1_b, w_router_group, b_router_group, w_router_expert, b_router_expert, w_expert_gate, w_expert_up, w_expert_down, ln2_g, ln2_b):
    bsz, seq, d = x.shape
    l = 0

    c_all = jnp.concatenate([c, c_ctx[None, :], jnp.zeros((16 - bsz - 1, d), c.dtype)], axis=0)
    mod = _ada(c_all, w_ada[l], b_ada[l])
    mod_lat = mod[:bsz].reshape(bsz, 6, d)
    mod_ctx = mod[bsz:bsz + 1].reshape(1, 6, d)

    w_in_p = _pack_w_in(w_in[l])
    main_lat, misc_lat = _inproj(x, mod_lat, w_in_p, True, 512)
    main_ctx, misc_ctx = _inproj(ctx, mod_ctx, w_in_p, False, ctx.shape[1])

    wgf = _pack_gate_w(w_gk_f[l], 0)
    wgb = _pack_gate_w(w_gk_b[l], GLA_RANK)
    bgf = b_gk_f[l].reshape(1, GLA_QK)
    bgb = b_gk_b[l].reshape(1, GLA_QK)
    s_zero = jnp.zeros((bsz, 2, GLA_WIDTH, GLA_QK), F32)
    _, _, s_ctx = _gla(main_ctx, misc_ctx, wgf, bgf, wgb, bgb, s_zero, ctx.shape[1])
    o_f, o_b, _ = _gla(main_lat, misc_lat, wgf, bgf, wgb, bgb, s_ctx, 512)

    cos_t, sin_t = _rope_tables(seq)
    qg = mla_q_norm_g[l].reshape(1, MLA_Q_RANK)
    kvg = mla_kv_norm_g[l].reshape(1, MLA_KV_RANK)
    wq_p = _pack_w_uq(w_uq[l])
    wkv_p = _pack_w_ukv(w_ukv[l])
    q_lat, k_lat, v_lat = _mla_proj_lat(main_lat, misc_lat, cos_t, sin_t, qg, kvg, wq_p, wkv_p, 512)
    k_ctx, v_ctx = _mla_proj_ctx(main_ctx, misc_ctx, kvg, wkv_p)
    m_lat = _attn(q_lat, k_lat, k_ctx, v_lat, v_ctx, ATTN_TB, ATTN_TQ, ATTN_TK)

    gg = gla_norm_g[l].reshape(1, GLA_DV)
    w_r, b_r = _pack_router(w_router_group[l], b_router_group[l], w_router_expert[l], b_router_expert[l])
    x1, h, comb = _outproj(o_f, o_b, main_lat, m_lat, x, mod_lat, gg, w_o[l].astype(BF16),
                           ln1_g[l].reshape(1, d), ln1_b[l].reshape(1, d), w_r, b_r, 256)

    wgu, wd = _pack_experts(w_expert_gate[l], w_expert_up[l], w_expert_down[l])
    return _moe(h, comb, x1, mod_lat, wgu, wd, ln2_g[l].reshape(1, d), ln2_b[l].reshape(1, d), 512)
```

```python
import functools
import math

import jax
import jax.numpy as jnp
import numpy as np
from jax import lax
from jax.experimental import pallas as pl
from jax.experimental.pallas import tpu as pltpu

F32 = jnp.float32
BF16 = jnp.bfloat16

D_MODEL = 1024
GRID_W = 64
GLA_HEADS = 4
GLA_DK = 64
GLA_DV = 128
GLA_RANK = 16
GLA_GATE_NORM = 16.0
GLA_CHUNK = 64
GLA_QK = GLA_HEADS * GLA_DK
GLA_WIDTH = GLA_HEADS * GLA_DV
MLA_HEADS = 8
MLA_NOPE = 64
MLA_ROPE = 32
MLA_DV = 64
MLA_Q_RANK = 256
MLA_KV_RANK = 128
MLA_PAIRS = MLA_HEADS // 2
ROPE_BASE = 10000.0
N_GROUPS = 4
EXPERTS_PER_GROUP = 4
N_EXPERTS = 16
D_EXPERT = 256
DEPTH = 1
DEEPNORM_ALPHA = (2.0 * DEPTH) ** 0.25
EPS = 1e-6

LANES = 128
COL_Q, COL_K, COL_V, COL_R, COL_CQ, COL_CKV = 0, 256, 512, 1024, 1536, 1792
MAIN_W = 1920
MISC_W = 256
INPROJ_CHUNK = 768
ROPE_LANE0 = 64
V_ROWS = MLA_DV + 16
V_TILE = 2 * V_ROWS
ROUTER_LANE0 = N_GROUPS
VMEM_LIMIT = 48 * 1024 * 1024
MOE_VMEM_LIMIT = 56 * 1024 * 1024
BF16_SUBLANES = 16
MOE_ROW_BLOCK = 160
ATTN_BUFS = 4
ATTN_TB = 1024
ATTN_TQ = 256
ATTN_TK = 512


def _cparams(sem):
    return pltpu.CompilerParams(dimension_semantics=sem, vmem_limit_bytes=VMEM_LIMIT)


def _dot(a, b):
    return jnp.dot(a, b, preferred_element_type=F32)


def _dot_nt(a, b):
    return lax.dot_general(a, b, (((1,), (1,)), ((), ())), preferred_element_type=F32)


def _dot_tn(a, b):
    return lax.dot_general(a, b, (((0,), (0,)), ((), ())), preferred_element_type=F32)


def _sigmoid(x):
    return 1.0 / (1.0 + jnp.exp(-x))


def _ada_kernel(c_ref, w_ref, b_ref, o_ref):
    a = c_ref[...]
    a = a * _sigmoid(a)
    o_ref[...] = _dot(a.astype(BF16), w_ref[...].astype(BF16)) + b_ref[...]


def _ada(c_all, w, b):
    rows, d = c_all.shape
    n = w.shape[1]
    bn = 1536
    return pl.pallas_call(
        _ada_kernel,
        grid=(n // bn,),
        in_specs=[pl.BlockSpec((rows, d), lambda j: (0, 0)),
                  pl.BlockSpec((d, bn), lambda j: (0, j)),
                  pl.BlockSpec((1, bn), lambda j: (0, j))],
        out_specs=pl.BlockSpec((rows, bn), lambda j: (0, j)),
        out_shape=jax.ShapeDtypeStruct((rows, n), F32),
        compiler_params=_cparams(("parallel",)),
        name="ada",
    )(c_all, w, b.reshape(1, n))


def _rope_tab_kernel(cos_ref, sin_ref):
    shape = cos_ref.shape
    t = lax.broadcasted_iota(jnp.int32, shape, 0) + pl.program_id(0) * shape[0]
    lane = lax.broadcasted_iota(jnp.int32, shape, 1)
    j = lane - ROPE_LANE0
    valid = (j >= 0) & (j < MLA_ROPE)
    f = (j & 7).astype(F32)
    inv_freq = jnp.exp(f * (-math.log(ROPE_BASE) / 8.0))
    pos = jnp.where(j >= 16, t & (GRID_W - 1), jnp.right_shift(t, int(math.log2(GRID_W)))).astype(F32)
    ang = pos * inv_freq
    sign = jnp.where((j & 15) < 8, -1.0, 1.0)
    cos_ref[...] = jnp.where(valid, jnp.cos(ang), 0.0)
    sin_ref[...] = jnp.where(valid, sign * jnp.sin(ang), 0.0)


def _rope_tables(seq):
    tm = 512
    spec = pl.BlockSpec((tm, LANES), lambda i: (i, 0))
    return pl.pallas_call(
        _rope_tab_kernel,
        grid=(seq // tm,),
        out_specs=[spec, spec],
        out_shape=(jax.ShapeDtypeStruct((seq, LANES), F32), jax.ShapeDtypeStruct((seq, LANES), F32)),
        compiler_params=_cparams(("parallel",)),
        name="rope_tab",
    )()


def _inproj_kernel(x_ref, mod_ref, w_ref, main_ref, misc_ref):
    shift = mod_ref[0:1, :]
    scale = mod_ref[1:2, :]
    u = (x_ref[...] * (1.0 + scale) + shift).astype(BF16)
    for c0 in range(0, MAIN_W, INPROJ_CHUNK):
        c1 = min(c0 + INPROJ_CHUNK, MAIN_W + MISC_W)
        y = _dot(u, w_ref[:, c0:c1])
        if c1 <= MAIN_W:
            main_ref[:, c0:c1] = y.astype(BF16)
        else:
            main_ref[:, c0:MAIN_W] = y[:, 0:MAIN_W - c0].astype(BF16)
            misc_ref[...] = y[:, MAIN_W - c0:c1 - c0]


def _inproj(x, mod, w, per_batch, tm):
    bsz, t, d = x.shape
    mod_map = (lambda b, i: (b, 0, 0)) if per_batch else (lambda b, i: (0, 0, 0))
    return pl.pallas_call(
        _inproj_kernel,
        grid=(bsz, t // tm),
        in_specs=[pl.BlockSpec((None, tm, d), lambda b, i: (b, i, 0)),
                  pl.BlockSpec((None, 6, d), mod_map),
                  pl.BlockSpec(w.shape, lambda b, i: (0, 0))],
        out_specs=[pl.BlockSpec((None, tm, MAIN_W), lambda b, i: (b, i, 0)),
                   pl.BlockSpec((None, tm, MISC_W), lambda b, i: (b, i, 0))],
        out_shape=(jax.ShapeDtypeStruct((bsz, t, MAIN_W), BF16),
                   jax.ShapeDtypeStruct((bsz, t, MISC_W), F32)),
        compiler_params=_cparams(("parallel", "parallel")),
        name="inproj",
    )(x, mod, w)


def _gla_kernel(qkf_ref, vf_ref, mf_ref, qkb_ref, vb_ref, mb_ref, wgf_ref, bgf_ref, wgb_ref, bgb_ref,
                s0_ref, of_ref, ob_ref, sfin_ref, st_f, st_b, dsf_scr, dsb_scr, *, n_chunks):
    i = pl.program_id(1)
    nblk = pl.num_programs(1)
    C = GLA_CHUNK

    @pl.when(i == 0)
    def _():
        st_f[...] = s0_ref[0]
        st_b[...] = s0_ref[1]

    r64 = lax.broadcasted_iota(jnp.int32, (C, C), 0)
    c64 = lax.broadcasted_iota(jnp.int32, (C, C), 1)
    ra = lax.broadcasted_iota(jnp.int32, (GLA_HEADS * C, C), 0) & (C - 1)
    ca = lax.broadcasted_iota(jnp.int32, (GLA_HEADS * C, C), 1)
    lane_head = lax.broadcasted_iota(jnp.int32, (C, GLA_QK), 1) // GLA_DK
    head_masks = [jnp.where(lane_head == h, 1.0, 0.0) for h in range(GLA_HEADS)]

    def local_part(qk_ref, v_ref, m_ref, wg_ref, bg_ref, o_ref, ds_scr, forward):
        if forward:
            tri = jnp.where(c64 <= r64, 1.0, 0.0).astype(BF16)
            causal = ca <= ra
            last_row = C - 1
        else:
            tri = jnp.where(c64 >= r64, 1.0, 0.0).astype(BF16)
            causal = ca >= ra
            last_row = 0
        z_all = _dot(m_ref[...].astype(BF16), wg_ref[...]) + bg_ref[...]
        lg_all = (jnp.minimum(z_all, 0.0) - jnp.log(1.0 + jnp.exp(-jnp.abs(z_all)))) * (1.0 / GLA_GATE_NORM)
        chunk_rows = [slice(c * C, (c + 1) * C) for c in range(n_chunks)]
        lg_hi_all = lg_all.astype(BF16)
        lg_lo_all = (lg_all - lg_hi_all.astype(F32)).astype(BF16)
        bs = [_dot(tri, lg_hi_all[r, :]) + _dot(tri, lg_lo_all[r, :]) for r in chunk_rows]
        tots = [b[last_row:last_row + 1, :] for b in bs]
        qs = [qk_ref[r, 0:GLA_QK].astype(F32) for r in chunk_rows]
        ks = [qk_ref[r, GLA_QK:2 * GLA_QK].astype(F32) for r in chunk_rows]
        vs = [v_ref[r, :] for r in chunk_rows]
        q_es = [q * (jnp.exp(b) * (GLA_DK ** -0.5)) for q, b in zip(qs, bs)]
        k_es = [(k * jnp.exp(-b)).astype(BF16) for k, b in zip(ks, bs)]
        k_decs = [(k * jnp.exp(t - b)).astype(BF16) for k, b, t in zip(ks, bs, tots)]
        qms = [jnp.concatenate([(q_e * head_masks[h]).astype(BF16) for h in range(GLA_HEADS)], axis=0) for q_e in q_es]
        a_s = [jnp.where(causal, _dot_nt(qm, k_e), 0.0).astype(BF16) for qm, k_e in zip(qms, k_es)]
        for c in range(n_chunks):
            o_ref[chunk_rows[c], :] = jnp.concatenate(
                [_dot(a_s[c][h * C:(h + 1) * C, :], vs[c][:, h * GLA_DV:(h + 1) * GLA_DV]) for h in range(GLA_HEADS)],
                axis=1)
        for c in range(n_chunks):
            ds_scr[c] = _dot_tn(vs[c], k_decs[c])
        return [(qm, jnp.exp(t)) for qm, t in zip(qms, tots)]

    def state_step(c, qm, dec, o_ref, ds_scr, st):
        rows = slice(c * C, (c + 1) * C)
        st_b16 = st[...].astype(BF16)
        o_inter = jnp.concatenate(
            [_dot_nt(qm[h * C:(h + 1) * C, :], st_b16[h * GLA_DV:(h + 1) * GLA_DV, :]) for h in range(GLA_HEADS)],
            axis=1)
        o_ref[rows, :] += o_inter
        st[...] = st[...] * dec + ds_scr[c]

    loc_f = local_part(qkf_ref, vf_ref, mf_ref, wgf_ref, bgf_ref, of_ref, dsf_scr, True)
    loc_b = local_part(qkb_ref, vb_ref, mb_ref, wgb_ref, bgb_ref, ob_ref, dsb_scr, False)
    for c in range(n_chunks):
        cb = n_chunks - 1 - c
        state_step(c, *loc_f[c], of_ref, dsf_scr, st_f)
        state_step(cb, *loc_b[cb], ob_ref, dsb_scr, st_b)

    @pl.when(i == nblk - 1)
    def _():
        sfin_ref[0] = st_f[...]
        sfin_ref[1] = st_b[...]


def _gla(main, misc, wgf, bgf, wgb, bgb, s0, tm):
    bsz, t, _ = main.shape
    nblk = t // tm
    fwd = lambda b, i: (b, i, 0)
    bwd = lambda b, i: (b, nblk - 1 - i, 0)
    const2 = lambda b, i: (0, 0)
    kern = functools.partial(_gla_kernel, n_chunks=tm // GLA_CHUNK)
    return pl.pallas_call(
        kern,
        grid=(bsz, nblk),
        in_specs=[pl.BlockSpec((None, tm, 2 * GLA_QK), fwd),
                  pl.BlockSpec((None, tm, GLA_WIDTH), lambda b, i: (b, i, 1)),
                  pl.BlockSpec((None, tm, LANES), fwd),
                  pl.BlockSpec((None, tm, 2 * GLA_QK), bwd),
                  pl.BlockSpec((None, tm, GLA_WIDTH), lambda b, i: (b, nblk - 1 - i, 1)),
                  pl.BlockSpec((None, tm, LANES), bwd),
                  pl.BlockSpec(wgf.shape, const2), pl.BlockSpec(bgf.shape, const2),
                  pl.BlockSpec(wgb.shape, const2), pl.BlockSpec(bgb.shape, const2),
                  pl.BlockSpec((None, 2, GLA_WIDTH, GLA_QK), lambda b, i: (b, 0, 0, 0))],
        out_specs=[pl.BlockSpec((None, tm, GLA_WIDTH), fwd),
                   pl.BlockSpec((None, tm, GLA_WIDTH), bwd),
                   pl.BlockSpec((None, 2, GLA_WIDTH, GLA_QK), lambda b, i: (b, 0, 0, 0))],
        out_shape=(jax.ShapeDtypeStruct((bsz, t, GLA_WIDTH), F32),
                   jax.ShapeDtypeStruct((bsz, t, GLA_WIDTH), F32),
                   jax.ShapeDtypeStruct((bsz, 2, GLA_WIDTH, GLA_QK), F32)),
        scratch_shapes=[pltpu.VMEM((GLA_WIDTH, GLA_QK), F32), pltpu.VMEM((GLA_WIDTH, GLA_QK), F32),
                        pltpu.VMEM((tm // GLA_CHUNK, GLA_WIDTH, GLA_QK), F32),
                        pltpu.VMEM((tm // GLA_CHUNK, GLA_WIDTH, GLA_QK), F32)],
        compiler_params=_cparams(("parallel", "arbitrary")),
        name="gla",
    )(main, main, misc, main, main, misc, wgf, bgf, wgb, bgb, s0)


def _rmsnorm_rows(x, g):
    xf = x.astype(F32)
    ms = jnp.mean(xf * xf, axis=-1, keepdims=True)
    return (xf * lax.rsqrt(ms + EPS)) * g


def _mla_proj_kernel(*refs, rotate, with_q):
    if with_q:
        (cq_ref, ckv_ref, m0_ref, m1_ref, cos_ref, sin_ref, qg_ref, kvg_ref, wq_ref, wkv_ref,
         q_out, k_out, v_out) = refs
    else:
        ckv_ref, m0_ref, kvg_ref, wkv_ref, k_out, v_out = refs
    hw = MLA_HEADS * LANES
    lane = lax.broadcasted_iota(jnp.int32, m0_ref.shape, 1)
    rope_lanes = (lane >= ROPE_LANE0) & (lane < ROPE_LANE0 + MLA_ROPE)
    if rotate:
        cos = cos_ref[...]
        sin = sin_ref[...]
        kr = m0_ref[...] * cos + m1_ref[...] * sin
    else:
        kr = jnp.where(rope_lanes, m0_ref[...], 0.0)
    kv = _dot(_rmsnorm_rows(ckv_ref[...], kvg_ref[...]).astype(BF16), wkv_ref[...])
    for h in range(MLA_HEADS):
        k_out[h] = (kv[:, h * LANES:(h + 1) * LANES] + kr).astype(BF16)
    ones = jnp.ones((V_ROWS - MLA_DV, kv.shape[0]), BF16)
    for p in range(MLA_PAIRS):
        v_t = jnp.transpose(kv[:, hw + p * LANES:hw + (p + 1) * LANES]).astype(BF16)
        for hh in range(2):
            v_out[p, hh * V_ROWS:hh * V_ROWS + MLA_DV, :] = v_t[hh * MLA_DV:(hh + 1) * MLA_DV, :]
            v_out[p, hh * V_ROWS + MLA_DV:(hh + 1) * V_ROWS, :] = ones
    if with_q:
        qs = (MLA_NOPE + MLA_ROPE) ** -0.5 * math.log2(math.e)
        cq_tab = jnp.where(lane < MLA_NOPE, qs, cos * qs)
        sq_tab = sin * qs
        qq = _dot(_rmsnorm_rows(cq_ref[...], qg_ref[...]).astype(BF16), wq_ref[...])
        for h in range(MLA_HEADS):
            qa = qq[:, h * LANES:(h + 1) * LANES]
            qb = qq[:, hw + h * LANES:hw + (h + 1) * LANES]
            q_out[h] = (qa * cq_tab + qb * sq_tab).astype(BF16)


def _mla_proj_lat(main, misc, cos_t, sin_t, qg, kvg, wq, wkv, tm):
    bsz, t, _ = main.shape
    c2 = lambda b, i: (0, 0)
    kern = functools.partial(_mla_proj_kernel, rotate=True, with_q=True)
    return pl.pallas_call(
        kern,
        grid=(bsz, t // tm),
        in_specs=[pl.BlockSpec((None, tm, MLA_Q_RANK), lambda b, i: (b, i, COL_CQ // MLA_Q_RANK)),
                  pl.BlockSpec((None, tm, MLA_KV_RANK), lambda b, i: (b, i, COL_CKV // MLA_KV_RANK)),
                  pl.BlockSpec((None, tm, LANES), lambda b, i: (b, i, 0)),
                  pl.BlockSpec((None, tm, LANES), lambda b, i: (b, i, 1)),
                  pl.BlockSpec((tm, LANES), lambda b, i: (i, 0)),
                  pl.BlockSpec((tm, LANES), lambda b, i: (i, 0)),
                  pl.BlockSpec(qg.shape, c2), pl.BlockSpec(kvg.shape, c2),
                  pl.BlockSpec(wq.shape, c2), pl.BlockSpec(wkv.shape, c2)],
        out_specs=[pl.BlockSpec((None, MLA_HEADS, tm, LANES), lambda b, i: (b, 0, i, 0)),
                   pl.BlockSpec((None, MLA_HEADS, tm, LANES), lambda b, i: (b, 0, i, 0)),
                   pl.BlockSpec((None, MLA_PAIRS, V_TILE, tm), lambda b, i: (b, 0, 0, i))],
        out_shape=(jax.ShapeDtypeStruct((bsz, MLA_HEADS, t, LANES), BF16),
                   jax.ShapeDtypeStruct((bsz, MLA_HEADS, t, LANES), BF16),
                   jax.ShapeDtypeStruct((bsz, MLA_PAIRS, V_TILE, t), BF16)),
        compiler_params=_cparams(("parallel", "parallel")),
        name="mla_proj_lat",
    )(main, main, misc, misc, cos_t, sin_t, qg, kvg, wq, wkv)


def _mla_proj_ctx(main, misc, kvg, wkv):
    bsz, t, _ = main.shape
    c2 = lambda b: (0, 0)
    kern = functools.partial(_mla_proj_kernel, rotate=False, with_q=False)
    return pl.pallas_call(
        kern,
        grid=(bsz,),
        in_specs=[pl.BlockSpec((None, t, MLA_KV_RANK), lambda b: (b, 0, COL_CKV // MLA_KV_RANK)),
                  pl.BlockSpec((None, t, LANES), lambda b: (b, 0, 0)),
                  pl.BlockSpec(kvg.shape, c2), pl.BlockSpec(wkv.shape, c2)],
        out_specs=[pl.BlockSpec((None, MLA_HEADS, t, LANES), lambda b: (b, 0, 0, 0)),
                   pl.BlockSpec((None, MLA_PAIRS, V_TILE, t), lambda b: (b, 0, 0, 0))],
        out_shape=(jax.ShapeDtypeStruct((bsz, MLA_HEADS, t, LANES), BF16),
                   jax.ShapeDtypeStruct((bsz, MLA_PAIRS, V_TILE, t), BF16)),
        compiler_params=_cparams(("parallel",)),
        name="mla_proj_ctx",
    )(main, misc, kvg, wkv)


def _attn_t_kernel(q_ref, kl_ref, kc_ref, vtl_ref, vtc_ref, o_ref, *bufs, tq, tk):
    tb = q_ref.shape[1]
    s_len = kl_ref.shape[1]
    c_len = kc_ref.shape[1]
    s_bufs = bufs
    items = [(hh, sub) for sub in range(tb // tq) for hh in range(2)]
    chunks = [(kl_ref, vtl_ref, c0, tk, c0) for c0 in range(0, s_len, tk)] + [(kc_ref, vtc_ref, 0, c_len, s_len)]

    def pass1_chunk(idx, c, m):
        hh, sub = items[idx]
        k_ref, _, r0, n, row = chunks[c]
        q = q_ref[hh, sub * tq:(sub + 1) * tq, :]
        s_t = _dot_nt(k_ref[hh, r0:r0 + n, :], q)
        s_bufs[idx % ATTN_BUFS][row:row + n, :] = s_t
        return jnp.maximum(m, jnp.max(s_t, axis=0, keepdims=True))

    def pass2_chunk(idx, c, m, acc):
        hh, _ = items[idx]
        _, vt_ref, r0, n, row = chunks[c]
        p_t = jnp.exp2((s_bufs[idx % ATTN_BUFS][row:row + n, :] - m).astype(BF16))
        return acc + _dot(vt_ref[hh * V_ROWS:(hh + 1) * V_ROWS, r0:r0 + n], p_t)

    outs = {}
    m_prev = None
    for idx in range(len(items) + 1):
        m_new = jnp.full((1, tq), -jnp.inf, F32)
        acc = jnp.zeros((V_ROWS, tq), F32)
        for c in range(len(chunks)):
            if idx > 0:
                acc = pass2_chunk(idx - 1, c, m_prev, acc)
            if idx < len(items):
                m_new = pass1_chunk(idx, c, m_new)
        if idx > 0:
            outs[items[idx - 1]] = acc[0:MLA_DV, :] * (1.0 / acc[MLA_DV:MLA_DV + 1, :])
        m_prev = m_new
    for sub in range(tb // tq):
        o_t = jnp.concatenate([outs[(0, sub)], outs[(1, sub)]], axis=0)
        o_ref[sub * tq:(sub + 1) * tq, :] = jnp.transpose(o_t).astype(BF16)


def _attn(q, k_lat, k_ctx, vt_lat, vt_ctx, tb, tq, tk):
    bsz, _, s_len, _ = q.shape
    c_len = k_ctx.shape[2]
    kern = functools.partial(_attn_t_kernel, tq=tq, tk=tk)
    return pl.pallas_call(
        kern,
        grid=(bsz, MLA_PAIRS, s_len // tb),
        in_specs=[pl.BlockSpec((None, 2, tb, LANES), lambda b, p, i: (b, p, i, 0)),
                  pl.BlockSpec((None, 2, s_len, LANES), lambda b, p, i: (b, p, 0, 0)),
                  pl.BlockSpec((None, 2, c_len, LANES), lambda b, p, i: (b, p, 0, 0)),
                  pl.BlockSpec((None, None, V_TILE, s_len), lambda b, p, i: (b, p, 0, 0)),
                  pl.BlockSpec((None, None, V_TILE, c_len), lambda b, p, i: (b, p, 0, 0))],
        out_specs=pl.BlockSpec((None, None, tb, LANES), lambda b, p, i: (b, p, i, 0)),
        out_shape=jax.ShapeDtypeStruct((bsz, MLA_PAIRS, s_len, LANES), BF16),
        scratch_shapes=[pltpu.VMEM((s_len + c_len, tq), F32) for _ in range(ATTN_BUFS)],
        compiler_params=_cparams(("parallel", "parallel", "arbitrary")),
        name="attn",
    )(q, k_lat, k_ctx, vt_lat, vt_ctx)


def _layernorm_rows(z, g, b):
    mu = jnp.mean(z, axis=-1, keepdims=True)
    zc = z - mu
    var = jnp.mean(zc * zc, axis=-1, keepdims=True)
    return (zc * lax.rsqrt(var + EPS)) * g + b


def _outproj_kernel(of_ref, ob_ref, r_ref, ml_ref, x_ref, mod_ref, gg_ref, wo_ref, l1g_ref, l1b_ref,
                    wr_ref, br_ref, x1_ref, h_ref, comb_ref):
    tm = x_ref.shape[0]
    o = of_ref[...] + ob_ref[...]
    r = r_ref[...].astype(F32)
    gg = gg_ref[...]
    mix = []
    for h in range(GLA_HEADS):
        sl = slice(h * GLA_DV, (h + 1) * GLA_DV)
        oh = o[:, sl]
        ms = jnp.mean(oh * oh, axis=-1, keepdims=True)
        rh = r[:, sl]
        mix.append(((oh * lax.rsqrt(ms + EPS)) * gg * (rh * _sigmoid(rh))).astype(BF16))
    mix += [ml_ref[p] for p in range(MLA_PAIRS)]
    y = _dot(jnp.concatenate(mix, axis=1), wo_ref[...])
    gate1 = mod_ref[2:3, :]
    x1 = _layernorm_rows(DEEPNORM_ALPHA * x_ref[...] + gate1 * y, l1g_ref[...], l1b_ref[...])
    x1_ref[...] = x1
    hmod = x1 * (1.0 + mod_ref[4:5, :]) + mod_ref[3:4, :]
    h_ref[...] = hmod.astype(BF16)

    h_hi = hmod.astype(BF16)
    h_lo = (hmod - h_hi.astype(F32)).astype(BF16)
    wr = wr_ref[...]
    w_hi = wr.astype(BF16)
    w_lo = (wr - w_hi.astype(F32)).astype(BF16)
    pp = _dot(jnp.concatenate([h_hi, h_lo], axis=0), jnp.concatenate([w_hi, w_lo], axis=1))
    logits = ((pp[0:tm, 0:LANES] + pp[0:tm, LANES:2 * LANES])
              + (pp[tm:2 * tm, 0:LANES] + pp[tm:2 * tm, LANES:2 * LANES]) + br_ref[...])

    lane = lax.broadcasted_iota(jnp.int32, (tm, LANES), 1).astype(F32)
    neg = -jnp.inf
    far = float(LANES)
    gl = jnp.where(lane < N_GROUPS, logits, neg)
    gmax = jnp.max(gl, axis=-1, keepdims=True)
    gsum = jnp.sum(jnp.exp(gl - gmax), axis=-1, keepdims=True)
    p_g = 1.0 / gsum
    g_top = jnp.min(jnp.where(gl == gmax, lane, far), axis=-1, keepdims=True)
    e0 = ROUTER_LANE0 + g_top * EXPERTS_PER_GROUP
    el = jnp.where((lane >= e0) & (lane < e0 + EXPERTS_PER_GROUP), logits, neg)
    e1max = jnp.max(el, axis=-1, keepdims=True)
    i1 = jnp.min(jnp.where(el == e1max, lane, far), axis=-1, keepdims=True)
    el2 = jnp.where(lane == i1, neg, el)
    e2max = jnp.max(el2, axis=-1, keepdims=True)
    i2 = jnp.min(jnp.where(el2 == e2max, lane, far), axis=-1, keepdims=True)
    t = jnp.exp(e2max - e1max)
    w1 = p_g / (1.0 + t)
    w2 = w1 * t
    comb_ref[...] = jnp.where(lane == i1, w1, jnp.where(lane == i2, w2, jnp.where(lane == g_top, 1.0, 0.0)))


def _outproj(o_f, o_b, main, mlat, x, mod, gg, wo, l1g, l1b, wr, br, tm):
    bsz, t, d = x.shape
    c2 = lambda b, i: (0, 0)
    row = lambda b, i: (b, i, 0)
    return pl.pallas_call(
        _outproj_kernel,
        grid=(bsz, t // tm),
        in_specs=[pl.BlockSpec((None, tm, GLA_WIDTH), row),
                  pl.BlockSpec((None, tm, GLA_WIDTH), row),
                  pl.BlockSpec((None, tm, GLA_WIDTH), lambda b, i: (b, i, COL_R // GLA_WIDTH)),
                  pl.BlockSpec((None, MLA_PAIRS, tm, LANES), lambda b, i: (b, 0, i, 0)),
                  pl.BlockSpec((None, tm, d), row),
                  pl.BlockSpec((None, 6, d), lambda b, i: (b, 0, 0)),
                  pl.BlockSpec(gg.shape, c2), pl.BlockSpec(wo.shape, c2),
                  pl.BlockSpec(l1g.shape, c2), pl.BlockSpec(l1b.shape, c2),
                  pl.BlockSpec(wr.shape, c2), pl.BlockSpec(br.shape, c2)],
        out_specs=[pl.BlockSpec((None, tm, d), row),
                   pl.BlockSpec((None, tm, d), row),
                   pl.BlockSpec((None, tm, LANES), row)],
        out_shape=(jax.ShapeDtypeStruct((bsz, t, d), F32),
                   jax.ShapeDtypeStruct((bsz, t, d), BF16),
                   jax.ShapeDtypeStruct((bsz, t, LANES), F32)),
        compiler_params=_cparams(("parallel", "parallel")),
        name="outproj",
    )(o_f, o_b, main, mlat, x, mod, gg, wo, l1g, l1b, wr, br)


def _moe_kernel(h_ref, comb_ref, x1_ref, mod_ref, wgu_ref, wd_ref, l2g_ref, l2b_ref, o_ref,
                hs_scr, cs_scr, acc_scr):
    tm = h_ref.shape[0]
    rb = MOE_ROW_BLOCK
    gw = EXPERTS_PER_GROUP * D_EXPERT
    comb = comb_ref[...]
    lane = lax.broadcasted_iota(jnp.int32, (tm, LANES), 1)
    onehot = jnp.where(lane < N_GROUPS, comb, 0.0)

    ri = lax.broadcasted_iota(jnp.int32, (tm, tm), 0)
    ci = lax.broadcasted_iota(jnp.int32, (tm, tm), 1)
    lower = jnp.where(ci < ri, 1.0, 0.0).astype(BF16)
    before = _dot(lower, onehot.astype(BF16))
    rank = jnp.sum(before * onehot, axis=-1, keepdims=True)
    totals = jnp.broadcast_to(jnp.sum(onehot, axis=0, keepdims=True), (8, LANES))
    offs = pltpu.roll(totals, 1, 1) + pltpu.roll(totals, 2, 1) + pltpu.roll(totals, 3, 1)
    pos = jnp.sum(onehot * offs[0:1, :], axis=-1, keepdims=True) + rank
    pt = jnp.where(ci.astype(F32) == pos, 1.0, 0.0).astype(BF16)

    hs_scr[0:tm, :] = _dot_tn(pt, h_ref[...]).astype(BF16)
    hs_scr[tm:tm + rb, :] = jnp.zeros((rb, hs_scr.shape[1]), BF16)
    c_hi = comb.astype(BF16)
    c_lo = (comb - c_hi.astype(F32)).astype(BF16)
    cs_scr[0:tm, :] = _dot_tn(pt, c_hi) + _dot_tn(pt, c_lo)
    cs_scr[tm:tm + rb, :] = jnp.zeros((rb, LANES), F32)
    acc_scr[...] = jnp.zeros_like(acc_scr)

    tot_i = totals.astype(jnp.int32)
    off_i = offs.astype(jnp.int32)
    lane_r = lax.broadcasted_iota(jnp.int32, (rb, LANES), 1)
    for g in range(N_GROUPS):
        n_g = tot_i[0, g]
        start = off_i[0, g]
        first = (start // BF16_SUBLANES) * BF16_SUBLANES
        n_blocks = jnp.where(n_g > 0, (start + n_g - first + rb - 1) // rb, 0)

        def block(k, carry, g=g, first=first):
            rows = pl.ds(pl.multiple_of(first + k * rb, BF16_SUBLANES), rb)
            hb = hs_scr[rows, :]
            cb = cs_scr[rows, :]
            parts = []
            for j in range(EXPERTS_PER_GROUP):
                e = g * EXPERTS_PER_GROUP + j
                w = jnp.sum(jnp.where(lane_r == ROUTER_LANE0 + e, cb, 0.0), axis=-1, keepdims=True)
                gu = _dot(hb, wgu_ref[e])
                gj = gu[:, 0:D_EXPERT]
                uj = gu[:, D_EXPERT:2 * D_EXPERT]
                parts.append(((gj * _sigmoid(gj)) * uj * w).astype(BF16))
            acc_scr[rows, :] += _dot(jnp.concatenate(parts, axis=1), wd_ref[g])
            return carry

        lax.fori_loop(0, n_blocks, block, 0)

    y = _dot(pt, acc_scr[0:tm, :].astype(BF16))
    gate2 = mod_ref[5:6, :]
    z = DEEPNORM_ALPHA * x1_ref[...] + gate2 * y
    o_ref[...] = _layernorm_rows(z, l2g_ref[...], l2b_ref[...])


def _moe(h, comb, x1, mod, wgu, wd, l2g, l2b, tm):
    bsz, t, d = x1.shape
    row = lambda b, i: (b, i, 0)
    c2 = lambda b, i: (0, 0)
    c3 = lambda b, i: (0, 0, 0)
    resident = pl.Buffered(1)
    return pl.pallas_call(
        _moe_kernel,
        grid=(bsz, t // tm),
        in_specs=[pl.BlockSpec((None, tm, d), row),
                  pl.BlockSpec((None, tm, LANES), row),
                  pl.BlockSpec((None, tm, d), row),
                  pl.BlockSpec((None, 6, d), lambda b, i: (b, 0, 0)),
                  pl.BlockSpec(wgu.shape, c3, pipeline_mode=resident),
                  pl.BlockSpec(wd.shape, c3, pipeline_mode=resident),
                  pl.BlockSpec(l2g.shape, c2), pl.BlockSpec(l2b.shape, c2)],
        out_specs=pl.BlockSpec((None, tm, d), row),
        out_shape=jax.ShapeDtypeStruct((bsz, t, d), F32),
        scratch_shapes=[pltpu.VMEM((tm + MOE_ROW_BLOCK, d), BF16),
                        pltpu.VMEM((tm + MOE_ROW_BLOCK, LANES), F32),
                        pltpu.VMEM((tm + MOE_ROW_BLOCK, d), F32)],
        compiler_params=pltpu.CompilerParams(dimension_semantics=("parallel", "parallel"),
                                             vmem_limit_bytes=MOE_VMEM_LIMIT),
        name="moe",
    )(h, comb, x1, mod, wgu, wd, l2g, l2b)


_ROPE_SWAP = np.concatenate([np.arange(8, 16), np.arange(0, 8), np.arange(24, 32), np.arange(16, 24)])


def _pack_w_in(w_in):
    d = w_in.shape[0]
    o_q, o_k, o_v, o_gf, o_gb, o_r, o_cq, o_ckv, o_kr = 0, 256, 512, 1024, 1040, 1056, 1568, 1824, 1952
    z = lambda n: jnp.zeros((d, n), w_in.dtype)
    kr = w_in[:, o_kr:o_kr + MLA_ROPE]
    cols = [w_in[:, o_q:o_k], w_in[:, o_k:o_v], w_in[:, o_v:o_gf], w_in[:, o_r:o_cq], w_in[:, o_cq:o_ckv],
            w_in[:, o_ckv:o_kr],
            w_in[:, o_gf:o_gb], w_in[:, o_gb:o_r], z(32), kr, z(32),
            z(64), kr[:, _ROPE_SWAP], z(32)]
    return jnp.concatenate(cols, axis=1).astype(BF16)


def _pack_gate_w(w_gk, lane0):
    out = jnp.zeros((LANES, w_gk.shape[1]), w_gk.dtype)
    return out.at[lane0:lane0 + GLA_RANK].set(w_gk).astype(BF16)


def _pack_w_uq(w_uq):
    r = w_uq.shape[0]
    w = w_uq.reshape(r, MLA_HEADS, MLA_NOPE + MLA_ROPE)
    nope, rope = w[..., :MLA_NOPE], w[..., MLA_NOPE:]
    z32 = jnp.zeros((r, MLA_HEADS, 32), w_uq.dtype)
    z64 = jnp.zeros((r, MLA_HEADS, 64), w_uq.dtype)
    a = jnp.concatenate([nope, rope, z32], axis=-1).reshape(r, MLA_HEADS * LANES)
    b = jnp.concatenate([z64, rope[..., _ROPE_SWAP], z32], axis=-1).reshape(r, MLA_HEADS * LANES)
    return jnp.concatenate([a, b], axis=1).astype(BF16)


def _pack_w_ukv(w_ukv):
    r = w_ukv.shape[0]
    w = w_ukv.reshape(r, MLA_HEADS, MLA_NOPE + MLA_DV)
    kn, v = w[..., :MLA_NOPE], w[..., MLA_NOPE:]
    k_t = jnp.concatenate([kn, jnp.zeros_like(kn)], axis=-1).reshape(r, MLA_HEADS * LANES)
    v_t = v.reshape(r, MLA_HEADS * MLA_DV)
    return jnp.concatenate([k_t, v_t], axis=1).astype(BF16)


def _pack_experts(w_gate, w_up, w_down):
    n_e, d, de = w_gate.shape
    wgu = jnp.concatenate([w_gate, w_up], axis=-1).astype(BF16)
    wd = w_down.reshape(N_GROUPS, EXPERTS_PER_GROUP * de, d).astype(BF16)
    return wgu, wd


def _pack_router(w_rg, b_rg, w_re, b_re):
    d = w_rg.shape[0]
    pad = LANES - N_GROUPS - N_EXPERTS
    w = jnp.concatenate([w_rg, w_re, jnp.zeros((d, pad), w_rg.dtype)], axis=1)
    b = jnp.concatenate([b_rg, b_re, jnp.zeros((pad,), b_rg.dtype)]).reshape(1, LANES)
    return w, b


def kernel(x, c, ctx, c_ctx, w_ada, b_ada, w_in, w_gk_f, b_gk_f, w_gk_b, b_gk_b, gla_norm_g, mla_q_norm_g, w_uq, mla_kv_norm_g, w_ukv, w_o, ln1_g, ln1_b, w_router_group, b_router_group, w_router_expert, b_router_expert, w_expert_gate, w_expert_up, w_expert_down, ln2_g, ln2_b):
    bsz, seq, d = x.shape
    l = 0

    c_all = jnp.concatenate([c, c_ctx[None, :], jnp.zeros((16 - bsz - 1, d), c.dtype)], axis=0)
    mod = _ada(c_all, w_ada[l], b_ada[l])
    mod_lat = mod[:bsz].reshape(bsz, 6, d)
    mod_ctx = mod[bsz:bsz + 1].reshape(1, 6, d)

    w_in_p = _pack_w_in(w_in[l])
    main_lat, misc_lat = _inproj(x, mod_lat, w_in_p, True, 512)
    main_ctx, misc_ctx = _inproj(ctx, mod_ctx, w_in_p, False, ctx.shape[1])

    wgf = _pack_gate_w(w_gk_f[l], 0)
    wgb = _pack_gate_w(w_gk_b[l], GLA_RANK)
    bgf = b_gk_f[l].reshape(1, GLA_QK)
    bgb = b_gk_b[l].reshape(1, GLA_QK)
    s_zero = jnp.zeros((bsz, 2, GLA_WIDTH, GLA_QK), F32)
    _, _, s_ctx = _gla(main_ctx, misc_ctx, wgf, bgf, wgb, bgb, s_zero, ctx.shape[1])
    o_f, o_b, _ = _gla(main_lat, misc_lat, wgf, bgf, wgb, bgb, s_ctx, 512)

    cos_t, sin_t = _rope_tables(seq)
    qg = mla_q_norm_g[l].reshape(1, MLA_Q_RANK)
    kvg = mla_kv_norm_g[l].reshape(1, MLA_KV_RANK)
    wq_p = _pack_w_uq(w_uq[l])
    wkv_p = _pack_w_ukv(w_ukv[l])
    q_lat, k_lat, v_lat = _mla_proj_lat(main_lat, misc_lat, cos_t, sin_t, qg, kvg, wq_p, wkv_p, 512)
    k_ctx, v_ctx = _mla_proj_ctx(main_ctx, misc_ctx, kvg, wkv_p)
    m_lat = _attn(q_lat, k_lat, k_ctx, v_lat, v_ctx, ATTN_TB, ATTN_TQ, ATTN_TK)

    gg = gla_norm_g[l].reshape(1, GLA_DV)
    w_r, b_r = _pack_router(w_router_group[l], b_router_group[l], w_router_expert[l], b_router_expert[l])
    x1, h, comb = _outproj(o_f, o_b, main_lat, m_lat, x, mod_lat, gg, w_o[l].astype(BF16),
                           ln1_g[l].reshape(1, d), ln1_b[l].reshape(1, d), w_r, b_r, 256)

    wgu, wd = _pack_experts(w_expert_gate[l], w_expert_up[l], w_expert_down[l])
    return _moe(h, comb, x1, mod_lat, wgu, wd, ln2_g[l].reshape(1, d), ln2_b[l].reshape(1, d), 512)
```

```python
import functools
import math

import jax
import jax.numpy as jnp
import numpy as np
from jax import lax
from jax.experimental import pallas as pl
from jax.experimental.pallas import tpu as pltpu

F32 = jnp.float32
BF16 = jnp.bfloat16

D_MODEL = 1024
GRID_W = 64
GLA_HEADS = 4
GLA_DK = 64
GLA_DV = 128
GLA_RANK = 16
GLA_GATE_NORM = 16.0
GLA_CHUNK = 64
GLA_QK = GLA_HEADS * GLA_DK
GLA_WIDTH = GLA_HEADS * GLA_DV
MLA_HEADS = 8
MLA_NOPE = 64
MLA_ROPE = 32
MLA_DV = 64
MLA_Q_RANK = 256
MLA_KV_RANK = 128
MLA_PAIRS = MLA_HEADS // 2
ROPE_BASE = 10000.0
N_GROUPS = 4
EXPERTS_PER_GROUP = 4
N_EXPERTS = 16
D_EXPERT = 256
DEPTH = 1
DEEPNORM_ALPHA = (2.0 * DEPTH) ** 0.25
EPS = 1e-6

LANES = 128
COL_Q, COL_K, COL_V, COL_R, COL_CQ, COL_CKV = 0, 256, 512, 1024, 1536, 1792
MAIN_W = 1920
MISC_W = 256
INPROJ_CHUNK = 768
ROPE_LANE0 = 64
V_TILE = LANES
ROUTER_LANE0 = N_GROUPS
VMEM_LIMIT = 48 * 1024 * 1024
MOE_VMEM_LIMIT = 56 * 1024 * 1024
BF16_SUBLANES = 16
MOE_ROW_BLOCK = 160
ATTN_PIPES = 2
ATTN_BUFS = 2 * ATTN_PIPES
ATTN_TB = 2048
ATTN_TQ = 256
ATTN_TK = 1024


def _cparams(sem):
    return pltpu.CompilerParams(dimension_semantics=sem, vmem_limit_bytes=VMEM_LIMIT)


def _dot(a, b):
    return jnp.dot(a, b, preferred_element_type=F32)


def _dot_nt(a, b):
    return lax.dot_general(a, b, (((1,), (1,)), ((), ())), preferred_element_type=F32)


def _dot_tn(a, b):
    return lax.dot_general(a, b, (((0,), (0,)), ((), ())), preferred_element_type=F32)


def _sigmoid(x):
    return 1.0 / (1.0 + jnp.exp(-x))


def _ada_kernel(c_ref, w_ref, b_ref, o_ref):
    a = c_ref[...]
    a = a * _sigmoid(a)
    o_ref[...] = _dot(a.astype(BF16), w_ref[...].astype(BF16)) + b_ref[...]


def _ada(c_all, w, b):
    rows, d = c_all.shape
    n = w.shape[1]
    bn = 1536
    return pl.pallas_call(
        _ada_kernel,
        grid=(n // bn,),
        in_specs=[pl.BlockSpec((rows, d), lambda j: (0, 0)),
                  pl.BlockSpec((d, bn), lambda j: (0, j)),
                  pl.BlockSpec((1, bn), lambda j: (0, j))],
        out_specs=pl.BlockSpec((rows, bn), lambda j: (0, j)),
        out_shape=jax.ShapeDtypeStruct((rows, n), F32),
        compiler_params=_cparams(("parallel",)),
        name="ada",
    )(c_all, w, b.reshape(1, n))


def _rope_tab_kernel(cos_ref, sin_ref):
    shape = cos_ref.shape
    t = lax.broadcasted_iota(jnp.int32, shape, 0) + pl.program_id(0) * shape[0]
    lane = lax.broadcasted_iota(jnp.int32, shape, 1)
    j = lane - ROPE_LANE0
    valid = (j >= 0) & (j < MLA_ROPE)
    f = (j & 7).astype(F32)
    inv_freq = jnp.exp(f * (-math.log(ROPE_BASE) / 8.0))
    pos = jnp.where(j >= 16, t & (GRID_W - 1), jnp.right_shift(t, int(math.log2(GRID_W)))).astype(F32)
    ang = pos * inv_freq
    sign = jnp.where((j & 15) < 8, -1.0, 1.0)
    cos_ref[...] = jnp.where(valid, jnp.cos(ang), 0.0)
    sin_ref[...] = jnp.where(valid, sign * jnp.sin(ang), 0.0)


def _rope_tables(seq):
    tm = 512
    spec = pl.BlockSpec((tm, LANES), lambda i: (i, 0))
    return pl.pallas_call(
        _rope_tab_kernel,
        grid=(seq // tm,),
        out_specs=[spec, spec],
        out_shape=(jax.ShapeDtypeStruct((seq, LANES), F32), jax.ShapeDtypeStruct((seq, LANES), F32)),
        compiler_params=_cparams(("parallel",)),
        name="rope_tab",
    )()


def _inproj_kernel(x_ref, mod_ref, w_ref, main_ref, misc_ref):
    shift = mod_ref[0:1, :]
    scale = mod_ref[1:2, :]
    u = (x_ref[...] * (1.0 + scale) + shift).astype(BF16)
    for c0 in range(0, MAIN_W, INPROJ_CHUNK):
        c1 = min(c0 + INPROJ_CHUNK, MAIN_W + MISC_W)
        y = _dot(u, w_ref[:, c0:c1])
        if c1 <= MAIN_W:
            main_ref[:, c0:c1] = y.astype(BF16)
        else:
            main_ref[:, c0:MAIN_W] = y[:, 0:MAIN_W - c0].astype(BF16)
            misc_ref[...] = y[:, MAIN_W - c0:c1 - c0]


def _inproj(x, mod, w, per_batch, tm):
    bsz, t, d = x.shape
    mod_map = (lambda b, i: (b, 0, 0)) if per_batch else (lambda b, i: (0, 0, 0))
    return pl.pallas_call(
        _inproj_kernel,
        grid=(bsz, t // tm),
        in_specs=[pl.BlockSpec((None, tm, d), lambda b, i: (b, i, 0)),
                  pl.BlockSpec((None, 6, d), mod_map),
                  pl.BlockSpec(w.shape, lambda b, i: (0, 0))],
        out_specs=[pl.BlockSpec((None, tm, MAIN_W), lambda b, i: (b, i, 0)),
                   pl.BlockSpec((None, tm, MISC_W), lambda b, i: (b, i, 0))],
        out_shape=(jax.ShapeDtypeStruct((bsz, t, MAIN_W), BF16),
                   jax.ShapeDtypeStruct((bsz, t, MISC_W), F32)),
        compiler_params=_cparams(("parallel", "parallel")),
        name="inproj",
    )(x, mod, w)


def _gla_kernel(qkf_ref, vf_ref, mf_ref, qkb_ref, vb_ref, mb_ref, wgf_ref, bgf_ref, wgb_ref, bgb_ref,
                s0_ref, of_ref, ob_ref, sfin_ref, st_f, st_b, dsf_scr, dsb_scr, *, n_chunks):
    i = pl.program_id(1)
    nblk = pl.num_programs(1)
    C = GLA_CHUNK

    @pl.when(i == 0)
    def _():
        st_f[...] = s0_ref[0]
        st_b[...] = s0_ref[1]

    r64 = lax.broadcasted_iota(jnp.int32, (C, C), 0)
    c64 = lax.broadcasted_iota(jnp.int32, (C, C), 1)
    ra = lax.broadcasted_iota(jnp.int32, (GLA_HEADS * C, C), 0) & (C - 1)
    ca = lax.broadcasted_iota(jnp.int32, (GLA_HEADS * C, C), 1)
    lane_head = lax.broadcasted_iota(jnp.int32, (C, GLA_QK), 1) // GLA_DK
    head_masks = [jnp.where(lane_head == h, 1.0, 0.0) for h in range(GLA_HEADS)]

    def local_part(qk_ref, v_ref, m_ref, wg_ref, bg_ref, o_ref, ds_scr, forward):
        if forward:
            tri = jnp.where(c64 <= r64, 1.0, 0.0).astype(BF16)
            causal = ca <= ra
            last_row = C - 1
        else:
            tri = jnp.where(c64 >= r64, 1.0, 0.0).astype(BF16)
            causal = ca >= ra
            last_row = 0
        z_all = _dot(m_ref[...].astype(BF16), wg_ref[...]) + bg_ref[...]
        lg_all = (jnp.minimum(z_all, 0.0) - jnp.log(1.0 + jnp.exp(-jnp.abs(z_all)))) * (1.0 / GLA_GATE_NORM)
        chunk_rows = [slice(c * C, (c + 1) * C) for c in range(n_chunks)]
        lg_hi_all = lg_all.astype(BF16)
        lg_lo_all = (lg_all - lg_hi_all.astype(F32)).astype(BF16)
        bs = [_dot(tri, lg_hi_all[r, :]) + _dot(tri, lg_lo_all[r, :]) for r in chunk_rows]
        tots = [b[last_row:last_row + 1, :] for b in bs]
        qs = [qk_ref[r, 0:GLA_QK].astype(F32) for r in chunk_rows]
        ks = [qk_ref[r, GLA_QK:2 * GLA_QK].astype(F32) for r in chunk_rows]
        vs = [v_ref[r, :] for r in chunk_rows]
        q_es = [q * (jnp.exp(b) * (GLA_DK ** -0.5)) for q, b in zip(qs, bs)]
        k_es = [(k * jnp.exp(-b)).astype(BF16) for k, b in zip(ks, bs)]
        k_decs = [(k * jnp.exp(t - b)).astype(BF16) for k, b, t in zip(ks, bs, tots)]
        qms = [jnp.concatenate([(q_e * head_masks[h]).astype(BF16) for h in range(GLA_HEADS)], axis=0) for q_e in q_es]
        a_s = [jnp.where(causal, _dot_nt(qm, k_e), 0.0).astype(BF16) for qm, k_e in zip(qms, k_es)]
        for c in range(n_chunks):
            o_ref[chunk_rows[c], :] = jnp.concatenate(
                [_dot(a_s[c][h * C:(h + 1) * C, :], vs[c][:, h * GLA_DV:(h + 1) * GLA_DV]) for h in range(GLA_HEADS)],
                axis=1)
        for c in range(n_chunks):
            ds_scr[c] = _dot_tn(vs[c], k_decs[c])
        return [(qm, jnp.exp(t)) for qm, t in zip(qms, tots)]

    def state_step(c, qm, dec, o_ref, ds_scr, st):
        rows = slice(c * C, (c + 1) * C)
        st_b16 = st[...].astype(BF16)
        o_inter = jnp.concatenate(
            [_dot_nt(qm[h * C:(h + 1) * C, :], st_b16[h * GLA_DV:(h + 1) * GLA_DV, :]) for h in range(GLA_HEADS)],
            axis=1)
        o_ref[rows, :] += o_inter
        st[...] = st[...] * dec + ds_scr[c]

    loc_f = local_part(qkf_ref, vf_ref, mf_ref, wgf_ref, bgf_ref, of_ref, dsf_scr, True)
    loc_b = local_part(qkb_ref, vb_ref, mb_ref, wgb_ref, bgb_ref, ob_ref, dsb_scr, False)
    for c in range(n_chunks):
        cb = n_chunks - 1 - c
        state_step(c, *loc_f[c], of_ref, dsf_scr, st_f)
        state_step(cb, *loc_b[cb], ob_ref, dsb_scr, st_b)

    @pl.when(i == nblk - 1)
    def _():
        sfin_ref[0] = st_f[...]
        sfin_ref[1] = st_b[...]


def _gla(main, misc, wgf, bgf, wgb, bgb, s0, tm):
    bsz, t, _ = main.shape
    nblk = t // tm
    fwd = lambda b, i: (b, i, 0)
    bwd = lambda b, i: (b, nblk - 1 - i, 0)
    const2 = lambda b, i: (0, 0)
    kern = functools.partial(_gla_kernel, n_chunks=tm // GLA_CHUNK)
    return pl.pallas_call(
        kern,
        grid=(bsz, nblk),
        in_specs=[pl.BlockSpec((None, tm, 2 * GLA_QK), fwd),
                  pl.BlockSpec((None, tm, GLA_WIDTH), lambda b, i: (b, i, 1)),
                  pl.BlockSpec((None, tm, LANES), fwd),
                  pl.BlockSpec((None, tm, 2 * GLA_QK), bwd),
                  pl.BlockSpec((None, tm, GLA_WIDTH), lambda b, i: (b, nblk - 1 - i, 1)),
                  pl.BlockSpec((None, tm, LANES), bwd),
                  pl.BlockSpec(wgf.shape, const2), pl.BlockSpec(bgf.shape, const2),
                  pl.BlockSpec(wgb.shape, const2), pl.BlockSpec(bgb.shape, const2),
                  pl.BlockSpec((None, 2, GLA_WIDTH, GLA_QK), lambda b, i: (b, 0, 0, 0))],
        out_specs=[pl.BlockSpec((None, tm, GLA_WIDTH), fwd),
                   pl.BlockSpec((None, tm, GLA_WIDTH), bwd),
                   pl.BlockSpec((None, 2, GLA_WIDTH, GLA_QK), lambda b, i: (b, 0, 0, 0))],
        out_shape=(jax.ShapeDtypeStruct((bsz, t, GLA_WIDTH), F32),
                   jax.ShapeDtypeStruct((bsz, t, GLA_WIDTH), F32),
                   jax.ShapeDtypeStruct((bsz, 2, GLA_WIDTH, GLA_QK), F32)),
        scratch_shapes=[pltpu.VMEM((GLA_WIDTH, GLA_QK), F32), pltpu.VMEM((GLA_WIDTH, GLA_QK), F32),
                        pltpu.VMEM((tm // GLA_CHUNK, GLA_WIDTH, GLA_QK), F32),
                        pltpu.VMEM((tm // GLA_CHUNK, GLA_WIDTH, GLA_QK), F32)],
        compiler_params=_cparams(("parallel", "arbitrary")),
        name="gla",
    )(main, main, misc, main, main, misc, wgf, bgf, wgb, bgb, s0)


def _rmsnorm_rows(x, g):
    xf = x.astype(F32)
    ms = jnp.mean(xf * xf, axis=-1, keepdims=True)
    return (xf * lax.rsqrt(ms + EPS)) * g


def _mla_proj_kernel(*refs, rotate, with_q):
    if with_q:
        (cq_ref, ckv_ref, m0_ref, m1_ref, cos_ref, sin_ref, qg_ref, kvg_ref, wq_ref, wkv_ref,
         q_out, k_out, v_out) = refs
    else:
        ckv_ref, m0_ref, kvg_ref, wkv_ref, k_out, v_out = refs
    hw = MLA_HEADS * LANES
    lane = lax.broadcasted_iota(jnp.int32, m0_ref.shape, 1)
    rope_lanes = (lane >= ROPE_LANE0) & (lane < ROPE_LANE0 + MLA_ROPE)
    if rotate:
        cos = cos_ref[...]
        sin = sin_ref[...]
        kr = m0_ref[...] * cos + m1_ref[...] * sin
    else:
        kr = jnp.where(rope_lanes, m0_ref[...], 0.0)
    kv = _dot(_rmsnorm_rows(ckv_ref[...], kvg_ref[...]).astype(BF16), wkv_ref[...])
    for h in range(MLA_HEADS):
        k_out[h] = (kv[:, h * LANES:(h + 1) * LANES] + kr).astype(BF16)
    for p in range(MLA_PAIRS):
        v_out[p] = jnp.transpose(kv[:, hw + p * LANES:hw + (p + 1) * LANES]).astype(BF16)
    if with_q:
        qs = (MLA_NOPE + MLA_ROPE) ** -0.5 * math.log2(math.e)
        cq_tab = jnp.where(lane < MLA_NOPE, qs, cos * qs)
        sq_tab = sin * qs
        qq = _dot(_rmsnorm_rows(cq_ref[...], qg_ref[...]).astype(BF16), wq_ref[...])
        for h in range(MLA_HEADS):
            qa = qq[:, h * LANES:(h + 1) * LANES]
            qb = qq[:, hw + h * LANES:hw + (h + 1) * LANES]
            q_out[h] = (qa * cq_tab + qb * sq_tab).astype(BF16)


def _mla_proj_lat(main, misc, cos_t, sin_t, qg, kvg, wq, wkv, tm):
    bsz, t, _ = main.shape
    c2 = lambda b, i: (0, 0)
    kern = functools.partial(_mla_proj_kernel, rotate=True, with_q=True)
    return pl.pallas_call(
        kern,
        grid=(bsz, t // tm),
        in_specs=[pl.BlockSpec((None, tm, MLA_Q_RANK), lambda b, i: (b, i, COL_CQ // MLA_Q_RANK)),
                  pl.BlockSpec((None, tm, MLA_KV_RANK), lambda b, i: (b, i, COL_CKV // MLA_KV_RANK)),
                  pl.BlockSpec((None, tm, LANES), lambda b, i: (b, i, 0)),
                  pl.BlockSpec((None, tm, LANES), lambda b, i: (b, i, 1)),
                  pl.BlockSpec((tm, LANES), lambda b, i: (i, 0)),
                  pl.BlockSpec((tm, LANES), lambda b, i: (i, 0)),
                  pl.BlockSpec(qg.shape, c2), pl.BlockSpec(kvg.shape, c2),
                  pl.BlockSpec(wq.shape, c2), pl.BlockSpec(wkv.shape, c2)],
        out_specs=[pl.BlockSpec((None, MLA_HEADS, tm, LANES), lambda b, i: (b, 0, i, 0)),
                   pl.BlockSpec((None, MLA_HEADS, tm, LANES), lambda b, i: (b, 0, i, 0)),
                   pl.BlockSpec((None, MLA_PAIRS, V_TILE, tm), lambda b, i: (b, 0, 0, i))],
        out_shape=(jax.ShapeDtypeStruct((bsz, MLA_HEADS, t, LANES), BF16),
                   jax.ShapeDtypeStruct((bsz, MLA_HEADS, t, LANES), BF16),
                   jax.ShapeDtypeStruct((bsz, MLA_PAIRS, V_TILE, t), BF16)),
        compiler_params=_cparams(("parallel", "parallel")),
        name="mla_proj_lat",
    )(main, main, misc, misc, cos_t, sin_t, qg, kvg, wq, wkv)


def _mla_proj_ctx(main, misc, kvg, wkv):
    bsz, t, _ = main.shape
    c2 = lambda b: (0, 0)
    kern = functools.partial(_mla_proj_kernel, rotate=False, with_q=False)
    return pl.pallas_call(
        kern,
        grid=(bsz,),
        in_specs=[pl.BlockSpec((None, t, MLA_KV_RANK), lambda b: (b, 0, COL_CKV // MLA_KV_RANK)),
                  pl.BlockSpec((None, t, LANES), lambda b: (b, 0, 0)),
                  pl.BlockSpec(kvg.shape, c2), pl.BlockSpec(wkv.shape, c2)],
        out_specs=[pl.BlockSpec((None, MLA_HEADS, t, LANES), lambda b: (b, 0, 0, 0)),
                   pl.BlockSpec((None, MLA_PAIRS, V_TILE, t), lambda b: (b, 0, 0, 0))],
        out_shape=(jax.ShapeDtypeStruct((bsz, MLA_HEADS, t, LANES), BF16),
                   jax.ShapeDtypeStruct((bsz, MLA_PAIRS, V_TILE, t), BF16)),
        compiler_params=_cparams(("parallel",)),
        name="mla_proj_ctx",
    )(main, misc, kvg, wkv)


def _attn_t_kernel(q_ref, kl_ref, kc_ref, vtl_ref, vtc_ref, o_ref, *bufs, tq, tk):
    tb = q_ref.shape[1]
    s_len = kl_ref.shape[1]
    c_len = kc_ref.shape[1]
    n_sub = tb // tq
    chunks = [(kl_ref, vtl_ref, c0, tk, c0) for c0 in range(0, s_len, tk)] + [(kc_ref, vtc_ref, 0, c_len, s_len)]

    def pass1_chunk(buf, hh, sub, c, m):
        k_ref, _, r0, n, row = chunks[c]
        q = q_ref[hh, sub * tq:(sub + 1) * tq, :]
        s_t = _dot_nt(k_ref[hh, r0:r0 + n, :], q)
        buf[row:row + n, :] = s_t
        return jnp.maximum(m, jnp.max(s_t, axis=0, keepdims=True))

    def pass2_chunk(buf, hh, c, m, l, acc):
        _, vt_ref, r0, n, row = chunks[c]
        p_t = jnp.exp2(buf[row:row + n, :] - m)
        l = l + jnp.sum(p_t, axis=0, keepdims=True)
        acc = acc + _dot(vt_ref[hh * MLA_DV:(hh + 1) * MLA_DV, r0:r0 + n], p_t.astype(BF16))
        return l, acc

    per_head = ATTN_PIPES // 2
    per_pipe = n_sub // per_head
    pipes = [(hh, [part * per_pipe + j for j in range(per_pipe)]) for hh in range(2) for part in range(per_head)]
    outs = {}
    m_prev = [None] * len(pipes)
    for step in range(per_pipe + 1):
        m_new = [jnp.full((1, tq), -jnp.inf, F32) for _ in pipes]
        l = [jnp.zeros((1, tq), F32) for _ in pipes]
        acc = [jnp.zeros((MLA_DV, tq), F32) for _ in pipes]
        for c in range(len(chunks)):
            for p, (hh, subs) in enumerate(pipes):
                if step > 0:
                    l[p], acc[p] = pass2_chunk(bufs[2 * p + (step - 1) % 2], hh, c, m_prev[p], l[p], acc[p])
                if step < per_pipe:
                    m_new[p] = pass1_chunk(bufs[2 * p + step % 2], hh, subs[step], c, m_new[p])
        if step > 0:
            for p, (hh, subs) in enumerate(pipes):
                outs[(hh, subs[step - 1])] = acc[p] * (1.0 / l[p])
        m_prev = m_new
    for sub in range(n_sub):
        o_t = jnp.concatenate([outs[(0, sub)], outs[(1, sub)]], axis=0)
        o_ref[sub * tq:(sub + 1) * tq, :] = jnp.transpose(o_t).astype(BF16)


def _attn(q, k_lat, k_ctx, vt_lat, vt_ctx, tb, tq, tk):
    bsz, _, s_len, _ = q.shape
    c_len = k_ctx.shape[2]
    kern = functools.partial(_attn_t_kernel, tq=tq, tk=tk)
    return pl.pallas_call(
        kern,
        grid=(bsz, MLA_PAIRS, s_len // tb),
        in_specs=[pl.BlockSpec((None, 2, tb, LANES), lambda b, p, i: (b, p, i, 0)),
                  pl.BlockSpec((None, 2, s_len, LANES), lambda b, p, i: (b, p, 0, 0)),
                  pl.BlockSpec((None, 2, c_len, LANES), lambda b, p, i: (b, p, 0, 0)),
                  pl.BlockSpec((None, None, V_TILE, s_len), lambda b, p, i: (b, p, 0, 0)),
                  pl.BlockSpec((None, None, V_TILE, c_len), lambda b, p, i: (b, p, 0, 0))],
        out_specs=pl.BlockSpec((None, None, tb, LANES), lambda b, p, i: (b, p, i, 0)),
        out_shape=jax.ShapeDtypeStruct((bsz, MLA_PAIRS, s_len, LANES), BF16),
        scratch_shapes=[pltpu.VMEM((s_len + c_len, tq), F32) for _ in range(ATTN_BUFS)],
        compiler_params=_cparams(("parallel", "parallel", "arbitrary")),
        name="attn",
    )(q, k_lat, k_ctx, vt_lat, vt_ctx)


def _layernorm_rows(z, g, b):
    mu = jnp.mean(z, axis=-1, keepdims=True)
    zc = z - mu
    var = jnp.mean(zc * zc, axis=-1, keepdims=True)
    return (zc * lax.rsqrt(var + EPS)) * g + b


def _outproj_kernel(of_ref, ob_ref, r_ref, ml_ref, x_ref, mod_ref, gg_ref, wo_ref, l1g_ref, l1b_ref,
                    wr_ref, br_ref, x1_ref, h_ref, comb_ref):
    tm = x_ref.shape[0]
    o = of_ref[...] + ob_ref[...]
    r = r_ref[...].astype(F32)
    gg = gg_ref[...]
    mix = []
    for h in range(GLA_HEADS):
        sl = slice(h * GLA_DV, (h + 1) * GLA_DV)
        oh = o[:, sl]
        ms = jnp.mean(oh * oh, axis=-1, keepdims=True)
        rh = r[:, sl]
        mix.append(((oh * lax.rsqrt(ms + EPS)) * gg * (rh * _sigmoid(rh))).astype(BF16))
    mix += [ml_ref[p] for p in range(MLA_PAIRS)]
    y = _dot(jnp.concatenate(mix, axis=1), wo_ref[...])
    gate1 = mod_ref[2:3, :]
    x1 = _layernorm_rows(DEEPNORM_ALPHA * x_ref[...] + gate1 * y, l1g_ref[...], l1b_ref[...])
    x1_ref[...] = x1
    hmod = x1 * (1.0 + mod_ref[4:5, :]) + mod_ref[3:4, :]
    h_ref[...] = hmod.astype(BF16)

    h_hi = hmod.astype(BF16)
    h_lo = (hmod - h_hi.astype(F32)).astype(BF16)
    wr = wr_ref[...]
    w_hi = wr.astype(BF16)
    w_lo = (wr - w_hi.astype(F32)).astype(BF16)
    pp = _dot(jnp.concatenate([h_hi, h_lo], axis=0), jnp.concatenate([w_hi, w_lo], axis=1))
    logits = ((pp[0:tm, 0:LANES] + pp[0:tm, LANES:2 * LANES])
              + (pp[tm:2 * tm, 0:LANES] + pp[tm:2 * tm, LANES:2 * LANES]) + br_ref[...])

    lane = lax.broadcasted_iota(jnp.int32, (tm, LANES), 1).astype(F32)
    neg = -jnp.inf
    far = float(LANES)
    gl = jnp.where(lane < N_GROUPS, logits, neg)
    gmax = jnp.max(gl, axis=-1, keepdims=True)
    gsum = jnp.sum(jnp.exp(gl - gmax), axis=-1, keepdims=True)
    p_g = 1.0 / gsum
    g_top = jnp.min(jnp.where(gl == gmax, lane, far), axis=-1, keepdims=True)
    e0 = ROUTER_LANE0 + g_top * EXPERTS_PER_GROUP
    el = jnp.where((lane >= e0) & (lane < e0 + EXPERTS_PER_GROUP), logits, neg)
    e1max = jnp.max(el, axis=-1, keepdims=True)
    i1 = jnp.min(jnp.where(el == e1max, lane, far), axis=-1, keepdims=True)
    el2 = jnp.where(lane == i1, neg, el)
    e2max = jnp.max(el2, axis=-1, keepdims=True)
    i2 = jnp.min(jnp.where(el2 == e2max, lane, far), axis=-1, keepdims=True)
    t = jnp.exp(e2max - e1max)
    w1 = p_g / (1.0 + t)
    w2 = w1 * t
    comb_ref[...] = jnp.where(lane == i1, w1, jnp.where(lane == i2, w2, jnp.where(lane == g_top, 1.0, 0.0)))


def _outproj(o_f, o_b, main, mlat, x, mod, gg, wo, l1g, l1b, wr, br, tm):
    bsz, t, d = x.shape
    c2 = lambda b, i: (0, 0)
    row = lambda b, i: (b, i, 0)
    return pl.pallas_call(
        _outproj_kernel,
        grid=(bsz, t // tm),
        in_specs=[pl.BlockSpec((None, tm, GLA_WIDTH), row),
                  pl.BlockSpec((None, tm, GLA_WIDTH), row),
                  pl.BlockSpec((None, tm, GLA_WIDTH), lambda b, i: (b, i, COL_R // GLA_WIDTH)),
                  pl.BlockSpec((None, MLA_PAIRS, tm, LANES), lambda b, i: (b, 0, i, 0)),
                  pl.BlockSpec((None, tm, d), row),
                  pl.BlockSpec((None, 6, d), lambda b, i: (b, 0, 0)),
                  pl.BlockSpec(gg.shape, c2), pl.BlockSpec(wo.shape, c2),
                  pl.BlockSpec(l1g.shape, c2), pl.BlockSpec(l1b.shape, c2),
                  pl.BlockSpec(wr.shape, c2), pl.BlockSpec(br.shape, c2)],
        out_specs=[pl.BlockSpec((None, tm, d), row),
                   pl.BlockSpec((None, tm, d), row),
                   pl.BlockSpec((None, tm, LANES), row)],
        out_shape=(jax.ShapeDtypeStruct((bsz, t, d), F32),
                   jax.ShapeDtypeStruct((bsz, t, d), BF16),
                   jax.ShapeDtypeStruct((bsz, t, LANES), F32)),
        compiler_params=_cparams(("parallel", "parallel")),
        name="outproj",
    )(o_f, o_b, main, mlat, x, mod, gg, wo, l1g, l1b, wr, br)


def _moe_kernel(h_ref, comb_ref, x1_ref, mod_ref, wgu_ref, wd_ref, l2g_ref, l2b_ref, o_ref,
                hs_scr, cs_scr, acc_scr):
    tm = h_ref.shape[0]
    rb = MOE_ROW_BLOCK
    gw = EXPERTS_PER_GROUP * D_EXPERT
    comb = comb_ref[...]
    lane = lax.broadcasted_iota(jnp.int32, (tm, LANES), 1)
    onehot = jnp.where(lane < N_GROUPS, comb, 0.0)

    ri = lax.broadcasted_iota(jnp.int32, (tm, tm), 0)
    ci = lax.broadcasted_iota(jnp.int32, (tm, tm), 1)
    lower = jnp.where(ci < ri, 1.0, 0.0).astype(BF16)
    before = _dot(lower, onehot.astype(BF16))
    rank = jnp.sum(before * onehot, axis=-1, keepdims=True)
    totals = jnp.broadcast_to(jnp.sum(onehot, axis=0, keepdims=True), (8, LANES))
    offs = pltpu.roll(totals, 1, 1) + pltpu.roll(totals, 2, 1) + pltpu.roll(totals, 3, 1)
    pos = jnp.sum(onehot * offs[0:1, :], axis=-1, keepdims=True) + rank
    pt = jnp.where(ci.astype(F32) == pos, 1.0, 0.0).astype(BF16)

    hs_scr[0:tm, :] = _dot_tn(pt, h_ref[...]).astype(BF16)
    hs_scr[tm:tm + rb, :] = jnp.zeros((rb, hs_scr.shape[1]), BF16)
    c_hi = comb.astype(BF16)
    c_lo = (comb - c_hi.astype(F32)).astype(BF16)
    cs_scr[0:tm, :] = _dot_tn(pt, c_hi) + _dot_tn(pt, c_lo)
    cs_scr[tm:tm + rb, :] = jnp.zeros((rb, LANES), F32)
    acc_scr[...] = jnp.zeros_like(acc_scr)

    tot_i = totals.astype(jnp.int32)
    off_i = offs.astype(jnp.int32)
    lane_r = lax.broadcasted_iota(jnp.int32, (rb, LANES), 1)
    for g in range(N_GROUPS):
        n_g = tot_i[0, g]
        start = off_i[0, g]
        first = (start // BF16_SUBLANES) * BF16_SUBLANES
        n_blocks = jnp.where(n_g > 0, (start + n_g - first + rb - 1) // rb, 0)

        def block(k, carry, g=g, first=first):
            rows = pl.ds(pl.multiple_of(first + k * rb, BF16_SUBLANES), rb)
            hb = hs_scr[rows, :]
            cb = cs_scr[rows, :]
            parts = []
            for j in range(EXPERTS_PER_GROUP):
                e = g * EXPERTS_PER_GROUP + j
                w = jnp.sum(jnp.where(lane_r == ROUTER_LANE0 + e, cb, 0.0), axis=-1, keepdims=True)
                gu = _dot(hb, wgu_ref[e])
                gj = gu[:, 0:D_EXPERT]
                uj = gu[:, D_EXPERT:2 * D_EXPERT]
                parts.append(((gj * _sigmoid(gj)) * uj * w).astype(BF16))
            acc_scr[rows, :] += _dot(jnp.concatenate(parts, axis=1), wd_ref[g])
            return carry

        lax.fori_loop(0, n_blocks, block, 0)

    y = _dot(pt, acc_scr[0:tm, :].astype(BF16))
    gate2 = mod_ref[5:6, :]
    z = DEEPNORM_ALPHA * x1_ref[...] + gate2 * y
    o_ref[...] = _layernorm_rows(z, l2g_ref[...], l2b_ref[...])


def _moe(h, comb, x1, mod, wgu, wd, l2g, l2b, tm):
    bsz, t, d = x1.shape
    row = lambda b, i: (b, i, 0)
    c2 = lambda b, i: (0, 0)
    c3 = lambda b, i: (0, 0, 0)
    resident = pl.Buffered(1)
    return pl.pallas_call(
        _moe_kernel,
        grid=(bsz, t // tm),
        in_specs=[pl.BlockSpec((None, tm, d), row),
                  pl.BlockSpec((None, tm, LANES), row),
                  pl.BlockSpec((None, tm, d), row),
                  pl.BlockSpec((None, 6, d), lambda b, i: (b, 0, 0)),
                  pl.BlockSpec(wgu.shape, c3, pipeline_mode=resident),
                  pl.BlockSpec(wd.shape, c3, pipeline_mode=resident),
                  pl.BlockSpec(l2g.shape, c2), pl.BlockSpec(l2b.shape, c2)],
        out_specs=pl.BlockSpec((None, tm, d), row),
        out_shape=jax.ShapeDtypeStruct((bsz, t, d), F32),
        scratch_shapes=[pltpu.VMEM((tm + MOE_ROW_BLOCK, d), BF16),
                        pltpu.VMEM((tm + MOE_ROW_BLOCK, LANES), F32),
                        pltpu.VMEM((tm + MOE_ROW_BLOCK, d), F32)],
        compiler_params=pltpu.CompilerParams(dimension_semantics=("parallel", "parallel"),
                                             vmem_limit_bytes=MOE_VMEM_LIMIT),
        name="moe",
    )(h, comb, x1, mod, wgu, wd, l2g, l2b)


_ROPE_SWAP = np.concatenate([np.arange(8, 16), np.arange(0, 8), np.arange(24, 32), np.arange(16, 24)])


def _pack_w_in(w_in):
    d = w_in.shape[0]
    o_q, o_k, o_v, o_gf, o_gb, o_r, o_cq, o_ckv, o_kr = 0, 256, 512, 1024, 1040, 1056, 1568, 1824, 1952
    z = lambda n: jnp.zeros((d, n), w_in.dtype)
    kr = w_in[:, o_kr:o_kr + MLA_ROPE]
    cols = [w_in[:, o_q:o_k], w_in[:, o_k:o_v], w_in[:, o_v:o_gf], w_in[:, o_r:o_cq], w_in[:, o_cq:o_ckv],
            w_in[:, o_ckv:o_kr],
            w_in[:, o_gf:o_gb], w_in[:, o_gb:o_r], z(32), kr, z(32),
            z(64), kr[:, _ROPE_SWAP], z(32)]
    return jnp.concatenate(cols, axis=1).astype(BF16)


def _pack_gate_w(w_gk, lane0):
    out = jnp.zeros((LANES, w_gk.shape[1]), w_gk.dtype)
    return out.at[lane0:lane0 + GLA_RANK].set(w_gk).astype(BF16)


def _pack_w_uq(w_uq):
    r = w_uq.shape[0]
    w = w_uq.reshape(r, MLA_HEADS, MLA_NOPE + MLA_ROPE)
    nope, rope = w[..., :MLA_NOPE], w[..., MLA_NOPE:]
    z32 = jnp.zeros((r, MLA_HEADS, 32), w_uq.dtype)
    z64 = jnp.zeros((r, MLA_HEADS, 64), w_uq.dtype)
    a = jnp.concatenate([nope, rope, z32], axis=-1).reshape(r, MLA_HEADS * LANES)
    b = jnp.concatenate([z64, rope[..., _ROPE_SWAP], z32], axis=-1).reshape(r, MLA_HEADS * LANES)
    return jnp.concatenate([a, b], axis=1).astype(BF16)


def _pack_w_ukv(w_ukv):
    r = w_ukv.shape[0]
    w = w_ukv.reshape(r, MLA_HEADS, MLA_NOPE + MLA_DV)
    kn, v = w[..., :MLA_NOPE], w[..., MLA_NOPE:]
    k_t = jnp.concatenate([kn, jnp.zeros_like(kn)], axis=-1).reshape(r, MLA_HEADS * LANES)
    v_t = v.reshape(r, MLA_HEADS * MLA_DV)
    return jnp.concatenate([k_t, v_t], axis=1).astype(BF16)


def _pack_experts(w_gate, w_up, w_down):
    n_e, d, de = w_gate.shape
    wgu = jnp.concatenate([w_gate, w_up], axis=-1).astype(BF16)
    wd = w_down.reshape(N_GROUPS, EXPERTS_PER_GROUP * de, d).astype(BF16)
    return wgu, wd


def _pack_router(w_rg, b_rg, w_re, b_re):
    d = w_rg.shape[0]
    pad = LANES - N_GROUPS - N_EXPERTS
    w = jnp.concatenate([w_rg, w_re, jnp.zeros((d, pad), w_rg.dtype)], axis=1)
    b = jnp.concatenate([b_rg, b_re, jnp.zeros((pad,), b_rg.dtype)]).reshape(1, LANES)
    return w, b


def kernel(x, c, ctx, c_ctx, w_ada, b_ada, w_in, w_gk_f, b_gk_f, w_gk_b, b_gk_b, gla_norm_g, mla_q_norm_g, w_uq, mla_kv_norm_g, w_ukv, w_o, ln1_g, ln1_b, w_router_group, b_router_group, w_router_expert, b_router_expert, w_expert_gate, w_expert_up, w_expert_down, ln2_g, ln2_b):
    bsz, seq, d = x.shape
    l = 0

    c_all = jnp.concatenate([c, c_ctx[None, :], jnp.zeros((16 - bsz - 1, d), c.dtype)], axis=0)
    mod = _ada(c_all, w_ada[l], b_ada[l])
    mod_lat = mod[:bsz].reshape(bsz, 6, d)
    mod_ctx = mod[bsz:bsz + 1].reshape(1, 6, d)

    w_in_p = _pack_w_in(w_in[l])
    main_lat, misc_lat = _inproj(x, mod_lat, w_in_p, True, 512)
    main_ctx, misc_ctx = _inproj(ctx, mod_ctx, w_in_p, False, ctx.shape[1])

    wgf = _pack_gate_w(w_gk_f[l], 0)
    wgb = _pack_gate_w(w_gk_b[l], GLA_RANK)
    bgf = b_gk_f[l].reshape(1, GLA_QK)
    bgb = b_gk_b[l].reshape(1, GLA_QK)
    s_zero = jnp.zeros((bsz, 2, GLA_WIDTH, GLA_QK), F32)
    _, _, s_ctx = _gla(main_ctx, misc_ctx, wgf, bgf, wgb, bgb, s_zero, ctx.shape[1])
    o_f, o_b, _ = _gla(main_lat, misc_lat, wgf, bgf, wgb, bgb, s_ctx, 512)

    cos_t, sin_t = _rope_tables(seq)
    qg = mla_q_norm_g[l].reshape(1, MLA_Q_RANK)
    kvg = mla_kv_norm_g[l].reshape(1, MLA_KV_RANK)
    wq_p = _pack_w_uq(w_uq[l])
    wkv_p = _pack_w_ukv(w_ukv[l])
    q_lat, k_lat, v_lat = _mla_proj_lat(main_lat, misc_lat, cos_t, sin_t, qg, kvg, wq_p, wkv_p, 512)
    k_ctx, v_ctx = _mla_proj_ctx(main_ctx, misc_ctx, kvg, wkv_p)
    m_lat = _attn(q_lat, k_lat, k_ctx, v_lat, v_ctx, ATTN_TB, ATTN_TQ, ATTN_TK)

    gg = gla_norm_g[l].reshape(1, GLA_DV)
    w_r, b_r = _pack_router(w_router_group[l], b_router_group[l], w_router_expert[l], b_router_expert[l])
    x1, h, comb = _outproj(o_f, o_b, main_lat, m_lat, x, mod_lat, gg, w_o[l].astype(BF16),
                           ln1_g[l].reshape(1, d), ln1_b[l].reshape(1, d), w_r, b_r, 256)

    wgu, wd = _pack_experts(w_expert_gate[l], w_expert_up[l], w_expert_down[l])
    return _moe(h, comb, x1, mod_lat, wgu, wd, ln2_g[l].reshape(1, d), ln2_b[l].reshape(1, d), 512)
```

```python
import functools
import math

import jax
import jax.numpy as jnp
import numpy as np
from jax import lax
from jax.experimental import pallas as pl
from jax.experimental.pallas import tpu as pltpu

F32 = jnp.float32
BF16 = jnp.bfloat16

D_MODEL = 1024
GRID_W = 64
GLA_HEADS = 4
GLA_DK = 64
GLA_DV = 128
GLA_RANK = 16
GLA_GATE_NORM = 16.0
GLA_CHUNK = 64
GLA_QK = GLA_HEADS * GLA_DK
GLA_WIDTH = GLA_HEADS * GLA_DV
MLA_HEADS = 8
MLA_NOPE = 64
MLA_ROPE = 32
MLA_DV = 64
MLA_Q_RANK = 256
MLA_KV_RANK = 128
MLA_PAIRS = MLA_HEADS // 2
ROPE_BASE = 10000.0
N_GROUPS = 4
EXPERTS_PER_GROUP = 4
N_EXPERTS = 16
D_EXPERT = 256
DEPTH = 1
DEEPNORM_ALPHA = (2.0 * DEPTH) ** 0.25
EPS = 1e-6

LANES = 128
COL_Q, COL_K, COL_V, COL_R, COL_CQ, COL_CKV = 0, 256, 512, 1024, 1536, 1792
MAIN_W = 1920
MISC_W = 256
INPROJ_CHUNK = 768
ROPE_LANE0 = 64
V_TILE = LANES
ROUTER_LANE0 = N_GROUPS
VMEM_LIMIT = 48 * 1024 * 1024
MOE_VMEM_LIMIT = 56 * 1024 * 1024
BF16_SUBLANES = 16
MOE_ROW_BLOCK = 160
ATTN_PIPES = 2
ATTN_BUFS = 2 * ATTN_PIPES
ADA_COLS = 1536
TOKEN_TILE = 512
OUTPROJ_TILE = 256
ATTN_TB = 2048
ATTN_TQ = 256
ATTN_TK = 1024


def _cparams(sem):
    return pltpu.CompilerParams(dimension_semantics=sem, vmem_limit_bytes=VMEM_LIMIT)


def _dot(a, b):
    return jnp.dot(a, b, preferred_element_type=F32)


def _dot_nt(a, b):
    return lax.dot_general(a, b, (((1,), (1,)), ((), ())), preferred_element_type=F32)


def _dot_tn(a, b):
    return lax.dot_general(a, b, (((0,), (0,)), ((), ())), preferred_element_type=F32)


def _sigmoid(x):
    return 1.0 / (1.0 + jnp.exp(-x))


def _ada_kernel(c_ref, w_ref, b_ref, o_ref):
    a = c_ref[...]
    a = a * _sigmoid(a)
    o_ref[...] = _dot(a.astype(BF16), w_ref[...].astype(BF16)) + b_ref[...]


def _ada(c_all, w, b):
    rows, d = c_all.shape
    n = w.shape[1]
    bn = ADA_COLS
    return pl.pallas_call(
        _ada_kernel,
        grid=(n // bn,),
        in_specs=[pl.BlockSpec((rows, d), lambda j: (0, 0)),
                  pl.BlockSpec((d, bn), lambda j: (0, j)),
                  pl.BlockSpec((1, bn), lambda j: (0, j))],
        out_specs=pl.BlockSpec((rows, bn), lambda j: (0, j)),
        out_shape=jax.ShapeDtypeStruct((rows, n), F32),
        compiler_params=_cparams(("parallel",)),
        name="ada",
    )(c_all, w, b.reshape(1, n))


def _rope_tab_kernel(cos_ref, sin_ref):
    shape = cos_ref.shape
    t = lax.broadcasted_iota(jnp.int32, shape, 0) + pl.program_id(0) * shape[0]
    lane = lax.broadcasted_iota(jnp.int32, shape, 1)
    j = lane - ROPE_LANE0
    valid = (j >= 0) & (j < MLA_ROPE)
    f = (j & 7).astype(F32)
    inv_freq = jnp.exp(f * (-math.log(ROPE_BASE) / 8.0))
    pos = jnp.where(j >= 16, t & (GRID_W - 1), jnp.right_shift(t, int(math.log2(GRID_W)))).astype(F32)
    ang = pos * inv_freq
    sign = jnp.where((j & 15) < 8, -1.0, 1.0)
    cos_ref[...] = jnp.where(valid, jnp.cos(ang), 0.0)
    sin_ref[...] = jnp.where(valid, sign * jnp.sin(ang), 0.0)


def _rope_tables(seq):
    tm = TOKEN_TILE
    spec = pl.BlockSpec((tm, LANES), lambda i: (i, 0))
    return pl.pallas_call(
        _rope_tab_kernel,
        grid=(seq // tm,),
        out_specs=[spec, spec],
        out_shape=(jax.ShapeDtypeStruct((seq, LANES), F32), jax.ShapeDtypeStruct((seq, LANES), F32)),
        compiler_params=_cparams(("parallel",)),
        name="rope_tab",
    )()


def _inproj_kernel(x_ref, mod_ref, w_ref, main_ref, misc_ref):
    shift = mod_ref[0:1, :]
    scale = mod_ref[1:2, :]
    u = (x_ref[...] * (1.0 + scale) + shift).astype(BF16)
    for c0 in range(0, MAIN_W, INPROJ_CHUNK):
        c1 = min(c0 + INPROJ_CHUNK, MAIN_W + MISC_W)
        y = _dot(u, w_ref[:, c0:c1])
        if c1 <= MAIN_W:
            main_ref[:, c0:c1] = y.astype(BF16)
        else:
            main_ref[:, c0:MAIN_W] = y[:, 0:MAIN_W - c0].astype(BF16)
            misc_ref[...] = y[:, MAIN_W - c0:c1 - c0]


def _inproj(x, mod, w, per_batch, tm):
    bsz, t, d = x.shape
    mod_map = (lambda b, i: (b, 0, 0)) if per_batch else (lambda b, i: (0, 0, 0))
    return pl.pallas_call(
        _inproj_kernel,
        grid=(bsz, t // tm),
        in_specs=[pl.BlockSpec((None, tm, d), lambda b, i: (b, i, 0)),
                  pl.BlockSpec((None, 6, d), mod_map),
                  pl.BlockSpec(w.shape, lambda b, i: (0, 0))],
        out_specs=[pl.BlockSpec((None, tm, MAIN_W), lambda b, i: (b, i, 0)),
                   pl.BlockSpec((None, tm, MISC_W), lambda b, i: (b, i, 0))],
        out_shape=(jax.ShapeDtypeStruct((bsz, t, MAIN_W), BF16),
                   jax.ShapeDtypeStruct((bsz, t, MISC_W), F32)),
        compiler_params=_cparams(("parallel", "parallel")),
        name="inproj",
    )(x, mod, w)


def _gla_kernel(qkf_ref, vf_ref, mf_ref, qkb_ref, vb_ref, mb_ref, wgf_ref, bgf_ref, wgb_ref, bgb_ref,
                s0_ref, of_ref, ob_ref, sfin_ref, st_f, st_b, dsf_scr, dsb_scr, *, n_chunks):
    i = pl.program_id(1)
    nblk = pl.num_programs(1)
    C = GLA_CHUNK

    @pl.when(i == 0)
    def _():
        st_f[...] = s0_ref[0]
        st_b[...] = s0_ref[1]

    r64 = lax.broadcasted_iota(jnp.int32, (C, C), 0)
    c64 = lax.broadcasted_iota(jnp.int32, (C, C), 1)
    ra = lax.broadcasted_iota(jnp.int32, (GLA_HEADS * C, C), 0) & (C - 1)
    ca = lax.broadcasted_iota(jnp.int32, (GLA_HEADS * C, C), 1)
    lane_head = lax.broadcasted_iota(jnp.int32, (C, GLA_QK), 1) // GLA_DK
    head_masks = [jnp.where(lane_head == h, 1.0, 0.0) for h in range(GLA_HEADS)]

    def local_part(dirs):
        chunk_rows = [slice(c * C, (c + 1) * C) for c in range(n_chunks)]
        units = [(d, c) for d in range(len(dirs)) for c in range(n_chunks)]
        tri, causal, last_row, lg_hi, lg_lo = [], [], [], [], []
        for qk_ref, v_ref, m_ref, wg_ref, bg_ref, o_ref, ds_scr, forward in dirs:
            if forward:
                tri.append(jnp.where(c64 <= r64, 1.0, 0.0).astype(BF16))
                causal.append(ca <= ra)
                last_row.append(C - 1)
            else:
                tri.append(jnp.where(c64 >= r64, 1.0, 0.0).astype(BF16))
                causal.append(ca >= ra)
                last_row.append(0)
            z_all = _dot(m_ref[...].astype(BF16), wg_ref[...]) + bg_ref[...]
            lg_all = (jnp.minimum(z_all, 0.0) - jnp.log(1.0 + jnp.exp(-jnp.abs(z_all)))) * (1.0 / GLA_GATE_NORM)
            lg_hi.append(lg_all.astype(BF16))
            lg_lo.append((lg_all - lg_hi[-1].astype(F32)).astype(BF16))
        bs = [_dot(tri[d], lg_hi[d][chunk_rows[c], :]) + _dot(tri[d], lg_lo[d][chunk_rows[c], :]) for d, c in units]
        tots = [b[last_row[d]:last_row[d] + 1, :] for (d, c), b in zip(units, bs)]
        qs = [dirs[d][0][chunk_rows[c], 0:GLA_QK].astype(F32) for d, c in units]
        ks = [dirs[d][0][chunk_rows[c], GLA_QK:2 * GLA_QK].astype(F32) for d, c in units]
        vs = [dirs[d][1][chunk_rows[c], :] for d, c in units]
        q_es = [q * (jnp.exp(b) * (GLA_DK ** -0.5)) for q, b in zip(qs, bs)]
        k_es = [(k * jnp.exp(-b)).astype(BF16) for k, b in zip(ks, bs)]
        k_decs = [(k * jnp.exp(t - b)).astype(BF16) for k, b, t in zip(ks, bs, tots)]
        qms = [jnp.concatenate([(q_e * head_masks[h]).astype(BF16) for h in range(GLA_HEADS)], axis=0) for q_e in q_es]
        a_s = [jnp.where(causal[d], _dot_nt(qm, k_e), 0.0).astype(BF16) for (d, c), qm, k_e in zip(units, qms, k_es)]
        for u, (d, c) in enumerate(units):
            dirs[d][5][chunk_rows[c], :] = jnp.concatenate(
                [_dot(a_s[u][h * C:(h + 1) * C, :], vs[u][:, h * GLA_DV:(h + 1) * GLA_DV]) for h in range(GLA_HEADS)],
                axis=1)
        for u, (d, c) in enumerate(units):
            dirs[d][6][c] = _dot_tn(vs[u], k_decs[u])
        return {unit: (qm, jnp.exp(t)) for unit, qm, t in zip(units, qms, tots)}

    def state_step(c, qm, dec, o_ref, ds_scr, st):
        rows = slice(c * C, (c + 1) * C)
        st_b16 = st[...].astype(BF16)
        o_inter = jnp.concatenate(
            [_dot_nt(qm[h * C:(h + 1) * C, :], st_b16[h * GLA_DV:(h + 1) * GLA_DV, :]) for h in range(GLA_HEADS)],
            axis=1)
        o_ref[rows, :] += o_inter
        st[...] = st[...] * dec + ds_scr[c]

    loc = local_part([(qkf_ref, vf_ref, mf_ref, wgf_ref, bgf_ref, of_ref, dsf_scr, True),
                      (qkb_ref, vb_ref, mb_ref, wgb_ref, bgb_ref, ob_ref, dsb_scr, False)])
    for c in range(n_chunks):
        cb = n_chunks - 1 - c
        state_step(c, *loc[(0, c)], of_ref, dsf_scr, st_f)
        state_step(cb, *loc[(1, cb)], ob_ref, dsb_scr, st_b)

    @pl.when(i == nblk - 1)
    def _():
        sfin_ref[0] = st_f[...]
        sfin_ref[1] = st_b[...]


def _gla(main, misc, wgf, bgf, wgb, bgb, s0, tm):
    bsz, t, _ = main.shape
    nblk = t // tm
    fwd = lambda b, i: (b, i, 0)
    bwd = lambda b, i: (b, nblk - 1 - i, 0)
    const2 = lambda b, i: (0, 0)
    kern = functools.partial(_gla_kernel, n_chunks=tm // GLA_CHUNK)
    return pl.pallas_call(
        kern,
        grid=(bsz, nblk),
        in_specs=[pl.BlockSpec((None, tm, 2 * GLA_QK), fwd),
                  pl.BlockSpec((None, tm, GLA_WIDTH), lambda b, i: (b, i, 1)),
                  pl.BlockSpec((None, tm, LANES), fwd),
                  pl.BlockSpec((None, tm, 2 * GLA_QK), bwd),
                  pl.BlockSpec((None, tm, GLA_WIDTH), lambda b, i: (b, nblk - 1 - i, 1)),
                  pl.BlockSpec((None, tm, LANES), bwd),
                  pl.BlockSpec(wgf.shape, const2), pl.BlockSpec(bgf.shape, const2),
                  pl.BlockSpec(wgb.shape, const2), pl.BlockSpec(bgb.shape, const2),
                  pl.BlockSpec((None, 2, GLA_WIDTH, GLA_QK), lambda b, i: (b, 0, 0, 0))],
        out_specs=[pl.BlockSpec((None, tm, GLA_WIDTH), fwd),
                   pl.BlockSpec((None, tm, GLA_WIDTH), bwd),
                   pl.BlockSpec((None, 2, GLA_WIDTH, GLA_QK), lambda b, i: (b, 0, 0, 0))],
        out_shape=(jax.ShapeDtypeStruct((bsz, t, GLA_WIDTH), F32),
                   jax.ShapeDtypeStruct((bsz, t, GLA_WIDTH), F32),
                   jax.ShapeDtypeStruct((bsz, 2, GLA_WIDTH, GLA_QK), F32)),
        scratch_shapes=[pltpu.VMEM((GLA_WIDTH, GLA_QK), F32), pltpu.VMEM((GLA_WIDTH, GLA_QK), F32),
                        pltpu.VMEM((tm // GLA_CHUNK, GLA_WIDTH, GLA_QK), F32),
                        pltpu.VMEM((tm // GLA_CHUNK, GLA_WIDTH, GLA_QK), F32)],
        compiler_params=_cparams(("parallel", "arbitrary")),
        name="gla",
    )(main, main, misc, main, main, misc, wgf, bgf, wgb, bgb, s0)


def _rmsnorm_rows(x, g):
    xf = x.astype(F32)
    ms = jnp.mean(xf * xf, axis=-1, keepdims=True)
    return (xf * lax.rsqrt(ms + EPS)) * g


def _mla_proj_kernel(*refs, rotate, with_q):
    if with_q:
        (cq_ref, ckv_ref, m0_ref, m1_ref, cos_ref, sin_ref, qg_ref, kvg_ref, wq_ref, wkv_ref,
         q_out, k_out, v_out) = refs
    else:
        ckv_ref, m0_ref, kvg_ref, wkv_ref, k_out, v_out = refs
    hw = MLA_HEADS * LANES
    lane = lax.broadcasted_iota(jnp.int32, m0_ref.shape, 1)
    rope_lanes = (lane >= ROPE_LANE0) & (lane < ROPE_LANE0 + MLA_ROPE)
    if rotate:
        cos = cos_ref[...]
        sin = sin_ref[...]
        kr = m0_ref[...] * cos + m1_ref[...] * sin
    else:
        kr = jnp.where(rope_lanes, m0_ref[...], 0.0)
    kv = _dot(_rmsnorm_rows(ckv_ref[...], kvg_ref[...]).astype(BF16), wkv_ref[...])
    for h in range(MLA_HEADS):
        k_out[h] = (kv[:, h * LANES:(h + 1) * LANES] + kr).astype(BF16)
    for p in range(MLA_PAIRS):
        v_out[p] = jnp.transpose(kv[:, hw + p * LANES:hw + (p + 1) * LANES]).astype(BF16)
    if with_q:
        qs = (MLA_NOPE + MLA_ROPE) ** -0.5 * math.log2(math.e)
        cq_tab = jnp.where(lane < MLA_NOPE, qs, cos * qs)
        sq_tab = sin * qs
        qq = _dot(_rmsnorm_rows(cq_ref[...], qg_ref[...]).astype(BF16), wq_ref[...])
        for h in range(MLA_HEADS):
            qa = qq[:, h * LANES:(h + 1) * LANES]
            qb = qq[:, hw + h * LANES:hw + (h + 1) * LANES]
            q_out[h] = (qa * cq_tab + qb * sq_tab).astype(BF16)


def _mla_proj_lat(main, misc, cos_t, sin_t, qg, kvg, wq, wkv, tm):
    bsz, t, _ = main.shape
    c2 = lambda b, i: (0, 0)
    kern = functools.partial(_mla_proj_kernel, rotate=True, with_q=True)
    return pl.pallas_call(
        kern,
        grid=(bsz, t // tm),
        in_specs=[pl.BlockSpec((None, tm, MLA_Q_RANK), lambda b, i: (b, i, COL_CQ // MLA_Q_RANK)),
                  pl.BlockSpec((None, tm, MLA_KV_RANK), lambda b, i: (b, i, COL_CKV // MLA_KV_RANK)),
                  pl.BlockSpec((None, tm, LANES), lambda b, i: (b, i, 0)),
                  pl.BlockSpec((None, tm, LANES), lambda b, i: (b, i, 1)),
                  pl.BlockSpec((tm, LANES), lambda b, i: (i, 0)),
                  pl.BlockSpec((tm, LANES), lambda b, i: (i, 0)),
                  pl.BlockSpec(qg.shape, c2), pl.BlockSpec(kvg.shape, c2),
                  pl.BlockSpec(wq.shape, c2), pl.BlockSpec(wkv.shape, c2)],
        out_specs=[pl.BlockSpec((None, MLA_HEADS, tm, LANES), lambda b, i: (b, 0, i, 0)),
                   pl.BlockSpec((None, MLA_HEADS, tm, LANES), lambda b, i: (b, 0, i, 0)),
                   pl.BlockSpec((None, MLA_PAIRS, V_TILE, tm), lambda b, i: (b, 0, 0, i))],
        out_shape=(jax.ShapeDtypeStruct((bsz, MLA_HEADS, t, LANES), BF16),
                   jax.ShapeDtypeStruct((bsz, MLA_HEADS, t, LANES), BF16),
                   jax.ShapeDtypeStruct((bsz, MLA_PAIRS, V_TILE, t), BF16)),
        compiler_params=_cparams(("parallel", "parallel")),
        name="mla_proj_lat",
    )(main, main, misc, misc, cos_t, sin_t, qg, kvg, wq, wkv)


def _mla_proj_ctx(main, misc, kvg, wkv):
    bsz, t, _ = main.shape
    c2 = lambda b: (0, 0)
    kern = functools.partial(_mla_proj_kernel, rotate=False, with_q=False)
    return pl.pallas_call(
        kern,
        grid=(bsz,),
        in_specs=[pl.BlockSpec((None, t, MLA_KV_RANK), lambda b: (b, 0, COL_CKV // MLA_KV_RANK)),
                  pl.BlockSpec((None, t, LANES), lambda b: (b, 0, 0)),
                  pl.BlockSpec(kvg.shape, c2), pl.BlockSpec(wkv.shape, c2)],
        out_specs=[pl.BlockSpec((None, MLA_HEADS, t, LANES), lambda b: (b, 0, 0, 0)),
                   pl.BlockSpec((None, MLA_PAIRS, V_TILE, t), lambda b: (b, 0, 0, 0))],
        out_shape=(jax.ShapeDtypeStruct((bsz, MLA_HEADS, t, LANES), BF16),
                   jax.ShapeDtypeStruct((bsz, MLA_PAIRS, V_TILE, t), BF16)),
        compiler_params=_cparams(("parallel",)),
        name="mla_proj_ctx",
    )(main, misc, kvg, wkv)


def _attn_t_kernel(q_ref, kl_ref, kc_ref, vtl_ref, vtc_ref, o_ref, *bufs, tq, tk):
    tb = q_ref.shape[1]
    s_len = kl_ref.shape[1]
    c_len = kc_ref.shape[1]
    n_sub = tb // tq
    chunks = [(kl_ref, vtl_ref, c0, tk, c0) for c0 in range(0, s_len, tk)] + [(kc_ref, vtc_ref, 0, c_len, s_len)]

    def pass1_chunk(buf, hh, sub, c, m):
        k_ref, _, r0, n, row = chunks[c]
        q = q_ref[hh, sub * tq:(sub + 1) * tq, :]
        s_t = _dot_nt(k_ref[hh, r0:r0 + n, :], q)
        buf[row:row + n, :] = s_t
        return jnp.maximum(m, jnp.max(s_t, axis=0, keepdims=True))

    def pass2_chunk(buf, hh, c, m, l, acc):
        _, vt_ref, r0, n, row = chunks[c]
        p_t = jnp.exp2(buf[row:row + n, :] - m)
        l = l + jnp.sum(p_t, axis=0, keepdims=True)
        acc = acc + _dot(vt_ref[hh * MLA_DV:(hh + 1) * MLA_DV, r0:r0 + n], p_t.astype(BF16))
        return l, acc

    per_head = ATTN_PIPES // 2
    per_pipe = n_sub // per_head
    pipes = [(hh, [part * per_pipe + j for j in range(per_pipe)]) for hh in range(2) for part in range(per_head)]
    outs = {}
    m_prev = [None] * len(pipes)
    for step in range(per_pipe + 1):
        m_new = [jnp.full((1, tq), -jnp.inf, F32) for _ in pipes]
        l = [jnp.zeros((1, tq), F32) for _ in pipes]
        acc = [jnp.zeros((MLA_DV, tq), F32) for _ in pipes]
        for c in range(len(chunks)):
            for p, (hh, subs) in enumerate(pipes):
                if step > 0:
                    l[p], acc[p] = pass2_chunk(bufs[2 * p + (step - 1) % 2], hh, c, m_prev[p], l[p], acc[p])
                if step < per_pipe:
                    m_new[p] = pass1_chunk(bufs[2 * p + step % 2], hh, subs[step], c, m_new[p])
        if step > 0:
            for p, (hh, subs) in enumerate(pipes):
                outs[(hh, subs[step - 1])] = acc[p] * (1.0 / l[p])
        m_prev = m_new
    for sub in range(n_sub):
        o_t = jnp.concatenate([outs[(0, sub)], outs[(1, sub)]], axis=0)
        o_ref[sub * tq:(sub + 1) * tq, :] = jnp.transpose(o_t).astype(BF16)


def _attn(q, k_lat, k_ctx, vt_lat, vt_ctx, tb, tq, tk):
    bsz, _, s_len, _ = q.shape
    c_len = k_ctx.shape[2]
    kern = functools.partial(_attn_t_kernel, tq=tq, tk=tk)
    return pl.pallas_call(
        kern,
        grid=(bsz, MLA_PAIRS, s_len // tb),
        in_specs=[pl.BlockSpec((None, 2, tb, LANES), lambda b, p, i: (b, p, i, 0)),
                  pl.BlockSpec((None, 2, s_len, LANES), lambda b, p, i: (b, p, 0, 0)),
                  pl.BlockSpec((None, 2, c_len, LANES), lambda b, p, i: (b, p, 0, 0)),
                  pl.BlockSpec((None, None, V_TILE, s_len), lambda b, p, i: (b, p, 0, 0)),
                  pl.BlockSpec((None, None, V_TILE, c_len), lambda b, p, i: (b, p, 0, 0))],
        out_specs=pl.BlockSpec((None, None, tb, LANES), lambda b, p, i: (b, p, i, 0)),
        out_shape=jax.ShapeDtypeStruct((bsz, MLA_PAIRS, s_len, LANES), BF16),
        scratch_shapes=[pltpu.VMEM((s_len + c_len, tq), F32) for _ in range(ATTN_BUFS)],
        compiler_params=_cparams(("parallel", "parallel", "arbitrary")),
        name="attn",
    )(q, k_lat, k_ctx, vt_lat, vt_ctx)


def _layernorm_rows(z, g, b):
    mu = jnp.mean(z, axis=-1, keepdims=True)
    zc = z - mu
    var = jnp.mean(zc * zc, axis=-1, keepdims=True)
    return (zc * lax.rsqrt(var + EPS)) * g + b


def _outproj_kernel(of_ref, ob_ref, r_ref, ml_ref, x_ref, mod_ref, gg_ref, wo_ref, l1g_ref, l1b_ref,
                    wr_ref, br_ref, x1_ref, h_ref, comb_ref):
    tm = x_ref.shape[0]
    o = of_ref[...] + ob_ref[...]
    r = r_ref[...].astype(F32)
    gg = gg_ref[...]
    mix = []
    for h in range(GLA_HEADS):
        sl = slice(h * GLA_DV, (h + 1) * GLA_DV)
        oh = o[:, sl]
        ms = jnp.mean(oh * oh, axis=-1, keepdims=True)
        rh = r[:, sl]
        mix.append(((oh * lax.rsqrt(ms + EPS)) * gg * (rh * _sigmoid(rh))).astype(BF16))
    mix += [ml_ref[p] for p in range(MLA_PAIRS)]
    y = _dot(jnp.concatenate(mix, axis=1), wo_ref[...])
    gate1 = mod_ref[2:3, :]
    x1 = _layernorm_rows(DEEPNORM_ALPHA * x_ref[...] + gate1 * y, l1g_ref[...], l1b_ref[...])
    x1_ref[...] = x1
    hmod = x1 * (1.0 + mod_ref[4:5, :]) + mod_ref[3:4, :]
    h_ref[...] = hmod.astype(BF16)

    h_hi = hmod.astype(BF16)
    h_lo = (hmod - h_hi.astype(F32)).astype(BF16)
    wr = wr_ref[...]
    w_hi = wr.astype(BF16)
    w_lo = (wr - w_hi.astype(F32)).astype(BF16)
    pp = _dot(jnp.concatenate([h_hi, h_lo], axis=0), jnp.concatenate([w_hi, w_lo], axis=1))
    logits = ((pp[0:tm, 0:LANES] + pp[0:tm, LANES:2 * LANES])
              + (pp[tm:2 * tm, 0:LANES] + pp[tm:2 * tm, LANES:2 * LANES]) + br_ref[...])

    lane = lax.broadcasted_iota(jnp.int32, (tm, LANES), 1).astype(F32)
    neg = -jnp.inf
    far = float(LANES)
    gl = jnp.where(lane < N_GROUPS, logits, neg)
    gmax = jnp.max(gl, axis=-1, keepdims=True)
    gsum = jnp.sum(jnp.exp(gl - gmax), axis=-1, keepdims=True)
    p_g = 1.0 / gsum
    g_top = jnp.min(jnp.where(gl == gmax, lane, far), axis=-1, keepdims=True)
    e0 = ROUTER_LANE0 + g_top * EXPERTS_PER_GROUP
    el = jnp.where((lane >= e0) & (lane < e0 + EXPERTS_PER_GROUP), logits, neg)
    e1max = jnp.max(el, axis=-1, keepdims=True)
    i1 = jnp.min(jnp.where(el == e1max, lane, far), axis=-1, keepdims=True)
    el2 = jnp.where(lane == i1, neg, el)
    e2max = jnp.max(el2, axis=-1, keepdims=True)
    i2 = jnp.min(jnp.where(el2 == e2max, lane, far), axis=-1, keepdims=True)
    t = jnp.exp(e2max - e1max)
    w1 = p_g / (1.0 + t)
    w2 = w1 * t
    comb_ref[...] = jnp.where(lane == i1, w1, jnp.where(lane == i2, w2, jnp.where(lane == g_top, 1.0, 0.0)))


def _outproj(o_f, o_b, main, mlat, x, mod, gg, wo, l1g, l1b, wr, br, tm):
    bsz, t, d = x.shape
    c2 = lambda b, i: (0, 0)
    row = lambda b, i: (b, i, 0)
    return pl.pallas_call(
        _outproj_kernel,
        grid=(bsz, t // tm),
        in_specs=[pl.BlockSpec((None, tm, GLA_WIDTH), row),
                  pl.BlockSpec((None, tm, GLA_WIDTH), row),
                  pl.BlockSpec((None, tm, GLA_WIDTH), lambda b, i: (b, i, COL_R // GLA_WIDTH)),
                  pl.BlockSpec((None, MLA_PAIRS, tm, LANES), lambda b, i: (b, 0, i, 0)),
                  pl.BlockSpec((None, tm, d), row),
                  pl.BlockSpec((None, 6, d), lambda b, i: (b, 0, 0)),
                  pl.BlockSpec(gg.shape, c2), pl.BlockSpec(wo.shape, c2),
                  pl.BlockSpec(l1g.shape, c2), pl.BlockSpec(l1b.shape, c2),
                  pl.BlockSpec(wr.shape, c2), pl.BlockSpec(br.shape, c2)],
        out_specs=[pl.BlockSpec((None, tm, d), row),
                   pl.BlockSpec((None, tm, d), row),
                   pl.BlockSpec((None, tm, LANES), row)],
        out_shape=(jax.ShapeDtypeStruct((bsz, t, d), F32),
                   jax.ShapeDtypeStruct((bsz, t, d), BF16),
                   jax.ShapeDtypeStruct((bsz, t, LANES), F32)),
        compiler_params=_cparams(("parallel", "parallel")),
        name="outproj",
    )(o_f, o_b, main, mlat, x, mod, gg, wo, l1g, l1b, wr, br)


def _moe_kernel(h_ref, comb_ref, x1_ref, mod_ref, wgu_ref, wd_ref, l2g_ref, l2b_ref, o_ref,
                hs_scr, cs_scr, acc_scr):
    tm = h_ref.shape[0]
    rb = MOE_ROW_BLOCK
    comb = comb_ref[...]
    lane = lax.broadcasted_iota(jnp.int32, (tm, LANES), 1)
    onehot = jnp.where(lane < N_GROUPS, comb, 0.0)

    ri = lax.broadcasted_iota(jnp.int32, (tm, tm), 0)
    ci = lax.broadcasted_iota(jnp.int32, (tm, tm), 1)
    lower = jnp.where(ci < ri, 1.0, 0.0).astype(BF16)
    before = _dot(lower, onehot.astype(BF16))
    rank = jnp.sum(before * onehot, axis=-1, keepdims=True)
    totals = jnp.broadcast_to(jnp.sum(onehot, axis=0, keepdims=True), (8, LANES))
    offs = pltpu.roll(totals, 1, 1) + pltpu.roll(totals, 2, 1) + pltpu.roll(totals, 3, 1)
    pos = jnp.sum(onehot * offs[0:1, :], axis=-1, keepdims=True) + rank
    pt = jnp.where(ci.astype(F32) == pos, 1.0, 0.0).astype(BF16)

    hs_scr[0:tm, :] = _dot_tn(pt, h_ref[...]).astype(BF16)
    hs_scr[tm:tm + rb, :] = jnp.zeros((rb, hs_scr.shape[1]), BF16)
    c_hi = comb.astype(BF16)
    c_lo = (comb - c_hi.astype(F32)).astype(BF16)
    cs_scr[0:tm, :] = _dot_tn(pt, c_hi) + _dot_tn(pt, c_lo)
    cs_scr[tm:tm + rb, :] = jnp.zeros((rb, LANES), F32)
    acc_scr[...] = jnp.zeros_like(acc_scr)

    tot_i = totals.astype(jnp.int32)
    off_i = offs.astype(jnp.int32)
    lane_r = lax.broadcasted_iota(jnp.int32, (rb, LANES), 1)
    for g in range(N_GROUPS):
        n_g = tot_i[0, g]
        start = off_i[0, g]
        first = (start // BF16_SUBLANES) * BF16_SUBLANES
        n_blocks = jnp.where(n_g > 0, (start + n_g - first + rb - 1) // rb, 0)

        def block(k, carry, g=g, first=first):
            rows = pl.ds(pl.multiple_of(first + k * rb, BF16_SUBLANES), rb)
            hb = hs_scr[rows, :]
            cb = cs_scr[rows, :]
            parts = []
            for j in range(EXPERTS_PER_GROUP):
                e = g * EXPERTS_PER_GROUP + j
                w = jnp.sum(jnp.where(lane_r == ROUTER_LANE0 + e, cb, 0.0), axis=-1, keepdims=True)
                gu = _dot(hb, wgu_ref[e])
                gj = gu[:, 0:D_EXPERT]
                uj = gu[:, D_EXPERT:2 * D_EXPERT]
                parts.append(((gj * _sigmoid(gj)) * uj * w).astype(BF16))
            acc_scr[rows, :] += _dot(jnp.concatenate(parts, axis=1), wd_ref[g])
            return carry

        lax.fori_loop(0, n_blocks, block, 0)

    y = _dot(pt, acc_scr[0:tm, :].astype(BF16))
    gate2 = mod_ref[5:6, :]
    z = DEEPNORM_ALPHA * x1_ref[...] + gate2 * y
    o_ref[...] = _layernorm_rows(z, l2g_ref[...], l2b_ref[...])


def _moe(h, comb, x1, mod, wgu, wd, l2g, l2b, tm):
    bsz, t, d = x1.shape
    row = lambda b, i: (b, i, 0)
    c2 = lambda b, i: (0, 0)
    c3 = lambda b, i: (0, 0, 0)
    resident = pl.Buffered(1)
    return pl.pallas_call(
        _moe_kernel,
        grid=(bsz, t // tm),
        in_specs=[pl.BlockSpec((None, tm, d), row),
                  pl.BlockSpec((None, tm, LANES), row),
                  pl.BlockSpec((None, tm, d), row),
                  pl.BlockSpec((None, 6, d), lambda b, i: (b, 0, 0)),
                  pl.BlockSpec(wgu.shape, c3, pipeline_mode=resident),
                  pl.BlockSpec(wd.shape, c3, pipeline_mode=resident),
                  pl.BlockSpec(l2g.shape, c2), pl.BlockSpec(l2b.shape, c2)],
        out_specs=pl.BlockSpec((None, tm, d), row),
        out_shape=jax.ShapeDtypeStruct((bsz, t, d), F32),
        scratch_shapes=[pltpu.VMEM((tm + MOE_ROW_BLOCK, d), BF16),
                        pltpu.VMEM((tm + MOE_ROW_BLOCK, LANES), F32),
                        pltpu.VMEM((tm + MOE_ROW_BLOCK, d), F32)],
        compiler_params=pltpu.CompilerParams(dimension_semantics=("parallel", "parallel"),
                                             vmem_limit_bytes=MOE_VMEM_LIMIT),
        name="moe",
    )(h, comb, x1, mod, wgu, wd, l2g, l2b)


_ROPE_SWAP = np.concatenate([np.arange(8, 16), np.arange(0, 8), np.arange(24, 32), np.arange(16, 24)])


def _pack_w_in(w_in):
    d = w_in.shape[0]
    o_q, o_k, o_v, o_gf, o_gb, o_r, o_cq, o_ckv, o_kr = 0, 256, 512, 1024, 1040, 1056, 1568, 1824, 1952
    z = lambda n: jnp.zeros((d, n), w_in.dtype)
    kr = w_in[:, o_kr:o_kr + MLA_ROPE]
    cols = [w_in[:, o_q:o_k], w_in[:, o_k:o_v], w_in[:, o_v:o_gf], w_in[:, o_r:o_cq], w_in[:, o_cq:o_ckv],
            w_in[:, o_ckv:o_kr],
            w_in[:, o_gf:o_gb], w_in[:, o_gb:o_r], z(32), kr, z(32),
            z(64), kr[:, _ROPE_SWAP], z(32)]
    return jnp.concatenate(cols, axis=1).astype(BF16)


def _pack_gate_w(w_gk, lane0):
    out = jnp.zeros((LANES, w_gk.shape[1]), w_gk.dtype)
    return out.at[lane0:lane0 + GLA_RANK].set(w_gk).astype(BF16)


def _pack_w_uq(w_uq):
    r = w_uq.shape[0]
    w = w_uq.reshape(r, MLA_HEADS, MLA_NOPE + MLA_ROPE)
    nope, rope = w[..., :MLA_NOPE], w[..., MLA_NOPE:]
    z32 = jnp.zeros((r, MLA_HEADS, 32), w_uq.dtype)
    z64 = jnp.zeros((r, MLA_HEADS, 64), w_uq.dtype)
    a = jnp.concatenate([nope, rope, z32], axis=-1).reshape(r, MLA_HEADS * LANES)
    b = jnp.concatenate([z64, rope[..., _ROPE_SWAP], z32], axis=-1).reshape(r, MLA_HEADS * LANES)
    return jnp.concatenate([a, b], axis=1).astype(BF16)


def _pack_w_ukv(w_ukv):
    r = w_ukv.shape[0]
    w = w_ukv.reshape(r, MLA_HEADS, MLA_NOPE + MLA_DV)
    kn, v = w[..., :MLA_NOPE], w[..., MLA_NOPE:]
    k_t = jnp.concatenate([kn, jnp.zeros_like(kn)], axis=-1).reshape(r, MLA_HEADS * LANES)
    v_t = v.reshape(r, MLA_HEADS * MLA_DV)
    return jnp.concatenate([k_t, v_t], axis=1).astype(BF16)


def _pack_experts(w_gate, w_up, w_down):
    n_e, d, de = w_gate.shape
    wgu = jnp.concatenate([w_gate, w_up], axis=-1).astype(BF16)
    wd = w_down.reshape(N_GROUPS, EXPERTS_PER_GROUP * de, d).astype(BF16)
    return wgu, wd


def _pack_router(w_rg, b_rg, w_re, b_re):
    d = w_rg.shape[0]
    pad = LANES - N_GROUPS - N_EXPERTS
    w = jnp.concatenate([w_rg, w_re, jnp.zeros((d, pad), w_rg.dtype)], axis=1)
    b = jnp.concatenate([b_rg, b_re, jnp.zeros((pad,), b_rg.dtype)]).reshape(1, LANES)
    return w, b


def kernel(x, c, ctx, c_ctx, w_ada, b_ada, w_in, w_gk_f, b_gk_f, w_gk_b, b_gk_b, gla_norm_g, mla_q_norm_g, w_uq, mla_kv_norm_g, w_ukv, w_o, ln1_g, ln1_b, w_router_group, b_router_group, w_router_expert, b_router_expert, w_expert_gate, w_expert_up, w_expert_down, ln2_g, ln2_b):
    bsz, seq, d = x.shape
    l = 0

    pad_rows = -(bsz + 1) % BF16_SUBLANES
    c_all = jnp.concatenate([c, c_ctx[None, :], jnp.zeros((pad_rows, d), c.dtype)], axis=0)
    mod = _ada(c_all, w_ada[l], b_ada[l])
    mod_lat = mod[:bsz].reshape(bsz, 6, d)
    mod_ctx = mod[bsz:bsz + 1].reshape(1, 6, d)

    w_in_p = _pack_w_in(w_in[l])
    main_lat, misc_lat = _inproj(x, mod_lat, w_in_p, True, TOKEN_TILE)
    main_ctx, misc_ctx = _inproj(ctx, mod_ctx, w_in_p, False, ctx.shape[1])

    wgf = _pack_gate_w(w_gk_f[l], 0)
    wgb = _pack_gate_w(w_gk_b[l], GLA_RANK)
    bgf = b_gk_f[l].reshape(1, GLA_QK)
    bgb = b_gk_b[l].reshape(1, GLA_QK)
    s_zero = jnp.zeros((bsz, 2, GLA_WIDTH, GLA_QK), F32)
    _, _, s_ctx = _gla(main_ctx, misc_ctx, wgf, bgf, wgb, bgb, s_zero, ctx.shape[1])
    o_f, o_b, _ = _gla(main_lat, misc_lat, wgf, bgf, wgb, bgb, s_ctx, TOKEN_TILE)

    cos_t, sin_t = _rope_tables(seq)
    qg = mla_q_norm_g[l].reshape(1, MLA_Q_RANK)
    kvg = mla_kv_norm_g[l].reshape(1, MLA_KV_RANK)
    wq_p = _pack_w_uq(w_uq[l])
    wkv_p = _pack_w_ukv(w_ukv[l])
    q_lat, k_lat, v_lat = _mla_proj_lat(main_lat, misc_lat, cos_t, sin_t, qg, kvg, wq_p, wkv_p, TOKEN_TILE)
    k_ctx, v_ctx = _mla_proj_ctx(main_ctx, misc_ctx, kvg, wkv_p)
    m_lat = _attn(q_lat, k_lat, k_ctx, v_lat, v_ctx, ATTN_TB, ATTN_TQ, ATTN_TK)

    gg = gla_norm_g[l].reshape(1, GLA_DV)
    w_r, b_r = _pack_router(w_router_group[l], b_router_group[l], w_router_expert[l], b_router_expert[l])
    x1, h, comb = _outproj(o_f, o_b, main_lat, m_lat, x, mod_lat, gg, w_o[l].astype(BF16),
                           ln1_g[l].reshape(1, d), ln1_b[l].reshape(1, d), w_r, b_r, OUTPROJ_TILE)

    wgu, wd = _pack_experts(w_expert_gate[l], w_expert_up[l], w_expert_down[l])
    return _moe(h, comb, x1, mod_lat, wgu, wd, ln2_g[l].reshape(1, d), ln2_b[l].reshape(1, d), TOKEN_TILE)
```

```python
import functools
import math

import jax
import jax.numpy as jnp
import numpy as np
from jax import lax
from jax.experimental import pallas as pl
from jax.experimental.pallas import tpu as pltpu

F32 = jnp.float32
BF16 = jnp.bfloat16

D_MODEL = 1024
GRID_W = 64
GLA_HEADS = 4
GLA_DK = 64
GLA_DV = 128
GLA_RANK = 16
GLA_GATE_NORM = 16.0
GLA_CHUNK = 64
GLA_QK = GLA_HEADS * GLA_DK
GLA_WIDTH = GLA_HEADS * GLA_DV
MLA_HEADS = 8
MLA_NOPE = 64
MLA_ROPE = 32
MLA_DV = 64
MLA_Q_RANK = 256
MLA_KV_RANK = 128
MLA_PAIRS = MLA_HEADS // 2
ROPE_BASE = 10000.0
N_GROUPS = 4
EXPERTS_PER_GROUP = 4
N_EXPERTS = 16
D_EXPERT = 256
DEPTH = 1
DEEPNORM_ALPHA = (2.0 * DEPTH) ** 0.25
EPS = 1e-6

LANES = 128
COL_Q, COL_K, COL_V, COL_R, COL_CQ, COL_CKV = 0, 256, 512, 1024, 1536, 1792
MAIN_W = 1920
MISC_W = 256
INPROJ_CHUNK = 768
ROPE_LANE0 = 64
V_TILE = LANES
ROUTER_LANE0 = N_GROUPS
VMEM_LIMIT = 48 * 1024 * 1024
MOE_VMEM_LIMIT = 58 * 1024 * 1024
BF16_SUBLANES = 16
MOE_SORT_TILE = 512
MOE_TILE = 2 * MOE_SORT_TILE
MOE_ROW_BLOCK = 160
ATTN_PIPES = 2
ATTN_BUFS = 2 * ATTN_PIPES
ADA_COLS = 1536
TOKEN_TILE = 512
OUTPROJ_TILE = 256
ATTN_TB = 2048
ATTN_TQ = 256
ATTN_TK = 1024


def _cparams(sem):
    return pltpu.CompilerParams(dimension_semantics=sem, vmem_limit_bytes=VMEM_LIMIT)


def _dot(a, b):
    return jnp.dot(a, b, preferred_element_type=F32)


def _dot_nt(a, b):
    return lax.dot_general(a, b, (((1,), (1,)), ((), ())), preferred_element_type=F32)


def _dot_tn(a, b):
    return lax.dot_general(a, b, (((0,), (0,)), ((), ())), preferred_element_type=F32)


def _sigmoid(x):
    return 1.0 / (1.0 + jnp.exp(-x))


def _ada_kernel(c_ref, w_ref, b_ref, o_ref):
    a = c_ref[...]
    a = a * _sigmoid(a)
    o_ref[...] = _dot(a.astype(BF16), w_ref[...].astype(BF16)) + b_ref[...]


def _ada(c_all, w, b):
    rows, d = c_all.shape
    n = w.shape[1]
    bn = ADA_COLS
    return pl.pallas_call(
        _ada_kernel,
        grid=(n // bn,),
        in_specs=[pl.BlockSpec((rows, d), lambda j: (0, 0)),
                  pl.BlockSpec((d, bn), lambda j: (0, j)),
                  pl.BlockSpec((1, bn), lambda j: (0, j))],
        out_specs=pl.BlockSpec((rows, bn), lambda j: (0, j)),
        out_shape=jax.ShapeDtypeStruct((rows, n), F32),
        compiler_params=_cparams(("parallel",)),
        name="ada",
    )(c_all, w, b.reshape(1, n))


def _rope_tab_kernel(cos_ref, sin_ref):
    shape = cos_ref.shape
    t = lax.broadcasted_iota(jnp.int32, shape, 0) + pl.program_id(0) * shape[0]
    lane = lax.broadcasted_iota(jnp.int32, shape, 1)
    j = lane - ROPE_LANE0
    valid = (j >= 0) & (j < MLA_ROPE)
    f = (j & 7).astype(F32)
    inv_freq = jnp.exp(f * (-math.log(ROPE_BASE) / 8.0))
    pos = jnp.where(j >= 16, t & (GRID_W - 1), jnp.right_shift(t, int(math.log2(GRID_W)))).astype(F32)
    ang = pos * inv_freq
    sign = jnp.where((j & 15) < 8, -1.0, 1.0)
    cos_ref[...] = jnp.where(valid, jnp.cos(ang), 0.0)
    sin_ref[...] = jnp.where(valid, sign * jnp.sin(ang), 0.0)


def _rope_tables(seq):
    tm = TOKEN_TILE
    spec = pl.BlockSpec((tm, LANES), lambda i: (i, 0))
    return pl.pallas_call(
        _rope_tab_kernel,
        grid=(seq // tm,),
        out_specs=[spec, spec],
        out_shape=(jax.ShapeDtypeStruct((seq, LANES), F32), jax.ShapeDtypeStruct((seq, LANES), F32)),
        compiler_params=_cparams(("parallel",)),
        name="rope_tab",
    )()


def _inproj_kernel(x_ref, mod_ref, w_ref, main_ref, misc_ref):
    shift = mod_ref[0:1, :]
    scale = mod_ref[1:2, :]
    u = (x_ref[...] * (1.0 + scale) + shift).astype(BF16)
    for c0 in range(0, MAIN_W, INPROJ_CHUNK):
        c1 = min(c0 + INPROJ_CHUNK, MAIN_W + MISC_W)
        y = _dot(u, w_ref[:, c0:c1])
        if c1 <= MAIN_W:
            main_ref[:, c0:c1] = y.astype(BF16)
        else:
            main_ref[:, c0:MAIN_W] = y[:, 0:MAIN_W - c0].astype(BF16)
            misc_ref[...] = y[:, MAIN_W - c0:c1 - c0]


def _inproj(x, mod, w, per_batch, tm):
    bsz, t, d = x.shape
    mod_map = (lambda b, i: (b, 0, 0)) if per_batch else (lambda b, i: (0, 0, 0))
    return pl.pallas_call(
        _inproj_kernel,
        grid=(bsz, t // tm),
        in_specs=[pl.BlockSpec((None, tm, d), lambda b, i: (b, i, 0)),
                  pl.BlockSpec((None, 6, d), mod_map),
                  pl.BlockSpec(w.shape, lambda b, i: (0, 0))],
        out_specs=[pl.BlockSpec((None, tm, MAIN_W), lambda b, i: (b, i, 0)),
                   pl.BlockSpec((None, tm, MISC_W), lambda b, i: (b, i, 0))],
        out_shape=(jax.ShapeDtypeStruct((bsz, t, MAIN_W), BF16),
                   jax.ShapeDtypeStruct((bsz, t, MISC_W), F32)),
        compiler_params=_cparams(("parallel", "parallel")),
        name="inproj",
    )(x, mod, w)


def _gla_kernel(qkf_ref, vf_ref, mf_ref, qkb_ref, vb_ref, mb_ref, wgf_ref, bgf_ref, wgb_ref, bgb_ref,
                s0_ref, of_ref, ob_ref, sfin_ref, st_f, st_b, dsf_scr, dsb_scr, *, n_chunks):
    i = pl.program_id(1)
    nblk = pl.num_programs(1)
    C = GLA_CHUNK

    @pl.when(i == 0)
    def _():
        st_f[...] = s0_ref[0]
        st_b[...] = s0_ref[1]

    r64 = lax.broadcasted_iota(jnp.int32, (C, C), 0)
    c64 = lax.broadcasted_iota(jnp.int32, (C, C), 1)
    ra = lax.broadcasted_iota(jnp.int32, (GLA_HEADS * C, C), 0) & (C - 1)
    ca = lax.broadcasted_iota(jnp.int32, (GLA_HEADS * C, C), 1)
    lane_head = lax.broadcasted_iota(jnp.int32, (C, GLA_QK), 1) // GLA_DK
    head_masks = [jnp.where(lane_head == h, 1.0, 0.0) for h in range(GLA_HEADS)]

    def local_part(dirs):
        chunk_rows = [slice(c * C, (c + 1) * C) for c in range(n_chunks)]
        units = [(d, c) for d in range(len(dirs)) for c in range(n_chunks)]
        tri, causal, last_row, lg_hi, lg_lo = [], [], [], [], []
        for qk_ref, v_ref, m_ref, wg_ref, bg_ref, o_ref, ds_scr, forward in dirs:
            if forward:
                tri.append(jnp.where(c64 <= r64, 1.0, 0.0).astype(BF16))
                causal.append(ca <= ra)
                last_row.append(C - 1)
            else:
                tri.append(jnp.where(c64 >= r64, 1.0, 0.0).astype(BF16))
                causal.append(ca >= ra)
                last_row.append(0)
            z_all = _dot(m_ref[...].astype(BF16), wg_ref[...]) + bg_ref[...]
            lg_all = (jnp.minimum(z_all, 0.0) - jnp.log(1.0 + jnp.exp(-jnp.abs(z_all)))) * (1.0 / GLA_GATE_NORM)
            lg_hi.append(lg_all.astype(BF16))
            lg_lo.append((lg_all - lg_hi[-1].astype(F32)).astype(BF16))
        bs = [_dot(tri[d], lg_hi[d][chunk_rows[c], :]) + _dot(tri[d], lg_lo[d][chunk_rows[c], :]) for d, c in units]
        tots = [b[last_row[d]:last_row[d] + 1, :] for (d, c), b in zip(units, bs)]
        qs = [dirs[d][0][chunk_rows[c], 0:GLA_QK].astype(F32) for d, c in units]
        ks = [dirs[d][0][chunk_rows[c], GLA_QK:2 * GLA_QK].astype(F32) for d, c in units]
        vs = [dirs[d][1][chunk_rows[c], :] for d, c in units]
        q_es = [q * (jnp.exp(b) * (GLA_DK ** -0.5)) for q, b in zip(qs, bs)]
        k_es = [(k * jnp.exp(-b)).astype(BF16) for k, b in zip(ks, bs)]
        k_decs = [(k * jnp.exp(t - b)).astype(BF16) for k, b, t in zip(ks, bs, tots)]
        qms = [jnp.concatenate([(q_e * head_masks[h]).astype(BF16) for h in range(GLA_HEADS)], axis=0) for q_e in q_es]
        a_s = [jnp.where(causal[d], _dot_nt(qm, k_e), 0.0).astype(BF16) for (d, c), qm, k_e in zip(units, qms, k_es)]
        for u, (d, c) in enumerate(units):
            dirs[d][5][chunk_rows[c], :] = jnp.concatenate(
                [_dot(a_s[u][h * C:(h + 1) * C, :], vs[u][:, h * GLA_DV:(h + 1) * GLA_DV]) for h in range(GLA_HEADS)],
                axis=1)
        for u, (d, c) in enumerate(units):
            dirs[d][6][c] = _dot_tn(vs[u], k_decs[u])
        return {unit: (qm, jnp.exp(t)) for unit, qm, t in zip(units, qms, tots)}

    def state_step(c, qm, dec, o_ref, ds_scr, st):
        rows = slice(c * C, (c + 1) * C)
        st_b16 = st[...].astype(BF16)
        o_inter = jnp.concatenate(
            [_dot_nt(qm[h * C:(h + 1) * C, :], st_b16[h * GLA_DV:(h + 1) * GLA_DV, :]) for h in range(GLA_HEADS)],
            axis=1)
        o_ref[rows, :] += o_inter
        st[...] = st[...] * dec + ds_scr[c]

    loc = local_part([(qkf_ref, vf_ref, mf_ref, wgf_ref, bgf_ref, of_ref, dsf_scr, True),
                      (qkb_ref, vb_ref, mb_ref, wgb_ref, bgb_ref, ob_ref, dsb_scr, False)])
    for c in range(n_chunks):
        cb = n_chunks - 1 - c
        state_step(c, *loc[(0, c)], of_ref, dsf_scr, st_f)
        state_step(cb, *loc[(1, cb)], ob_ref, dsb_scr, st_b)

    @pl.when(i == nblk - 1)
    def _():
        sfin_ref[0] = st_f[...]
        sfin_ref[1] = st_b[...]


def _gla(main, misc, wgf, bgf, wgb, bgb, s0, tm):
    bsz, t, _ = main.shape
    nblk = t // tm
    fwd = lambda b, i: (b, i, 0)
    bwd = lambda b, i: (b, nblk - 1 - i, 0)
    const2 = lambda b, i: (0, 0)
    kern = functools.partial(_gla_kernel, n_chunks=tm // GLA_CHUNK)
    return pl.pallas_call(
        kern,
        grid=(bsz, nblk),
        in_specs=[pl.BlockSpec((None, tm, 2 * GLA_QK), fwd),
                  pl.BlockSpec((None, tm, GLA_WIDTH), lambda b, i: (b, i, 1)),
                  pl.BlockSpec((None, tm, LANES), fwd),
                  pl.BlockSpec((None, tm, 2 * GLA_QK), bwd),
                  pl.BlockSpec((None, tm, GLA_WIDTH), lambda b, i: (b, nblk - 1 - i, 1)),
                  pl.BlockSpec((None, tm, LANES), bwd),
                  pl.BlockSpec(wgf.shape, const2), pl.BlockSpec(bgf.shape, const2),
                  pl.BlockSpec(wgb.shape, const2), pl.BlockSpec(bgb.shape, const2),
                  pl.BlockSpec((None, 2, GLA_WIDTH, GLA_QK), lambda b, i: (b, 0, 0, 0))],
        out_specs=[pl.BlockSpec((None, tm, GLA_WIDTH), fwd),
                   pl.BlockSpec((None, tm, GLA_WIDTH), bwd),
                   pl.BlockSpec((None, 2, GLA_WIDTH, GLA_QK), lambda b, i: (b, 0, 0, 0))],
        out_shape=(jax.ShapeDtypeStruct((bsz, t, GLA_WIDTH), F32),
                   jax.ShapeDtypeStruct((bsz, t, GLA_WIDTH), F32),
                   jax.ShapeDtypeStruct((bsz, 2, GLA_WIDTH, GLA_QK), F32)),
        scratch_shapes=[pltpu.VMEM((GLA_WIDTH, GLA_QK), F32), pltpu.VMEM((GLA_WIDTH, GLA_QK), F32),
                        pltpu.VMEM((tm // GLA_CHUNK, GLA_WIDTH, GLA_QK), F32),
                        pltpu.VMEM((tm // GLA_CHUNK, GLA_WIDTH, GLA_QK), F32)],
        compiler_params=_cparams(("parallel", "arbitrary")),
        name="gla",
    )(main, main, misc, main, main, misc, wgf, bgf, wgb, bgb, s0)


def _rmsnorm_rows(x, g):
    xf = x.astype(F32)
    ms = jnp.mean(xf * xf, axis=-1, keepdims=True)
    return (xf * lax.rsqrt(ms + EPS)) * g


def _mla_proj_kernel(*refs, rotate, with_q):
    if with_q:
        (cq_ref, ckv_ref, m0_ref, m1_ref, cos_ref, sin_ref, qg_ref, kvg_ref, wq_ref, wkv_ref,
         q_out, k_out, v_out) = refs
    else:
        ckv_ref, m0_ref, kvg_ref, wkv_ref, k_out, v_out = refs
    hw = MLA_HEADS * LANES
    lane = lax.broadcasted_iota(jnp.int32, m0_ref.shape, 1)
    rope_lanes = (lane >= ROPE_LANE0) & (lane < ROPE_LANE0 + MLA_ROPE)
    if rotate:
        cos = cos_ref[...]
        sin = sin_ref[...]
        kr = m0_ref[...] * cos + m1_ref[...] * sin
    else:
        kr = jnp.where(rope_lanes, m0_ref[...], 0.0)
    kv = _dot(_rmsnorm_rows(ckv_ref[...], kvg_ref[...]).astype(BF16), wkv_ref[...])
    for h in range(MLA_HEADS):
        k_out[h] = (kv[:, h * LANES:(h + 1) * LANES] + kr).astype(BF16)
    for p in range(MLA_PAIRS):
        v_out[p] = jnp.transpose(kv[:, hw + p * LANES:hw + (p + 1) * LANES]).astype(BF16)
    if with_q:
        qs = (MLA_NOPE + MLA_ROPE) ** -0.5 * math.log2(math.e)
        cq_tab = jnp.where(lane < MLA_NOPE, qs, cos * qs)
        sq_tab = sin * qs
        qq = _dot(_rmsnorm_rows(cq_ref[...], qg_ref[...]).astype(BF16), wq_ref[...])
        for h in range(MLA_HEADS):
            qa = qq[:, h * LANES:(h + 1) * LANES]
            qb = qq[:, hw + h * LANES:hw + (h + 1) * LANES]
            q_out[h] = (qa * cq_tab + qb * sq_tab).astype(BF16)


def _mla_proj_lat(main, misc, cos_t, sin_t, qg, kvg, wq, wkv, tm):
    bsz, t, _ = main.shape
    c2 = lambda b, i: (0, 0)
    kern = functools.partial(_mla_proj_kernel, rotate=True, with_q=True)
    return pl.pallas_call(
        kern,
        grid=(bsz, t // tm),
        in_specs=[pl.BlockSpec((None, tm, MLA_Q_RANK), lambda b, i: (b, i, COL_CQ // MLA_Q_RANK)),
                  pl.BlockSpec((None, tm, MLA_KV_RANK), lambda b, i: (b, i, COL_CKV // MLA_KV_RANK)),
                  pl.BlockSpec((None, tm, LANES), lambda b, i: (b, i, 0)),
                  pl.BlockSpec((None, tm, LANES), lambda b, i: (b, i, 1)),
                  pl.BlockSpec((tm, LANES), lambda b, i: (i, 0)),
                  pl.BlockSpec((tm, LANES), lambda b, i: (i, 0)),
                  pl.BlockSpec(qg.shape, c2), pl.BlockSpec(kvg.shape, c2),
                  pl.BlockSpec(wq.shape, c2), pl.BlockSpec(wkv.shape, c2)],
        out_specs=[pl.BlockSpec((None, MLA_HEADS, tm, LANES), lambda b, i: (b, 0, i, 0)),
                   pl.BlockSpec((None, MLA_HEADS, tm, LANES), lambda b, i: (b, 0, i, 0)),
                   pl.BlockSpec((None, MLA_PAIRS, V_TILE, tm), lambda b, i: (b, 0, 0, i))],
        out_shape=(jax.ShapeDtypeStruct((bsz, MLA_HEADS, t, LANES), BF16),
                   jax.ShapeDtypeStruct((bsz, MLA_HEADS, t, LANES), BF16),
                   jax.ShapeDtypeStruct((bsz, MLA_PAIRS, V_TILE, t), BF16)),
        compiler_params=_cparams(("parallel", "parallel")),
        name="mla_proj_lat",
    )(main, main, misc, misc, cos_t, sin_t, qg, kvg, wq, wkv)


def _mla_proj_ctx(main, misc, kvg, wkv):
    bsz, t, _ = main.shape
    c2 = lambda b: (0, 0)
    kern = functools.partial(_mla_proj_kernel, rotate=False, with_q=False)
    return pl.pallas_call(
        kern,
        grid=(bsz,),
        in_specs=[pl.BlockSpec((None, t, MLA_KV_RANK), lambda b: (b, 0, COL_CKV // MLA_KV_RANK)),
                  pl.BlockSpec((None, t, LANES), lambda b: (b, 0, 0)),
                  pl.BlockSpec(kvg.shape, c2), pl.BlockSpec(wkv.shape, c2)],
        out_specs=[pl.BlockSpec((None, MLA_HEADS, t, LANES), lambda b: (b, 0, 0, 0)),
                   pl.BlockSpec((None, MLA_PAIRS, V_TILE, t), lambda b: (b, 0, 0, 0))],
        out_shape=(jax.ShapeDtypeStruct((bsz, MLA_HEADS, t, LANES), BF16),
                   jax.ShapeDtypeStruct((bsz, MLA_PAIRS, V_TILE, t), BF16)),
        compiler_params=_cparams(("parallel",)),
        name="mla_proj_ctx",
    )(main, misc, kvg, wkv)


def _attn_t_kernel(q_ref, kl_ref, kc_ref, vtl_ref, vtc_ref, o_ref, *bufs, tq, tk):
    tb = q_ref.shape[1]
    s_len = kl_ref.shape[1]
    c_len = kc_ref.shape[1]
    n_sub = tb // tq
    chunks = [(kl_ref, vtl_ref, c0, tk, c0) for c0 in range(0, s_len, tk)] + [(kc_ref, vtc_ref, 0, c_len, s_len)]

    def pass1_chunk(buf, hh, sub, c, m):
        k_ref, _, r0, n, row = chunks[c]
        q = q_ref[hh, sub * tq:(sub + 1) * tq, :]
        s_t = _dot_nt(k_ref[hh, r0:r0 + n, :], q)
        buf[row:row + n, :] = s_t
        return jnp.maximum(m, jnp.max(s_t, axis=0, keepdims=True))

    def pass2_chunk(buf, hh, c, m, l, acc):
        _, vt_ref, r0, n, row = chunks[c]
        p_t = jnp.exp2(buf[row:row + n, :] - m)
        l = l + jnp.sum(p_t, axis=0, keepdims=True)
        acc = acc + _dot(vt_ref[hh * MLA_DV:(hh + 1) * MLA_DV, r0:r0 + n], p_t.astype(BF16))
        return l, acc

    per_head = ATTN_PIPES // 2
    per_pipe = n_sub // per_head
    pipes = [(hh, [part * per_pipe + j for j in range(per_pipe)]) for hh in range(2) for part in range(per_head)]
    outs = {}
    m_prev = [None] * len(pipes)
    for step in range(per_pipe + 1):
        m_new = [jnp.full((1, tq), -jnp.inf, F32) for _ in pipes]
        l = [jnp.zeros((1, tq), F32) for _ in pipes]
        acc = [jnp.zeros((MLA_DV, tq), F32) for _ in pipes]
        for c in range(len(chunks)):
            for p, (hh, subs) in enumerate(pipes):
                if step > 0:
                    l[p], acc[p] = pass2_chunk(bufs[2 * p + (step - 1) % 2], hh, c, m_prev[p], l[p], acc[p])
                if step < per_pipe:
                    m_new[p] = pass1_chunk(bufs[2 * p + step % 2], hh, subs[step], c, m_new[p])
        if step > 0:
            for p, (hh, subs) in enumerate(pipes):
                outs[(hh, subs[step - 1])] = acc[p] * (1.0 / l[p])
        m_prev = m_new
    for sub in range(n_sub):
        o_t = jnp.concatenate([outs[(0, sub)], outs[(1, sub)]], axis=0)
        o_ref[sub * tq:(sub + 1) * tq, :] = jnp.transpose(o_t).astype(BF16)


def _attn(q, k_lat, k_ctx, vt_lat, vt_ctx, tb, tq, tk):
    bsz, _, s_len, _ = q.shape
    c_len = k_ctx.shape[2]
    kern = functools.partial(_attn_t_kernel, tq=tq, tk=tk)
    return pl.pallas_call(
        kern,
        grid=(bsz, MLA_PAIRS, s_len // tb),
        in_specs=[pl.BlockSpec((None, 2, tb, LANES), lambda b, p, i: (b, p, i, 0)),
                  pl.BlockSpec((None, 2, s_len, LANES), lambda b, p, i: (b, p, 0, 0)),
                  pl.BlockSpec((None, 2, c_len, LANES), lambda b, p, i: (b, p, 0, 0)),
                  pl.BlockSpec((None, None, V_TILE, s_len), lambda b, p, i: (b, p, 0, 0)),
                  pl.BlockSpec((None, None, V_TILE, c_len), lambda b, p, i: (b, p, 0, 0))],
        out_specs=pl.BlockSpec((None, None, tb, LANES), lambda b, p, i: (b, p, i, 0)),
        out_shape=jax.ShapeDtypeStruct((bsz, MLA_PAIRS, s_len, LANES), BF16),
        scratch_shapes=[pltpu.VMEM((s_len + c_len, tq), F32) for _ in range(ATTN_BUFS)],
        compiler_params=_cparams(("parallel", "parallel", "arbitrary")),
        name="attn",
    )(q, k_lat, k_ctx, vt_lat, vt_ctx)


def _layernorm_rows(z, g, b):
    mu = jnp.mean(z, axis=-1, keepdims=True)
    zc = z - mu
    var = jnp.mean(zc * zc, axis=-1, keepdims=True)
    return (zc * lax.rsqrt(var + EPS)) * g + b


def _outproj_kernel(of_ref, ob_ref, r_ref, ml_ref, x_ref, mod_ref, gg_ref, wo_ref, l1g_ref, l1b_ref,
                    wr_ref, br_ref, x1_ref, h_ref, comb_ref):
    tm = x_ref.shape[0]
    o = of_ref[...] + ob_ref[...]
    r = r_ref[...].astype(F32)
    gg = gg_ref[...]
    mix = []
    for h in range(GLA_HEADS):
        sl = slice(h * GLA_DV, (h + 1) * GLA_DV)
        oh = o[:, sl]
        ms = jnp.mean(oh * oh, axis=-1, keepdims=True)
        rh = r[:, sl]
        mix.append(((oh * lax.rsqrt(ms + EPS)) * gg * (rh * _sigmoid(rh))).astype(BF16))
    mix += [ml_ref[p] for p in range(MLA_PAIRS)]
    y = _dot(jnp.concatenate(mix, axis=1), wo_ref[...])
    gate1 = mod_ref[2:3, :]
    x1 = _layernorm_rows(DEEPNORM_ALPHA * x_ref[...] + gate1 * y, l1g_ref[...], l1b_ref[...])
    x1_ref[...] = x1
    hmod = x1 * (1.0 + mod_ref[4:5, :]) + mod_ref[3:4, :]
    h_ref[...] = hmod.astype(BF16)

    h_hi = hmod.astype(BF16)
    h_lo = (hmod - h_hi.astype(F32)).astype(BF16)
    wr = wr_ref[...]
    w_hi = wr.astype(BF16)
    w_lo = (wr - w_hi.astype(F32)).astype(BF16)
    pp = _dot(jnp.concatenate([h_hi, h_lo], axis=0), jnp.concatenate([w_hi, w_lo], axis=1))
    logits = ((pp[0:tm, 0:LANES] + pp[0:tm, LANES:2 * LANES])
              + (pp[tm:2 * tm, 0:LANES] + pp[tm:2 * tm, LANES:2 * LANES]) + br_ref[...])

    lane = lax.broadcasted_iota(jnp.int32, (tm, LANES), 1).astype(F32)
    neg = -jnp.inf
    far = float(LANES)
    gl = jnp.where(lane < N_GROUPS, logits, neg)
    gmax = jnp.max(gl, axis=-1, keepdims=True)
    gsum = jnp.sum(jnp.exp(gl - gmax), axis=-1, keepdims=True)
    p_g = 1.0 / gsum
    g_top = jnp.min(jnp.where(gl == gmax, lane, far), axis=-1, keepdims=True)
    e0 = ROUTER_LANE0 + g_top * EXPERTS_PER_GROUP
    el = jnp.where((lane >= e0) & (lane < e0 + EXPERTS_PER_GROUP), logits, neg)
    e1max = jnp.max(el, axis=-1, keepdims=True)
    i1 = jnp.min(jnp.where(el == e1max, lane, far), axis=-1, keepdims=True)
    el2 = jnp.where(lane == i1, neg, el)
    e2max = jnp.max(el2, axis=-1, keepdims=True)
    i2 = jnp.min(jnp.where(el2 == e2max, lane, far), axis=-1, keepdims=True)
    t = jnp.exp(e2max - e1max)
    w1 = p_g / (1.0 + t)
    w2 = w1 * t
    comb_ref[...] = jnp.where(lane == i1, w1, jnp.where(lane == i2, w2, jnp.where(lane == g_top, 1.0, 0.0)))


def _outproj(o_f, o_b, main, mlat, x, mod, gg, wo, l1g, l1b, wr, br, tm):
    bsz, t, d = x.shape
    c2 = lambda b, i: (0, 0)
    row = lambda b, i: (b, i, 0)
    return pl.pallas_call(
        _outproj_kernel,
        grid=(bsz, t // tm),
        in_specs=[pl.BlockSpec((None, tm, GLA_WIDTH), row),
                  pl.BlockSpec((None, tm, GLA_WIDTH), row),
                  pl.BlockSpec((None, tm, GLA_WIDTH), lambda b, i: (b, i, COL_R // GLA_WIDTH)),
                  pl.BlockSpec((None, MLA_PAIRS, tm, LANES), lambda b, i: (b, 0, i, 0)),
                  pl.BlockSpec((None, tm, d), row),
                  pl.BlockSpec((None, 6, d), lambda b, i: (b, 0, 0)),
                  pl.BlockSpec(gg.shape, c2), pl.BlockSpec(wo.shape, c2),
                  pl.BlockSpec(l1g.shape, c2), pl.BlockSpec(l1b.shape, c2),
                  pl.BlockSpec(wr.shape, c2), pl.BlockSpec(br.shape, c2)],
        out_specs=[pl.BlockSpec((None, tm, d), row),
                   pl.BlockSpec((None, tm, d), row),
                   pl.BlockSpec((None, tm, LANES), row)],
        out_shape=(jax.ShapeDtypeStruct((bsz, t, d), F32),
                   jax.ShapeDtypeStruct((bsz, t, d), BF16),
                   jax.ShapeDtypeStruct((bsz, t, LANES), F32)),
        compiler_params=_cparams(("parallel", "parallel")),
        name="outproj",
    )(o_f, o_b, main, mlat, x, mod, gg, wo, l1g, l1b, wr, br)


def _moe_kernel(h_ref, comb_ref, x1_ref, mod_ref, wgu_ref, wd_ref, l2g_ref, l2b_ref, o_ref,
                hs_scr, cs_scr, acc_scr):
    tm = MOE_SORT_TILE
    n_sub = h_ref.shape[0] // tm
    rb = MOE_ROW_BLOCK
    lane = lax.broadcasted_iota(jnp.int32, (tm, LANES), 1)
    ri = lax.broadcasted_iota(jnp.int32, (tm, tm), 0)
    ci = lax.broadcasted_iota(jnp.int32, (tm, tm), 1)
    lower = jnp.where(ci < ri, 1.0, 0.0).astype(BF16)
    ci_f = ci.astype(F32)

    pts, tot_i, off_i = [], [], []
    for s in range(n_sub):
        comb = comb_ref[s * tm:(s + 1) * tm, :]
        onehot = jnp.where(lane < N_GROUPS, comb, 0.0)
        before = _dot(lower, onehot.astype(BF16))
        rank = jnp.sum(before * onehot, axis=-1, keepdims=True)
        totals = jnp.broadcast_to(jnp.sum(onehot, axis=0, keepdims=True), (8, LANES))
        offs = pltpu.roll(totals, 1, 1) + pltpu.roll(totals, 2, 1) + pltpu.roll(totals, 3, 1)
        pos = jnp.sum(onehot * offs[0:1, :], axis=-1, keepdims=True) + rank
        pt = jnp.where(ci_f == pos, 1.0, 0.0).astype(BF16)
        hs_scr[s, 0:tm, :] = _dot_tn(pt, h_ref[s * tm:(s + 1) * tm, :]).astype(BF16)
        hs_scr[s, tm:tm + rb, :] = jnp.zeros((rb, hs_scr.shape[2]), BF16)
        c_hi = comb.astype(BF16)
        c_lo = (comb - c_hi.astype(F32)).astype(BF16)
        cs_scr[s, 0:tm, :] = _dot_tn(pt, c_hi) + _dot_tn(pt, c_lo)
        cs_scr[s, tm:tm + rb, :] = jnp.zeros((rb, LANES), F32)
        acc_scr[s] = jnp.zeros(acc_scr.shape[1:], BF16)
        pts.append(pt)
        tot_i.append(totals.astype(jnp.int32))
        off_i.append(offs.astype(jnp.int32))

    lane_r = lax.broadcasted_iota(jnp.int32, (n_sub * rb, LANES), 1)
    for g in range(N_GROUPS):
        firsts, counts = [], []
        for s in range(n_sub):
            n_g = tot_i[s][0, g]
            start = off_i[s][0, g]
            first = (start // BF16_SUBLANES) * BF16_SUBLANES
            firsts.append(first)
            counts.append(jnp.where(n_g > 0, (start + n_g - first + rb - 1) // rb, 0))
        n_blocks = functools.reduce(jnp.maximum, counts)

        def block(k, carry, g=g, firsts=firsts, counts=counts):
            rows = [pl.ds(pl.multiple_of(jnp.where(k < counts[s], firsts[s] + k * rb, tm), BF16_SUBLANES), rb)
                    for s in range(n_sub)]
            hb = jnp.concatenate([hs_scr[s, rows[s], :] for s in range(n_sub)], axis=0)
            cb = jnp.concatenate([cs_scr[s, rows[s], :] for s in range(n_sub)], axis=0)
            parts = []
            for j in range(EXPERTS_PER_GROUP):
                e = g * EXPERTS_PER_GROUP + j
                w = jnp.sum(jnp.where(lane_r == ROUTER_LANE0 + e, cb, 0.0), axis=-1, keepdims=True)
                gu = _dot(hb, wgu_ref[e])
                gj = gu[:, 0:D_EXPERT]
                uj = gu[:, D_EXPERT:2 * D_EXPERT]
                parts.append(((gj * _sigmoid(gj)) * uj * w).astype(BF16))
            out = _dot(jnp.concatenate(parts, axis=1), wd_ref[g])
            for s in range(n_sub):
                acc_scr[s, rows[s], :] = (acc_scr[s, rows[s], :].astype(F32) + out[s * rb:(s + 1) * rb, :]).astype(BF16)
            return carry

        lax.fori_loop(0, n_blocks, block, 0)

    gate2 = mod_ref[5:6, :]
    for s in range(n_sub):
        y = _dot(pts[s], acc_scr[s, 0:tm, :])
        z = DEEPNORM_ALPHA * x1_ref[s * tm:(s + 1) * tm, :] + gate2 * y
        o_ref[s * tm:(s + 1) * tm, :] = _layernorm_rows(z, l2g_ref[...], l2b_ref[...])


def _moe(h, comb, x1, mod, wgu, wd, l2g, l2b, tm):
    bsz, t, d = x1.shape
    row = lambda b, i: (b, i, 0)
    c2 = lambda b, i: (0, 0)
    c3 = lambda b, i: (0, 0, 0)
    resident = pl.Buffered(1)
    return pl.pallas_call(
        _moe_kernel,
        grid=(bsz, t // tm),
        in_specs=[pl.BlockSpec((None, tm, d), row),
                  pl.BlockSpec((None, tm, LANES), row),
                  pl.BlockSpec((None, tm, d), row),
                  pl.BlockSpec((None, 6, d), lambda b, i: (b, 0, 0)),
                  pl.BlockSpec(wgu.shape, c3, pipeline_mode=resident),
                  pl.BlockSpec(wd.shape, c3, pipeline_mode=resident),
                  pl.BlockSpec(l2g.shape, c2), pl.BlockSpec(l2b.shape, c2)],
        out_specs=pl.BlockSpec((None, tm, d), row),
        out_shape=jax.ShapeDtypeStruct((bsz, t, d), F32),
        scratch_shapes=[pltpu.VMEM((tm // MOE_SORT_TILE, MOE_SORT_TILE + MOE_ROW_BLOCK, d), BF16),
                        pltpu.VMEM((tm // MOE_SORT_TILE, MOE_SORT_TILE + MOE_ROW_BLOCK, LANES), F32),
                        pltpu.VMEM((tm // MOE_SORT_TILE, MOE_SORT_TILE + MOE_ROW_BLOCK, d), BF16)],
        compiler_params=pltpu.CompilerParams(dimension_semantics=("parallel", "parallel"),
                                             vmem_limit_bytes=MOE_VMEM_LIMIT),
        name="moe",
    )(h, comb, x1, mod, wgu, wd, l2g, l2b)


_ROPE_SWAP = np.concatenate([np.arange(8, 16), np.arange(0, 8), np.arange(24, 32), np.arange(16, 24)])


def _pack_w_in(w_in):
    d = w_in.shape[0]
    o_q, o_k, o_v, o_gf, o_gb, o_r, o_cq, o_ckv, o_kr = 0, 256, 512, 1024, 1040, 1056, 1568, 1824, 1952
    z = lambda n: jnp.zeros((d, n), w_in.dtype)
    kr = w_in[:, o_kr:o_kr + MLA_ROPE]
    cols = [w_in[:, o_q:o_k], w_in[:, o_k:o_v], w_in[:, o_v:o_gf], w_in[:, o_r:o_cq], w_in[:, o_cq:o_ckv],
            w_in[:, o_ckv:o_kr],
            w_in[:, o_gf:o_gb], w_in[:, o_gb:o_r], z(32), kr, z(32),
            z(64), kr[:, _ROPE_SWAP], z(32)]
    return jnp.concatenate(cols, axis=1).astype(BF16)


def _pack_gate_w(w_gk, lane0):
    out = jnp.zeros((LANES, w_gk.shape[1]), w_gk.dtype)
    return out.at[lane0:lane0 + GLA_RANK].set(w_gk).astype(BF16)


def _pack_w_uq(w_uq):
    r = w_uq.shape[0]
    w = w_uq.reshape(r, MLA_HEADS, MLA_NOPE + MLA_ROPE)
    nope, rope = w[..., :MLA_NOPE], w[..., MLA_NOPE:]
    z32 = jnp.zeros((r, MLA_HEADS, 32), w_uq.dtype)
    z64 = jnp.zeros((r, MLA_HEADS, 64), w_uq.dtype)
    a = jnp.concatenate([nope, rope, z32], axis=-1).reshape(r, MLA_HEADS * LANES)
    b = jnp.concatenate([z64, rope[..., _ROPE_SWAP], z32], axis=-1).reshape(r, MLA_HEADS * LANES)
    return jnp.concatenate([a, b], axis=1).astype(BF16)


def _pack_w_ukv(w_ukv):
    r = w_ukv.shape[0]
    w = w_ukv.reshape(r, MLA_HEADS, MLA_NOPE + MLA_DV)
    kn, v = w[..., :MLA_NOPE], w[..., MLA_NOPE:]
    k_t = jnp.concatenate([kn, jnp.zeros_like(kn)], axis=-1).reshape(r, MLA_HEADS * LANES)
    v_t = v.reshape(r, MLA_HEADS * MLA_DV)
    return jnp.concatenate([k_t, v_t], axis=1).astype(BF16)


def _pack_experts(w_gate, w_up, w_down):
    n_e, d, de = w_gate.shape
    wgu = jnp.concatenate([w_gate, w_up], axis=-1).astype(BF16)
    wd = w_down.reshape(N_GROUPS, EXPERTS_PER_GROUP * de, d).astype(BF16)
    return wgu, wd


def _pack_router(w_rg, b_rg, w_re, b_re):
    d = w_rg.shape[0]
    pad = LANES - N_GROUPS - N_EXPERTS
    w = jnp.concatenate([w_rg, w_re, jnp.zeros((d, pad), w_rg.dtype)], axis=1)
    b = jnp.concatenate([b_rg, b_re, jnp.zeros((pad,), b_rg.dtype)]).reshape(1, LANES)
    return w, b


def kernel(x, c, ctx, c_ctx, w_ada, b_ada, w_in, w_gk_f, b_gk_f, w_gk_b, b_gk_b, gla_norm_g, mla_q_norm_g, w_uq, mla_kv_norm_g, w_ukv, w_o, ln1_g, ln1_b, w_router_group, b_router_group, w_router_expert, b_router_expert, w_expert_gate, w_expert_up, w_expert_down, ln2_g, ln2_b):
    bsz, seq, d = x.shape
    l = 0

    pad_rows = -(bsz + 1) % BF16_SUBLANES
    c_all = jnp.concatenate([c, c_ctx[None, :], jnp.zeros((pad_rows, d), c.dtype)], axis=0)
    mod = _ada(c_all, w_ada[l], b_ada[l])
    mod_lat = mod[:bsz].reshape(bsz, 6, d)
    mod_ctx = mod[bsz:bsz + 1].reshape(1, 6, d)

    w_in_p = _pack_w_in(w_in[l])
    main_lat, misc_lat = _inproj(x, mod_lat, w_in_p, True, TOKEN_TILE)
    main_ctx, misc_ctx = _inproj(ctx, mod_ctx, w_in_p, False, ctx.shape[1])

    wgf = _pack_gate_w(w_gk_f[l], 0)
    wgb = _pack_gate_w(w_gk_b[l], GLA_RANK)
    bgf = b_gk_f[l].reshape(1, GLA_QK)
    bgb = b_gk_b[l].reshape(1, GLA_QK)
    s_zero = jnp.zeros((bsz, 2, GLA_WIDTH, GLA_QK), F32)
    _, _, s_ctx = _gla(main_ctx, misc_ctx, wgf, bgf, wgb, bgb, s_zero, ctx.shape[1])
    o_f, o_b, _ = _gla(main_lat, misc_lat, wgf, bgf, wgb, bgb, s_ctx, TOKEN_TILE)

    cos_t, sin_t = _rope_tables(seq)
    qg = mla_q_norm_g[l].reshape(1, MLA_Q_RANK)
    kvg = mla_kv_norm_g[l].reshape(1, MLA_KV_RANK)
    wq_p = _pack_w_uq(w_uq[l])
    wkv_p = _pack_w_ukv(w_ukv[l])
    q_lat, k_lat, v_lat = _mla_proj_lat(main_lat, misc_lat, cos_t, sin_t, qg, kvg, wq_p, wkv_p, TOKEN_TILE)
    k_ctx, v_ctx = _mla_proj_ctx(main_ctx, misc_ctx, kvg, wkv_p)
    m_lat = _attn(q_lat, k_lat, k_ctx, v_lat, v_ctx, ATTN_TB, ATTN_TQ, ATTN_TK)

    gg = gla_norm_g[l].reshape(1, GLA_DV)
    w_r, b_r = _pack_router(w_router_group[l], b_router_group[l], w_router_expert[l], b_router_expert[l])
    x1, h, comb = _outproj(o_f, o_b, main_lat, m_lat, x, mod_lat, gg, w_o[l].astype(BF16),
                           ln1_g[l].reshape(1, d), ln1_b[l].reshape(1, d), w_r, b_r, OUTPROJ_TILE)

    wgu, wd = _pack_experts(w_expert_gate[l], w_expert_up[l], w_expert_down[l])
    return _moe(h, comb, x1, mod_lat, wgu, wd, ln2_g[l].reshape(1, d), ln2_b[l].reshape(1, d), MOE_TILE)
```

```python
import functools
import math

import jax
import jax.numpy as jnp
import numpy as np
from jax import lax
from jax.experimental import pallas as pl
from jax.experimental.pallas import tpu as pltpu

F32 = jnp.float32
BF16 = jnp.bfloat16

D_MODEL = 1024
GRID_W = 64
GLA_HEADS = 4
GLA_DK = 64
GLA_DV = 128
GLA_RANK = 16
GLA_GATE_NORM = 16.0
GLA_CHUNK = 64
GLA_QK = GLA_HEADS * GLA_DK
GLA_WIDTH = GLA_HEADS * GLA_DV
MLA_HEADS = 8
MLA_NOPE = 64
MLA_ROPE = 32
MLA_DV = 64
MLA_Q_RANK = 256
MLA_KV_RANK = 128
MLA_PAIRS = MLA_HEADS // 2
ROPE_BASE = 10000.0
N_GROUPS = 4
EXPERTS_PER_GROUP = 4
N_EXPERTS = 16
D_EXPERT = 256
DEPTH = 1
DEEPNORM_ALPHA = (2.0 * DEPTH) ** 0.25
EPS = 1e-6

LANES = 128
COL_Q, COL_K, COL_V, COL_R, COL_CQ, COL_CKV = 0, 256, 512, 1024, 1536, 1792
MAIN_W = 1920
MISC_W = 256
INPROJ_CHUNK = 768
ROPE_LANE0 = 64
V_TILE = LANES
ROUTER_LANE0 = N_GROUPS
VMEM_LIMIT = 48 * 1024 * 1024
MOE_VMEM_LIMIT = 58 * 1024 * 1024
BF16_SUBLANES = 16
MOE_SORT_TILE = 512
MOE_TILE = 2 * MOE_SORT_TILE
MOE_ROW_BLOCK = 144
ATTN_PIPES = 2
ATTN_BUFS = 2 * ATTN_PIPES
ADA_COLS = 1536
TOKEN_TILE = 1024
GLA_TILE = 512
OUTPROJ_TILE = 1024
ATTN_TB = 2048
ATTN_TQ = 256
ATTN_TK = 1024


def _cparams(sem):
    return pltpu.CompilerParams(dimension_semantics=sem, vmem_limit_bytes=VMEM_LIMIT)


def _dot(a, b):
    return jnp.dot(a, b, preferred_element_type=F32)


def _dot_nt(a, b):
    return lax.dot_general(a, b, (((1,), (1,)), ((), ())), preferred_element_type=F32)


def _dot_tn(a, b):
    return lax.dot_general(a, b, (((0,), (0,)), ((), ())), preferred_element_type=F32)


def _sigmoid(x):
    return 1.0 / (1.0 + jnp.exp(-x))


def _ada_kernel(c_ref, w_ref, b_ref, o_ref):
    a = c_ref[...]
    a = a * _sigmoid(a)
    o_ref[...] = _dot(a.astype(BF16), w_ref[...].astype(BF16)) + b_ref[...]


def _ada(c_all, w, b):
    rows, d = c_all.shape
    n = w.shape[1]
    bn = ADA_COLS
    return pl.pallas_call(
        _ada_kernel,
        grid=(n // bn,),
        in_specs=[pl.BlockSpec((rows, d), lambda j: (0, 0)),
                  pl.BlockSpec((d, bn), lambda j: (0, j)),
                  pl.BlockSpec((1, bn), lambda j: (0, j))],
        out_specs=pl.BlockSpec((rows, bn), lambda j: (0, j)),
        out_shape=jax.ShapeDtypeStruct((rows, n), F32),
        compiler_params=_cparams(("parallel",)),
        name="ada",
    )(c_all, w, b.reshape(1, n))


def _rope_tab_kernel(cos_ref, sin_ref):
    shape = cos_ref.shape
    t = lax.broadcasted_iota(jnp.int32, shape, 0) + pl.program_id(0) * shape[0]
    lane = lax.broadcasted_iota(jnp.int32, shape, 1)
    j = lane - ROPE_LANE0
    valid = (j >= 0) & (j < MLA_ROPE)
    f = (j & 7).astype(F32)
    inv_freq = jnp.exp(f * (-math.log(ROPE_BASE) / 8.0))
    pos = jnp.where(j >= 16, t & (GRID_W - 1), jnp.right_shift(t, int(math.log2(GRID_W)))).astype(F32)
    ang = pos * inv_freq
    sign = jnp.where((j & 15) < 8, -1.0, 1.0)
    cos_ref[...] = jnp.where(valid, jnp.cos(ang), 0.0)
    sin_ref[...] = jnp.where(valid, sign * jnp.sin(ang), 0.0)


def _rope_tables(seq):
    tm = TOKEN_TILE
    spec = pl.BlockSpec((tm, LANES), lambda i: (i, 0))
    return pl.pallas_call(
        _rope_tab_kernel,
        grid=(seq // tm,),
        out_specs=[spec, spec],
        out_shape=(jax.ShapeDtypeStruct((seq, LANES), F32), jax.ShapeDtypeStruct((seq, LANES), F32)),
        compiler_params=_cparams(("parallel",)),
        name="rope_tab",
    )()


def _inproj_kernel(x_ref, mod_ref, w_ref, main_ref, misc_ref):
    shift = mod_ref[0:1, :]
    scale = mod_ref[1:2, :]
    u = (x_ref[...] * (1.0 + scale) + shift).astype(BF16)
    for c0 in range(0, MAIN_W, INPROJ_CHUNK):
        c1 = min(c0 + INPROJ_CHUNK, MAIN_W + MISC_W)
        y = _dot(u, w_ref[:, c0:c1])
        if c1 <= MAIN_W:
            main_ref[:, c0:c1] = y.astype(BF16)
        else:
            main_ref[:, c0:MAIN_W] = y[:, 0:MAIN_W - c0].astype(BF16)
            misc_ref[...] = y[:, MAIN_W - c0:c1 - c0]


def _inproj(x, mod, w, per_batch, tm):
    bsz, t, d = x.shape
    mod_map = (lambda b, i: (b, 0, 0)) if per_batch else (lambda b, i: (0, 0, 0))
    return pl.pallas_call(
        _inproj_kernel,
        grid=(bsz, t // tm),
        in_specs=[pl.BlockSpec((None, tm, d), lambda b, i: (b, i, 0)),
                  pl.BlockSpec((None, 6, d), mod_map),
                  pl.BlockSpec(w.shape, lambda b, i: (0, 0))],
        out_specs=[pl.BlockSpec((None, tm, MAIN_W), lambda b, i: (b, i, 0)),
                   pl.BlockSpec((None, tm, MISC_W), lambda b, i: (b, i, 0))],
        out_shape=(jax.ShapeDtypeStruct((bsz, t, MAIN_W), BF16),
                   jax.ShapeDtypeStruct((bsz, t, MISC_W), F32)),
        compiler_params=_cparams(("parallel", "parallel")),
        name="inproj",
    )(x, mod, w)


def _gla_kernel(qkf_ref, vf_ref, mf_ref, qkb_ref, vb_ref, mb_ref, wgf_ref, bgf_ref, wgb_ref, bgb_ref,
                s0_ref, of_ref, ob_ref, sfin_ref, st_f, st_b, dsf_scr, dsb_scr, *, n_chunks):
    i = pl.program_id(1)
    nblk = pl.num_programs(1)
    C = GLA_CHUNK

    @pl.when(i == 0)
    def _():
        st_f[...] = s0_ref[0]
        st_b[...] = s0_ref[1]

    r64 = lax.broadcasted_iota(jnp.int32, (C, C), 0)
    c64 = lax.broadcasted_iota(jnp.int32, (C, C), 1)
    ra = lax.broadcasted_iota(jnp.int32, (GLA_HEADS * C, C), 0) & (C - 1)
    ca = lax.broadcasted_iota(jnp.int32, (GLA_HEADS * C, C), 1)
    lane_head = lax.broadcasted_iota(jnp.int32, (C, GLA_QK), 1) // GLA_DK
    head_masks = [jnp.where(lane_head == h, 1.0, 0.0) for h in range(GLA_HEADS)]

    def local_part(dirs):
        chunk_rows = [slice(c * C, (c + 1) * C) for c in range(n_chunks)]
        units = [(d, c) for d in range(len(dirs)) for c in range(n_chunks)]
        tri, causal, last_row, lg_hi, lg_lo = [], [], [], [], []
        for qk_ref, v_ref, m_ref, wg_ref, bg_ref, o_ref, ds_scr, forward in dirs:
            if forward:
                tri.append(jnp.where(c64 <= r64, 1.0, 0.0).astype(BF16))
                causal.append(ca <= ra)
                last_row.append(C - 1)
            else:
                tri.append(jnp.where(c64 >= r64, 1.0, 0.0).astype(BF16))
                causal.append(ca >= ra)
                last_row.append(0)
            z_all = _dot(m_ref[...].astype(BF16), wg_ref[...]) + bg_ref[...]
            lg_all = (jnp.minimum(z_all, 0.0) - jnp.log(1.0 + jnp.exp(-jnp.abs(z_all)))) * (1.0 / GLA_GATE_NORM)
            lg_hi.append(lg_all.astype(BF16))
            lg_lo.append((lg_all - lg_hi[-1].astype(F32)).astype(BF16))
        bs = [_dot(tri[d], lg_hi[d][chunk_rows[c], :]) + _dot(tri[d], lg_lo[d][chunk_rows[c], :]) for d, c in units]
        tots = [b[last_row[d]:last_row[d] + 1, :] for (d, c), b in zip(units, bs)]
        qs = [dirs[d][0][chunk_rows[c], 0:GLA_QK].astype(F32) for d, c in units]
        ks = [dirs[d][0][chunk_rows[c], GLA_QK:2 * GLA_QK].astype(F32) for d, c in units]
        vs = [dirs[d][1][chunk_rows[c], :] for d, c in units]
        q_es = [q * (jnp.exp(b) * (GLA_DK ** -0.5)) for q, b in zip(qs, bs)]
        k_es = [(k * jnp.exp(-b)).astype(BF16) for k, b in zip(ks, bs)]
        k_decs = [(k * jnp.exp(t - b)).astype(BF16) for k, b, t in zip(ks, bs, tots)]
        qms = [jnp.concatenate([(q_e * head_masks[h]).astype(BF16) for h in range(GLA_HEADS)], axis=0) for q_e in q_es]
        a_s = [jnp.where(causal[d], _dot_nt(qm, k_e), 0.0).astype(BF16) for (d, c), qm, k_e in zip(units, qms, k_es)]
        for u, (d, c) in enumerate(units):
            dirs[d][5][chunk_rows[c], :] = jnp.concatenate(
                [_dot(a_s[u][h * C:(h + 1) * C, :], vs[u][:, h * GLA_DV:(h + 1) * GLA_DV]) for h in range(GLA_HEADS)],
                axis=1)
        for u, (d, c) in enumerate(units):
            dirs[d][6][c] = _dot_tn(vs[u], k_decs[u])
        return {unit: (qm, jnp.exp(t)) for unit, qm, t in zip(units, qms, tots)}

    def state_step(c, qm, dec, o_ref, ds_scr, st):
        rows = slice(c * C, (c + 1) * C)
        st_b16 = st[...].astype(BF16)
        o_inter = jnp.concatenate(
            [_dot_nt(qm[h * C:(h + 1) * C, :], st_b16[h * GLA_DV:(h + 1) * GLA_DV, :]) for h in range(GLA_HEADS)],
            axis=1)
        o_ref[rows, :] += o_inter
        st[...] = st[...] * dec + ds_scr[c]

    loc = local_part([(qkf_ref, vf_ref, mf_ref, wgf_ref, bgf_ref, of_ref, dsf_scr, True),
                      (qkb_ref, vb_ref, mb_ref, wgb_ref, bgb_ref, ob_ref, dsb_scr, False)])
    for c in range(n_chunks):
        cb = n_chunks - 1 - c
        state_step(c, *loc[(0, c)], of_ref, dsf_scr, st_f)
        state_step(cb, *loc[(1, cb)], ob_ref, dsb_scr, st_b)

    @pl.when(i == nblk - 1)
    def _():
        sfin_ref[0] = st_f[...]
        sfin_ref[1] = st_b[...]


def _gla(main, misc, wgf, bgf, wgb, bgb, s0, tm):
    bsz, t, _ = main.shape
    nblk = t // tm
    fwd = lambda b, i: (b, i, 0)
    bwd = lambda b, i: (b, nblk - 1 - i, 0)
    const2 = lambda b, i: (0, 0)
    kern = functools.partial(_gla_kernel, n_chunks=tm // GLA_CHUNK)
    return pl.pallas_call(
        kern,
        grid=(bsz, nblk),
        in_specs=[pl.BlockSpec((None, tm, 2 * GLA_QK), fwd),
                  pl.BlockSpec((None, tm, GLA_WIDTH), lambda b, i: (b, i, 1)),
                  pl.BlockSpec((None, tm, LANES), fwd),
                  pl.BlockSpec((None, tm, 2 * GLA_QK), bwd),
                  pl.BlockSpec((None, tm, GLA_WIDTH), lambda b, i: (b, nblk - 1 - i, 1)),
                  pl.BlockSpec((None, tm, LANES), bwd),
                  pl.BlockSpec(wgf.shape, const2), pl.BlockSpec(bgf.shape, const2),
                  pl.BlockSpec(wgb.shape, const2), pl.BlockSpec(bgb.shape, const2),
                  pl.BlockSpec((None, 2, GLA_WIDTH, GLA_QK), lambda b, i: (b, 0, 0, 0))],
        out_specs=[pl.BlockSpec((None, tm, GLA_WIDTH), fwd),
                   pl.BlockSpec((None, tm, GLA_WIDTH), bwd),
                   pl.BlockSpec((None, 2, GLA_WIDTH, GLA_QK), lambda b, i: (b, 0, 0, 0))],
        out_shape=(jax.ShapeDtypeStruct((bsz, t, GLA_WIDTH), F32),
                   jax.ShapeDtypeStruct((bsz, t, GLA_WIDTH), F32),
                   jax.ShapeDtypeStruct((bsz, 2, GLA_WIDTH, GLA_QK), F32)),
        scratch_shapes=[pltpu.VMEM((GLA_WIDTH, GLA_QK), F32), pltpu.VMEM((GLA_WIDTH, GLA_QK), F32),
                        pltpu.VMEM((tm // GLA_CHUNK, GLA_WIDTH, GLA_QK), F32),
                        pltpu.VMEM((tm // GLA_CHUNK, GLA_WIDTH, GLA_QK), F32)],
        compiler_params=_cparams(("parallel", "arbitrary")),
        name="gla",
    )(main, main, misc, main, main, misc, wgf, bgf, wgb, bgb, s0)


def _rmsnorm_rows(x, g):
    xf = x.astype(F32)
    ms = jnp.mean(xf * xf, axis=-1, keepdims=True)
    return (xf * lax.rsqrt(ms + EPS)) * g


def _mla_proj_kernel(*refs, rotate, with_q):
    if with_q:
        (cq_ref, ckv_ref, m0_ref, m1_ref, cos_ref, sin_ref, qg_ref, kvg_ref, wq_ref, wkv_ref,
         q_out, k_out, v_out) = refs
    else:
        ckv_ref, m0_ref, kvg_ref, wkv_ref, k_out, v_out = refs
    hw = MLA_HEADS * LANES
    lane = lax.broadcasted_iota(jnp.int32, m0_ref.shape, 1)
    rope_lanes = (lane >= ROPE_LANE0) & (lane < ROPE_LANE0 + MLA_ROPE)
    if rotate:
        cos = cos_ref[...]
        sin = sin_ref[...]
        kr = m0_ref[...] * cos + m1_ref[...] * sin
    else:
        kr = jnp.where(rope_lanes, m0_ref[...], 0.0)
    kv = _dot(_rmsnorm_rows(ckv_ref[...], kvg_ref[...]).astype(BF16), wkv_ref[...])
    for h in range(MLA_HEADS):
        k_out[h] = (kv[:, h * LANES:(h + 1) * LANES] + kr).astype(BF16)
    for p in range(MLA_PAIRS):
        v_out[p] = jnp.transpose(kv[:, hw + p * LANES:hw + (p + 1) * LANES]).astype(BF16)
    if with_q:
        qs = (MLA_NOPE + MLA_ROPE) ** -0.5 * math.log2(math.e)
        cq_tab = jnp.where(lane < MLA_NOPE, qs, cos * qs)
        sq_tab = sin * qs
        qq = _dot(_rmsnorm_rows(cq_ref[...], qg_ref[...]).astype(BF16), wq_ref[...])
        for h in range(MLA_HEADS):
            qa = qq[:, h * LANES:(h + 1) * LANES]
            qb = qq[:, hw + h * LANES:hw + (h + 1) * LANES]
            q_out[h] = (qa * cq_tab + qb * sq_tab).astype(BF16)


def _mla_proj_lat(main, misc, cos_t, sin_t, qg, kvg, wq, wkv, tm):
    bsz, t, _ = main.shape
    c2 = lambda b, i: (0, 0)
    kern = functools.partial(_mla_proj_kernel, rotate=True, with_q=True)
    return pl.pallas_call(
        kern,
        grid=(bsz, t // tm),
        in_specs=[pl.BlockSpec((None, tm, MLA_Q_RANK), lambda b, i: (b, i, COL_CQ // MLA_Q_RANK)),
                  pl.BlockSpec((None, tm, MLA_KV_RANK), lambda b, i: (b, i, COL_CKV // MLA_KV_RANK)),
                  pl.BlockSpec((None, tm, LANES), lambda b, i: (b, i, 0)),
                  pl.BlockSpec((None, tm, LANES), lambda b, i: (b, i, 1)),
                  pl.BlockSpec((tm, LANES), lambda b, i: (i, 0)),
                  pl.BlockSpec((tm, LANES), lambda b, i: (i, 0)),
                  pl.BlockSpec(qg.shape, c2), pl.BlockSpec(kvg.shape, c2),
                  pl.BlockSpec(wq.shape, c2), pl.BlockSpec(wkv.shape, c2)],
        out_specs=[pl.BlockSpec((None, MLA_HEADS, tm, LANES), lambda b, i: (b, 0, i, 0)),
                   pl.BlockSpec((None, MLA_HEADS, tm, LANES), lambda b, i: (b, 0, i, 0)),
                   pl.BlockSpec((None, MLA_PAIRS, V_TILE, tm), lambda b, i: (b, 0, 0, i))],
        out_shape=(jax.ShapeDtypeStruct((bsz, MLA_HEADS, t, LANES), BF16),
                   jax.ShapeDtypeStruct((bsz, MLA_HEADS, t, LANES), BF16),
                   jax.ShapeDtypeStruct((bsz, MLA_PAIRS, V_TILE, t), BF16)),
        compiler_params=_cparams(("parallel", "parallel")),
        name="mla_proj_lat",
    )(main, main, misc, misc, cos_t, sin_t, qg, kvg, wq, wkv)


def _mla_proj_ctx(main, misc, kvg, wkv):
    bsz, t, _ = main.shape
    c2 = lambda b: (0, 0)
    kern = functools.partial(_mla_proj_kernel, rotate=False, with_q=False)
    return pl.pallas_call(
        kern,
        grid=(bsz,),
        in_specs=[pl.BlockSpec((None, t, MLA_KV_RANK), lambda b: (b, 0, COL_CKV // MLA_KV_RANK)),
                  pl.BlockSpec((None, t, LANES), lambda b: (b, 0, 0)),
                  pl.BlockSpec(kvg.shape, c2), pl.BlockSpec(wkv.shape, c2)],
        out_specs=[pl.BlockSpec((None, MLA_HEADS, t, LANES), lambda b: (b, 0, 0, 0)),
                   pl.BlockSpec((None, MLA_PAIRS, V_TILE, t), lambda b: (b, 0, 0, 0))],
        out_shape=(jax.ShapeDtypeStruct((bsz, MLA_HEADS, t, LANES), BF16),
                   jax.ShapeDtypeStruct((bsz, MLA_PAIRS, V_TILE, t), BF16)),
        compiler_params=_cparams(("parallel",)),
        name="mla_proj_ctx",
    )(main, misc, kvg, wkv)


def _attn_t_kernel(q_ref, kl_ref, kc_ref, vtl_ref, vtc_ref, o_ref, *bufs, tq, tk):
    tb = q_ref.shape[1]
    s_len = kl_ref.shape[1]
    c_len = kc_ref.shape[1]
    n_sub = tb // tq
    chunks = [(kl_ref, vtl_ref, c0, tk, c0) for c0 in range(0, s_len, tk)] + [(kc_ref, vtc_ref, 0, c_len, s_len)]

    def pass1_chunk(buf, hh, sub, c, m):
        k_ref, _, r0, n, row = chunks[c]
        q = q_ref[hh, sub * tq:(sub + 1) * tq, :]
        s_t = _dot_nt(k_ref[hh, r0:r0 + n, :], q)
        buf[row:row + n, :] = s_t
        return jnp.maximum(m, jnp.max(s_t, axis=0, keepdims=True))

    def pass2_chunk(buf, hh, c, m, l, acc):
        _, vt_ref, r0, n, row = chunks[c]
        p_t = jnp.exp2(buf[row:row + n, :] - m)
        l = l + jnp.sum(p_t, axis=0, keepdims=True)
        acc = acc + _dot(vt_ref[hh * MLA_DV:(hh + 1) * MLA_DV, r0:r0 + n], p_t.astype(BF16))
        return l, acc

    per_head = ATTN_PIPES // 2
    per_pipe = n_sub // per_head
    pipes = [(hh, [part * per_pipe + j for j in range(per_pipe)]) for hh in range(2) for part in range(per_head)]
    outs = {}
    m_prev = [None] * len(pipes)
    for step in range(per_pipe + 1):
        m_new = [jnp.full((1, tq), -jnp.inf, F32) for _ in pipes]
        l = [jnp.zeros((1, tq), F32) for _ in pipes]
        acc = [jnp.zeros((MLA_DV, tq), F32) for _ in pipes]
        for c in range(len(chunks)):
            for p, (hh, subs) in enumerate(pipes):
                if step > 0:
                    l[p], acc[p] = pass2_chunk(bufs[2 * p + (step - 1) % 2], hh, c, m_prev[p], l[p], acc[p])
                if step < per_pipe:
                    m_new[p] = pass1_chunk(bufs[2 * p + step % 2], hh, subs[step], c, m_new[p])
        if step > 0:
            for p, (hh, subs) in enumerate(pipes):
                outs[(hh, subs[step - 1])] = acc[p] * (1.0 / l[p])
        m_prev = m_new
    for sub in range(n_sub):
        o_t = jnp.concatenate([outs[(0, sub)], outs[(1, sub)]], axis=0)
        o_ref[sub * tq:(sub + 1) * tq, :] = jnp.transpose(o_t).astype(BF16)


def _attn(q, k_lat, k_ctx, vt_lat, vt_ctx, tb, tq, tk):
    bsz, _, s_len, _ = q.shape
    c_len = k_ctx.shape[2]
    kern = functools.partial(_attn_t_kernel, tq=tq, tk=tk)
    return pl.pallas_call(
        kern,
        grid=(bsz, MLA_PAIRS, s_len // tb),
        in_specs=[pl.BlockSpec((None, 2, tb, LANES), lambda b, p, i: (b, p, i, 0)),
                  pl.BlockSpec((None, 2, s_len, LANES), lambda b, p, i: (b, p, 0, 0)),
                  pl.BlockSpec((None, 2, c_len, LANES), lambda b, p, i: (b, p, 0, 0)),
                  pl.BlockSpec((None, None, V_TILE, s_len), lambda b, p, i: (b, p, 0, 0)),
                  pl.BlockSpec((None, None, V_TILE, c_len), lambda b, p, i: (b, p, 0, 0))],
        out_specs=pl.BlockSpec((None, None, tb, LANES), lambda b, p, i: (b, p, i, 0)),
        out_shape=jax.ShapeDtypeStruct((bsz, MLA_PAIRS, s_len, LANES), BF16),
        scratch_shapes=[pltpu.VMEM((s_len + c_len, tq), F32) for _ in range(ATTN_BUFS)],
        compiler_params=_cparams(("parallel", "parallel", "arbitrary")),
        name="attn",
    )(q, k_lat, k_ctx, vt_lat, vt_ctx)


def _layernorm_rows(z, g, b):
    mu = jnp.mean(z, axis=-1, keepdims=True)
    zc = z - mu
    var = jnp.mean(zc * zc, axis=-1, keepdims=True)
    return (zc * lax.rsqrt(var + EPS)) * g + b


def _outproj_kernel(of_ref, ob_ref, r_ref, ml_ref, x_ref, mod_ref, gg_ref, wo_ref, l1g_ref, l1b_ref,
                    wr_ref, br_ref, x1_ref, h_ref, comb_ref):
    tm = x_ref.shape[0]
    o = of_ref[...] + ob_ref[...]
    r = r_ref[...].astype(F32)
    gg = gg_ref[...]
    mix = []
    for h in range(GLA_HEADS):
        sl = slice(h * GLA_DV, (h + 1) * GLA_DV)
        oh = o[:, sl]
        ms = jnp.mean(oh * oh, axis=-1, keepdims=True)
        rh = r[:, sl]
        mix.append(((oh * lax.rsqrt(ms + EPS)) * gg * (rh * _sigmoid(rh))).astype(BF16))
    mix += [ml_ref[p] for p in range(MLA_PAIRS)]
    y = _dot(jnp.concatenate(mix, axis=1), wo_ref[...])
    gate1 = mod_ref[2:3, :]
    x1 = _layernorm_rows(DEEPNORM_ALPHA * x_ref[...] + gate1 * y, l1g_ref[...], l1b_ref[...])
    x1_ref[...] = x1
    hmod = x1 * (1.0 + mod_ref[4:5, :]) + mod_ref[3:4, :]
    h_ref[...] = hmod.astype(BF16)

    h_hi = hmod.astype(BF16)
    h_lo = (hmod - h_hi.astype(F32)).astype(BF16)
    wr = wr_ref[...]
    w_hi = wr.astype(BF16)
    w_lo = (wr - w_hi.astype(F32)).astype(BF16)
    pp = _dot(jnp.concatenate([h_hi, h_lo], axis=0), jnp.concatenate([w_hi, w_lo], axis=1))
    logits = ((pp[0:tm, 0:LANES] + pp[0:tm, LANES:2 * LANES])
              + (pp[tm:2 * tm, 0:LANES] + pp[tm:2 * tm, LANES:2 * LANES]) + br_ref[...])

    lane = lax.broadcasted_iota(jnp.int32, (tm, LANES), 1).astype(F32)
    neg = -jnp.inf
    far = float(LANES)
    gl = jnp.where(lane < N_GROUPS, logits, neg)
    gmax = jnp.max(gl, axis=-1, keepdims=True)
    gsum = jnp.sum(jnp.exp(gl - gmax), axis=-1, keepdims=True)
    p_g = 1.0 / gsum
    g_top = jnp.min(jnp.where(gl == gmax, lane, far), axis=-1, keepdims=True)
    e0 = ROUTER_LANE0 + g_top * EXPERTS_PER_GROUP
    el = jnp.where((lane >= e0) & (lane < e0 + EXPERTS_PER_GROUP), logits, neg)
    e1max = jnp.max(el, axis=-1, keepdims=True)
    i1 = jnp.min(jnp.where(el == e1max, lane, far), axis=-1, keepdims=True)
    el2 = jnp.where(lane == i1, neg, el)
    e2max = jnp.max(el2, axis=-1, keepdims=True)
    i2 = jnp.min(jnp.where(el2 == e2max, lane, far), axis=-1, keepdims=True)
    t = jnp.exp(e2max - e1max)
    w1 = p_g / (1.0 + t)
    w2 = w1 * t
    comb_ref[...] = jnp.where(lane == i1, w1, jnp.where(lane == i2, w2, jnp.where(lane == g_top, 1.0, 0.0)))


def _outproj(o_f, o_b, main, mlat, x, mod, gg, wo, l1g, l1b, wr, br, tm):
    bsz, t, d = x.shape
    c2 = lambda b, i: (0, 0)
    row = lambda b, i: (b, i, 0)
    return pl.pallas_call(
        _outproj_kernel,
        grid=(bsz, t // tm),
        in_specs=[pl.BlockSpec((None, tm, GLA_WIDTH), row),
                  pl.BlockSpec((None, tm, GLA_WIDTH), row),
                  pl.BlockSpec((None, tm, GLA_WIDTH), lambda b, i: (b, i, COL_R // GLA_WIDTH)),
                  pl.BlockSpec((None, MLA_PAIRS, tm, LANES), lambda b, i: (b, 0, i, 0)),
                  pl.BlockSpec((None, tm, d), row),
                  pl.BlockSpec((None, 6, d), lambda b, i: (b, 0, 0)),
                  pl.BlockSpec(gg.shape, c2), pl.BlockSpec(wo.shape, c2),
                  pl.BlockSpec(l1g.shape, c2), pl.BlockSpec(l1b.shape, c2),
                  pl.BlockSpec(wr.shape, c2), pl.BlockSpec(br.shape, c2)],
        out_specs=[pl.BlockSpec((None, tm, d), row),
                   pl.BlockSpec((None, tm, d), row),
                   pl.BlockSpec((None, tm, LANES), row)],
        out_shape=(jax.ShapeDtypeStruct((bsz, t, d), F32),
                   jax.ShapeDtypeStruct((bsz, t, d), BF16),
                   jax.ShapeDtypeStruct((bsz, t, LANES), F32)),
        compiler_params=_cparams(("parallel", "parallel")),
        name="outproj",
    )(o_f, o_b, main, mlat, x, mod, gg, wo, l1g, l1b, wr, br)


def _moe_kernel(h_ref, comb_ref, x1_ref, mod_ref, wgu_ref, wd_ref, l2g_ref, l2b_ref, o_ref,
                hs_scr, cs_scr, acc_scr):
    tm = MOE_SORT_TILE
    n_sub = h_ref.shape[0] // tm
    rb = MOE_ROW_BLOCK
    lane = lax.broadcasted_iota(jnp.int32, (tm, LANES), 1)
    ri = lax.broadcasted_iota(jnp.int32, (tm, tm), 0)
    ci = lax.broadcasted_iota(jnp.int32, (tm, tm), 1)
    lower = jnp.where(ci < ri, 1.0, 0.0).astype(BF16)
    ci_f = ci.astype(F32)

    pts, tot_i, off_i = [], [], []
    for s in range(n_sub):
        comb = comb_ref[s * tm:(s + 1) * tm, :]
        onehot = jnp.where(lane < N_GROUPS, comb, 0.0)
        before = _dot(lower, onehot.astype(BF16))
        rank = jnp.sum(before * onehot, axis=-1, keepdims=True)
        totals = jnp.broadcast_to(jnp.sum(onehot, axis=0, keepdims=True), (8, LANES))
        offs = pltpu.roll(totals, 1, 1) + pltpu.roll(totals, 2, 1) + pltpu.roll(totals, 3, 1)
        pos = jnp.sum(onehot * offs[0:1, :], axis=-1, keepdims=True) + rank
        pt = jnp.where(ci_f == pos, 1.0, 0.0).astype(BF16)
        hs_scr[s, 0:tm, :] = _dot_tn(pt, h_ref[s * tm:(s + 1) * tm, :]).astype(BF16)
        hs_scr[s, tm:tm + rb, :] = jnp.zeros((rb, hs_scr.shape[2]), BF16)
        c_hi = comb.astype(BF16)
        c_lo = (comb - c_hi.astype(F32)).astype(BF16)
        cs_scr[s, 0:tm, :] = _dot_tn(pt, c_hi) + _dot_tn(pt, c_lo)
        cs_scr[s, tm:tm + rb, :] = jnp.zeros((rb, LANES), F32)
        acc_scr[s] = jnp.zeros(acc_scr.shape[1:], BF16)
        pts.append(pt)
        tot_i.append(totals.astype(jnp.int32))
        off_i.append(offs.astype(jnp.int32))

    lane_r = lax.broadcasted_iota(jnp.int32, (n_sub * rb, LANES), 1)
    for g in range(N_GROUPS):
        firsts, counts = [], []
        for s in range(n_sub):
            n_g = tot_i[s][0, g]
            start = off_i[s][0, g]
            first = (start // BF16_SUBLANES) * BF16_SUBLANES
            firsts.append(first)
            counts.append(jnp.where(n_g > 0, (start + n_g - first + rb - 1) // rb, 0))
        n_blocks = functools.reduce(jnp.maximum, counts)

        def block(k, carry, g=g, firsts=firsts, counts=counts):
            rows = [pl.ds(pl.multiple_of(jnp.where(k < counts[s], firsts[s] + k * rb, tm), BF16_SUBLANES), rb)
                    for s in range(n_sub)]
            hb = jnp.concatenate([hs_scr[s, rows[s], :] for s in range(n_sub)], axis=0)
            cb = jnp.concatenate([cs_scr[s, rows[s], :] for s in range(n_sub)], axis=0)
            parts = []
            for j in range(EXPERTS_PER_GROUP):
                e = g * EXPERTS_PER_GROUP + j
                w = jnp.sum(jnp.where(lane_r == ROUTER_LANE0 + e, cb, 0.0), axis=-1, keepdims=True)
                gu = _dot(hb, wgu_ref[e])
                gj = gu[:, 0:D_EXPERT]
                uj = gu[:, D_EXPERT:2 * D_EXPERT]
                parts.append(((gj * _sigmoid(gj)) * uj * w).astype(BF16))
            out = _dot(jnp.concatenate(parts, axis=1), wd_ref[g])
            for s in range(n_sub):
                acc_scr[s, rows[s], :] = (acc_scr[s, rows[s], :].astype(F32) + out[s * rb:(s + 1) * rb, :]).astype(BF16)
            return carry

        lax.fori_loop(0, n_blocks, block, 0)

    gate2 = mod_ref[5:6, :]
    for s in range(n_sub):
        y = _dot(pts[s], acc_scr[s, 0:tm, :])
        z = DEEPNORM_ALPHA * x1_ref[s * tm:(s + 1) * tm, :] + gate2 * y
        o_ref[s * tm:(s + 1) * tm, :] = _layernorm_rows(z, l2g_ref[...], l2b_ref[...])


def _moe(h, comb, x1, mod, wgu, wd, l2g, l2b, tm):
    bsz, t, d = x1.shape
    row = lambda b, i: (b, i, 0)
    c2 = lambda b, i: (0, 0)
    c3 = lambda b, i: (0, 0, 0)
    resident = pl.Buffered(1)
    return pl.pallas_call(
        _moe_kernel,
        grid=(bsz, t // tm),
        in_specs=[pl.BlockSpec((None, tm, d), row),
                  pl.BlockSpec((None, tm, LANES), row),
                  pl.BlockSpec((None, tm, d), row),
                  pl.BlockSpec((None, 6, d), lambda b, i: (b, 0, 0)),
                  pl.BlockSpec(wgu.shape, c3, pipeline_mode=resident),
                  pl.BlockSpec(wd.shape, c3, pipeline_mode=resident),
                  pl.BlockSpec(l2g.shape, c2), pl.BlockSpec(l2b.shape, c2)],
        out_specs=pl.BlockSpec((None, tm, d), row),
        out_shape=jax.ShapeDtypeStruct((bsz, t, d), F32),
        scratch_shapes=[pltpu.VMEM((tm // MOE_SORT_TILE, MOE_SORT_TILE + MOE_ROW_BLOCK, d), BF16),
                        pltpu.VMEM((tm // MOE_SORT_TILE, MOE_SORT_TILE + MOE_ROW_BLOCK, LANES), F32),
                        pltpu.VMEM((tm // MOE_SORT_TILE, MOE_SORT_TILE + MOE_ROW_BLOCK, d), BF16)],
        compiler_params=pltpu.CompilerParams(dimension_semantics=("parallel", "parallel"),
                                             vmem_limit_bytes=MOE_VMEM_LIMIT),
        name="moe",
    )(h, comb, x1, mod, wgu, wd, l2g, l2b)


_ROPE_SWAP = np.concatenate([np.arange(8, 16), np.arange(0, 8), np.arange(24, 32), np.arange(16, 24)])


def _pack_w_in(w_in):
    d = w_in.shape[0]
    o_q, o_k, o_v, o_gf, o_gb, o_r, o_cq, o_ckv, o_kr = 0, 256, 512, 1024, 1040, 1056, 1568, 1824, 1952
    z = lambda n: jnp.zeros((d, n), w_in.dtype)
    kr = w_in[:, o_kr:o_kr + MLA_ROPE]
    cols = [w_in[:, o_q:o_k], w_in[:, o_k:o_v], w_in[:, o_v:o_gf], w_in[:, o_r:o_cq], w_in[:, o_cq:o_ckv],
            w_in[:, o_ckv:o_kr],
            w_in[:, o_gf:o_gb], w_in[:, o_gb:o_r], z(32), kr, z(32),
            z(64), kr[:, _ROPE_SWAP], z(32)]
    return jnp.concatenate(cols, axis=1).astype(BF16)


def _pack_gate_w(w_gk, lane0):
    out = jnp.zeros((LANES, w_gk.shape[1]), w_gk.dtype)
    return out.at[lane0:lane0 + GLA_RANK].set(w_gk).astype(BF16)


def _pack_w_uq(w_uq):
    r = w_uq.shape[0]
    w = w_uq.reshape(r, MLA_HEADS, MLA_NOPE + MLA_ROPE)
    nope, rope = w[..., :MLA_NOPE], w[..., MLA_NOPE:]
    z32 = jnp.zeros((r, MLA_HEADS, 32), w_uq.dtype)
    z64 = jnp.zeros((r, MLA_HEADS, 64), w_uq.dtype)
    a = jnp.concatenate([nope, rope, z32], axis=-1).reshape(r, MLA_HEADS * LANES)
    b = jnp.concatenate([z64, rope[..., _ROPE_SWAP], z32], axis=-1).reshape(r, MLA_HEADS * LANES)
    return jnp.concatenate([a, b], axis=1).astype(BF16)


def _pack_w_ukv(w_ukv):
    r = w_ukv.shape[0]
    w = w_ukv.reshape(r, MLA_HEADS, MLA_NOPE + MLA_DV)
    kn, v = w[..., :MLA_NOPE], w[..., MLA_NOPE:]
    k_t = jnp.concatenate([kn, jnp.zeros_like(kn)], axis=-1).reshape(r, MLA_HEADS * LANES)
    v_t = v.reshape(r, MLA_HEADS * MLA_DV)
    return jnp.concatenate([k_t, v_t], axis=1).astype(BF16)


def _pack_experts(w_gate, w_up, w_down):
    n_e, d, de = w_gate.shape
    wgu = jnp.concatenate([w_gate, w_up], axis=-1).astype(BF16)
    wd = w_down.reshape(N_GROUPS, EXPERTS_PER_GROUP * de, d).astype(BF16)
    return wgu, wd


def _pack_router(w_rg, b_rg, w_re, b_re):
    d = w_rg.shape[0]
    pad = LANES - N_GROUPS - N_EXPERTS
    w = jnp.concatenate([w_rg, w_re, jnp.zeros((d, pad), w_rg.dtype)], axis=1)
    b = jnp.concatenate([b_rg, b_re, jnp.zeros((pad,), b_rg.dtype)]).reshape(1, LANES)
    return w, b


def kernel(x, c, ctx, c_ctx, w_ada, b_ada, w_in, w_gk_f, b_gk_f, w_gk_b, b_gk_b, gla_norm_g, mla_q_norm_g, w_uq, mla_kv_norm_g, w_ukv, w_o, ln1_g, ln1_b, w_router_group, b_router_group, w_router_expert, b_router_expert, w_expert_gate, w_expert_up, w_expert_down, ln2_g, ln2_b):
    bsz, seq, d = x.shape
    l = 0

    pad_rows = -(bsz + 1) % BF16_SUBLANES
    c_all = jnp.concatenate([c, c_ctx[None, :], jnp.zeros((pad_rows, d), c.dtype)], axis=0)
    mod = _ada(c_all, w_ada[l], b_ada[l])
    mod_lat = mod[:bsz].reshape(bsz, 6, d)
    mod_ctx = mod[bsz:bsz + 1].reshape(1, 6, d)

    w_in_p = _pack_w_in(w_in[l])
    main_lat, misc_lat = _inproj(x, mod_lat, w_in_p, True, TOKEN_TILE)
    main_ctx, misc_ctx = _inproj(ctx, mod_ctx, w_in_p, False, ctx.shape[1])

    wgf = _pack_gate_w(w_gk_f[l], 0)
    wgb = _pack_gate_w(w_gk_b[l], GLA_RANK)
    bgf = b_gk_f[l].reshape(1, GLA_QK)
    bgb = b_gk_b[l].reshape(1, GLA_QK)
    s_zero = jnp.zeros((bsz, 2, GLA_WIDTH, GLA_QK), F32)
    _, _, s_ctx = _gla(main_ctx, misc_ctx, wgf, bgf, wgb, bgb, s_zero, ctx.shape[1])
    o_f, o_b, _ = _gla(main_lat, misc_lat, wgf, bgf, wgb, bgb, s_ctx, GLA_TILE)

    cos_t, sin_t = _rope_tables(seq)
    qg = mla_q_norm_g[l].reshape(1, MLA_Q_RANK)
    kvg = mla_kv_norm_g[l].reshape(1, MLA_KV_RANK)
    wq_p = _pack_w_uq(w_uq[l])
    wkv_p = _pack_w_ukv(w_ukv[l])
    q_lat, k_lat, v_lat = _mla_proj_lat(main_lat, misc_lat, cos_t, sin_t, qg, kvg, wq_p, wkv_p, TOKEN_TILE)
    k_ctx, v_ctx = _mla_proj_ctx(main_ctx, misc_ctx, kvg, wkv_p)
    m_lat = _attn(q_lat, k_lat, k_ctx, v_lat, v_ctx, ATTN_TB, ATTN_TQ, ATTN_TK)

    gg = gla_norm_g[l].reshape(1, GLA_DV)
    w_r, b_r = _pack_router(w_router_group[l], b_router_group[l], w_router_expert[l], b_router_expert[l])
    x1, h, comb = _outproj(o_f, o_b, main_lat, m_lat, x, mod_lat, gg, w_o[l].astype(BF16),
                           ln1_g[l].reshape(1, d), ln1_b[l].reshape(1, d), w_r, b_r, OUTPROJ_TILE)

    wgu, wd = _pack_experts(w_expert_gate[l], w_expert_up[l], w_expert_down[l])
    return _moe(h, comb, x1, mod_lat, wgu, wd, ln2_g[l].reshape(1, d), ln2_b[l].reshape(1, d), MOE_TILE)
```

```python
import functools
import math

import jax
import jax.numpy as jnp
import numpy as np
from jax import lax
from jax.experimental import pallas as pl
from jax.experimental.pallas import tpu as pltpu

F32 = jnp.float32
BF16 = jnp.bfloat16

D_MODEL = 1024
GRID_W = 64
GLA_HEADS = 4
GLA_DK = 64
GLA_DV = 128
GLA_RANK = 16
GLA_GATE_NORM = 16.0
GLA_CHUNK = 64
GLA_QK = GLA_HEADS * GLA_DK
GLA_WIDTH = GLA_HEADS * GLA_DV
MLA_HEADS = 8
MLA_NOPE = 64
MLA_ROPE = 32
MLA_DV = 64
MLA_Q_RANK = 256
MLA_KV_RANK = 128
MLA_PAIRS = MLA_HEADS // 2
ROPE_BASE = 10000.0
N_GROUPS = 4
EXPERTS_PER_GROUP = 4
N_EXPERTS = 16
D_EXPERT = 256
DEPTH = 1
DEEPNORM_ALPHA = (2.0 * DEPTH) ** 0.25
EPS = 1e-6

LANES = 128
COL_Q, COL_K, COL_V, COL_R, COL_CQ, COL_CKV = 0, 256, 512, 1024, 1536, 1792
MAIN_W = 1920
MISC_W = 256
INPROJ_CHUNK = 768
ROPE_LANE0 = 64
V_TILE = LANES
ROUTER_LANE0 = N_GROUPS
VMEM_LIMIT = 48 * 1024 * 1024
MOE_VMEM_LIMIT = 58 * 1024 * 1024
BF16_SUBLANES = 16
MOE_SORT_TILE = 512
MOE_TILE = 2 * MOE_SORT_TILE
MOE_ROW_BLOCK = 144
ATTN_PIPES = 2
ATTN_BUFS = 2 * ATTN_PIPES
ADA_COLS = 1536
TOKEN_TILE = 1024
GLA_TILE = 512
OUTPROJ_TILE = 1024
ATTN_TB = 2048
ATTN_TQ = 256
ATTN_TK = 1024


def _cparams(sem):
    return pltpu.CompilerParams(dimension_semantics=sem, vmem_limit_bytes=VMEM_LIMIT)


def _dot(a, b):
    return jnp.dot(a, b, preferred_element_type=F32)


def _dot_nt(a, b):
    return lax.dot_general(a, b, (((1,), (1,)), ((), ())), preferred_element_type=F32)


def _dot_tn(a, b):
    return lax.dot_general(a, b, (((0,), (0,)), ((), ())), preferred_element_type=F32)


def _sigmoid(x):
    return 1.0 / (1.0 + jnp.exp(-x))


def _ada_kernel(c_ref, w_ref, b_ref, o_ref):
    a = c_ref[...]
    a = a * _sigmoid(a)
    o_ref[...] = _dot(a.astype(BF16), w_ref[...].astype(BF16)) + b_ref[...]


def _ada(c_all, w, b):
    rows, d = c_all.shape
    n = w.shape[1]
    bn = ADA_COLS
    return pl.pallas_call(
        _ada_kernel,
        grid=(n // bn,),
        in_specs=[pl.BlockSpec((rows, d), lambda j: (0, 0)),
                  pl.BlockSpec((d, bn), lambda j: (0, j)),
                  pl.BlockSpec((1, bn), lambda j: (0, j))],
        out_specs=pl.BlockSpec((rows, bn), lambda j: (0, j)),
        out_shape=jax.ShapeDtypeStruct((rows, n), F32),
        compiler_params=_cparams(("parallel",)),
        name="ada",
    )(c_all, w, b.reshape(1, n))


def _rope_tab_kernel(cos_ref, sin_ref):
    shape = cos_ref.shape
    t = lax.broadcasted_iota(jnp.int32, shape, 0) + pl.program_id(0) * shape[0]
    lane = lax.broadcasted_iota(jnp.int32, shape, 1)
    j = lane - ROPE_LANE0
    valid = (j >= 0) & (j < MLA_ROPE)
    f = (j & 7).astype(F32)
    inv_freq = jnp.exp(f * (-math.log(ROPE_BASE) / 8.0))
    pos = jnp.where(j >= 16, t & (GRID_W - 1), jnp.right_shift(t, int(math.log2(GRID_W)))).astype(F32)
    ang = pos * inv_freq
    sign = jnp.where((j & 15) < 8, -1.0, 1.0)
    cos_ref[...] = jnp.where(valid, jnp.cos(ang), 0.0)
    sin_ref[...] = jnp.where(valid, sign * jnp.sin(ang), 0.0)


def _rope_tables(seq):
    tm = TOKEN_TILE
    spec = pl.BlockSpec((tm, LANES), lambda i: (i, 0))
    return pl.pallas_call(
        _rope_tab_kernel,
        grid=(seq // tm,),
        out_specs=[spec, spec],
        out_shape=(jax.ShapeDtypeStruct((seq, LANES), F32), jax.ShapeDtypeStruct((seq, LANES), F32)),
        compiler_params=_cparams(("parallel",)),
        name="rope_tab",
    )()


def _inproj_kernel(x_ref, mod_ref, w_ref, main_ref, misc_ref):
    shift = mod_ref[0:1, :]
    scale = mod_ref[1:2, :]
    u = (x_ref[...] * (1.0 + scale) + shift).astype(BF16)
    for c0 in range(0, MAIN_W, INPROJ_CHUNK):
        c1 = min(c0 + INPROJ_CHUNK, MAIN_W + MISC_W)
        y = _dot(u, w_ref[:, c0:c1])
        if c1 <= MAIN_W:
            main_ref[:, c0:c1] = y.astype(BF16)
        else:
            main_ref[:, c0:MAIN_W] = y[:, 0:MAIN_W - c0].astype(BF16)
            misc_ref[...] = y[:, MAIN_W - c0:c1 - c0]


def _inproj(x, mod, w, per_batch, tm):
    bsz, t, d = x.shape
    mod_map = (lambda b, i: (b, 0, 0)) if per_batch else (lambda b, i: (0, 0, 0))
    return pl.pallas_call(
        _inproj_kernel,
        grid=(bsz, t // tm),
        in_specs=[pl.BlockSpec((None, tm, d), lambda b, i: (b, i, 0)),
                  pl.BlockSpec((None, 6, d), mod_map),
                  pl.BlockSpec(w.shape, lambda b, i: (0, 0))],
        out_specs=[pl.BlockSpec((None, tm, MAIN_W), lambda b, i: (b, i, 0)),
                   pl.BlockSpec((None, tm, MISC_W), lambda b, i: (b, i, 0))],
        out_shape=(jax.ShapeDtypeStruct((bsz, t, MAIN_W), BF16),
                   jax.ShapeDtypeStruct((bsz, t, MISC_W), F32)),
        compiler_params=_cparams(("parallel", "parallel")),
        name="inproj",
    )(x, mod, w)


def _gla_kernel(qkf_ref, vf_ref, mf_ref, qkb_ref, vb_ref, mb_ref, wgf_ref, bgf_ref, wgb_ref, bgb_ref,
                s0_ref, of_ref, ob_ref, sfin_ref, st_f, st_b, dsf_scr, dsb_scr, oif_scr, oib_scr, *, n_chunks):
    i = pl.program_id(1)
    nblk = pl.num_programs(1)
    C = GLA_CHUNK

    @pl.when(i == 0)
    def _():
        st_f[...] = s0_ref[0]
        st_b[...] = s0_ref[1]

    r64 = lax.broadcasted_iota(jnp.int32, (C, C), 0)
    c64 = lax.broadcasted_iota(jnp.int32, (C, C), 1)
    ra = lax.broadcasted_iota(jnp.int32, (GLA_HEADS * C, C), 0) & (C - 1)
    ca = lax.broadcasted_iota(jnp.int32, (GLA_HEADS * C, C), 1)
    lane_head = lax.broadcasted_iota(jnp.int32, (C, GLA_QK), 1) // GLA_DK
    head_masks = [jnp.where(lane_head == h, 1.0, 0.0) for h in range(GLA_HEADS)]

    def local_part(dirs):
        chunk_rows = [slice(c * C, (c + 1) * C) for c in range(n_chunks)]
        units = [(d, c) for d in range(len(dirs)) for c in range(n_chunks)]
        tri, causal, last_row, lg_hi, lg_lo = [], [], [], [], []
        for qk_ref, v_ref, m_ref, wg_ref, bg_ref, o_ref, ds_scr, forward in dirs:
            if forward:
                tri.append(jnp.where(c64 <= r64, 1.0, 0.0).astype(BF16))
                causal.append(ca <= ra)
                last_row.append(C - 1)
            else:
                tri.append(jnp.where(c64 >= r64, 1.0, 0.0).astype(BF16))
                causal.append(ca >= ra)
                last_row.append(0)
            z_all = _dot(m_ref[...].astype(BF16), wg_ref[...]) + bg_ref[...]
            lg_all = (jnp.minimum(z_all, 0.0) - jnp.log(1.0 + jnp.exp(-jnp.abs(z_all)))) * (1.0 / GLA_GATE_NORM)
            lg_hi.append(lg_all.astype(BF16))
            lg_lo.append((lg_all - lg_hi[-1].astype(F32)).astype(BF16))
        bs = [_dot(tri[d], lg_hi[d][chunk_rows[c], :]) + _dot(tri[d], lg_lo[d][chunk_rows[c], :]) for d, c in units]
        tots = [b[last_row[d]:last_row[d] + 1, :] for (d, c), b in zip(units, bs)]
        qs = [dirs[d][0][chunk_rows[c], 0:GLA_QK].astype(F32) for d, c in units]
        ks = [dirs[d][0][chunk_rows[c], GLA_QK:2 * GLA_QK].astype(F32) for d, c in units]
        vs = [dirs[d][1][chunk_rows[c], :] for d, c in units]
        q_es = [q * (jnp.exp(b) * (GLA_DK ** -0.5)) for q, b in zip(qs, bs)]
        k_es = [(k * jnp.exp(-b)).astype(BF16) for k, b in zip(ks, bs)]
        k_decs = [(k * jnp.exp(t - b)).astype(BF16) for k, b, t in zip(ks, bs, tots)]
        qms = [jnp.concatenate([(q_e * head_masks[h]).astype(BF16) for h in range(GLA_HEADS)], axis=0) for q_e in q_es]
        a_s = [jnp.where(causal[d], _dot_nt(qm, k_e), 0.0).astype(BF16) for (d, c), qm, k_e in zip(units, qms, k_es)]
        for u, (d, c) in enumerate(units):
            dirs[d][5][chunk_rows[c], :] = jnp.concatenate(
                [_dot(a_s[u][h * C:(h + 1) * C, :], vs[u][:, h * GLA_DV:(h + 1) * GLA_DV]) for h in range(GLA_HEADS)],
                axis=1)
        for u, (d, c) in enumerate(units):
            dirs[d][6][c] = _dot_tn(vs[u], k_decs[u])
        return {unit: (qm, jnp.exp(t)) for unit, qm, t in zip(units, qms, tots)}

    def state_step(c, qm, dec, o_ref, oi_scr, ds_scr, st):
        rows = slice(c * C, (c + 1) * C)
        st_b16 = st[...].astype(BF16)
        o_inter = jnp.concatenate(
            [_dot_nt(qm[h * C:(h + 1) * C, :], st_b16[h * GLA_DV:(h + 1) * GLA_DV, :]) for h in range(GLA_HEADS)],
            axis=1)
        o_ref[rows, :] = (oi_scr[rows, :] + o_inter).astype(o_ref.dtype)
        st[...] = st[...] * dec + ds_scr[c]

    loc = local_part([(qkf_ref, vf_ref, mf_ref, wgf_ref, bgf_ref, oif_scr, dsf_scr, True),
                      (qkb_ref, vb_ref, mb_ref, wgb_ref, bgb_ref, oib_scr, dsb_scr, False)])
    for c in range(n_chunks):
        cb = n_chunks - 1 - c
        state_step(c, *loc[(0, c)], of_ref, oif_scr, dsf_scr, st_f)
        state_step(cb, *loc[(1, cb)], ob_ref, oib_scr, dsb_scr, st_b)

    @pl.when(i == nblk - 1)
    def _():
        sfin_ref[0] = st_f[...]
        sfin_ref[1] = st_b[...]


def _gla(main, misc, wgf, bgf, wgb, bgb, s0, tm):
    bsz, t, _ = main.shape
    nblk = t // tm
    fwd = lambda b, i: (b, i, 0)
    bwd = lambda b, i: (b, nblk - 1 - i, 0)
    const2 = lambda b, i: (0, 0)
    kern = functools.partial(_gla_kernel, n_chunks=tm // GLA_CHUNK)
    return pl.pallas_call(
        kern,
        grid=(bsz, nblk),
        in_specs=[pl.BlockSpec((None, tm, 2 * GLA_QK), fwd),
                  pl.BlockSpec((None, tm, GLA_WIDTH), lambda b, i: (b, i, 1)),
                  pl.BlockSpec((None, tm, LANES), fwd),
                  pl.BlockSpec((None, tm, 2 * GLA_QK), bwd),
                  pl.BlockSpec((None, tm, GLA_WIDTH), lambda b, i: (b, nblk - 1 - i, 1)),
                  pl.BlockSpec((None, tm, LANES), bwd),
                  pl.BlockSpec(wgf.shape, const2), pl.BlockSpec(bgf.shape, const2),
                  pl.BlockSpec(wgb.shape, const2), pl.BlockSpec(bgb.shape, const2),
                  pl.BlockSpec((None, 2, GLA_WIDTH, GLA_QK), lambda b, i: (b, 0, 0, 0))],
        out_specs=[pl.BlockSpec((None, tm, GLA_WIDTH), fwd),
                   pl.BlockSpec((None, tm, GLA_WIDTH), bwd),
                   pl.BlockSpec((None, 2, GLA_WIDTH, GLA_QK), lambda b, i: (b, 0, 0, 0))],
        out_shape=(jax.ShapeDtypeStruct((bsz, t, GLA_WIDTH), BF16),
                   jax.ShapeDtypeStruct((bsz, t, GLA_WIDTH), BF16),
                   jax.ShapeDtypeStruct((bsz, 2, GLA_WIDTH, GLA_QK), F32)),
        scratch_shapes=[pltpu.VMEM((GLA_WIDTH, GLA_QK), F32), pltpu.VMEM((GLA_WIDTH, GLA_QK), F32),
                        pltpu.VMEM((tm // GLA_CHUNK, GLA_WIDTH, GLA_QK), F32),
                        pltpu.VMEM((tm // GLA_CHUNK, GLA_WIDTH, GLA_QK), F32),
                        pltpu.VMEM((tm, GLA_WIDTH), F32), pltpu.VMEM((tm, GLA_WIDTH), F32)],
        compiler_params=_cparams(("parallel", "arbitrary")),
        name="gla",
    )(main, main, misc, main, main, misc, wgf, bgf, wgb, bgb, s0)


def _rmsnorm_rows(x, g):
    xf = x.astype(F32)
    ms = jnp.mean(xf * xf, axis=-1, keepdims=True)
    return (xf * lax.rsqrt(ms + EPS)) * g


def _mla_proj_kernel(*refs, rotate, with_q):
    if with_q:
        (cq_ref, ckv_ref, m0_ref, m1_ref, cos_ref, sin_ref, qg_ref, kvg_ref, wq_ref, wkv_ref,
         q_out, k_out, v_out) = refs
    else:
        ckv_ref, m0_ref, kvg_ref, wkv_ref, k_out, v_out = refs
    hw = MLA_HEADS * LANES
    lane = lax.broadcasted_iota(jnp.int32, m0_ref.shape, 1)
    rope_lanes = (lane >= ROPE_LANE0) & (lane < ROPE_LANE0 + MLA_ROPE)
    if rotate:
        cos = cos_ref[...]
        sin = sin_ref[...]
        kr = m0_ref[...] * cos + m1_ref[...] * sin
    else:
        kr = jnp.where(rope_lanes, m0_ref[...], 0.0)
    kv = _dot(_rmsnorm_rows(ckv_ref[...], kvg_ref[...]).astype(BF16), wkv_ref[...])
    for h in range(MLA_HEADS):
        k_out[h] = (kv[:, h * LANES:(h + 1) * LANES] + kr).astype(BF16)
    for p in range(MLA_PAIRS):
        v_out[p] = jnp.transpose(kv[:, hw + p * LANES:hw + (p + 1) * LANES]).astype(BF16)
    if with_q:
        qs = (MLA_NOPE + MLA_ROPE) ** -0.5 * math.log2(math.e)
        cq_tab = jnp.where(lane < MLA_NOPE, qs, cos * qs)
        sq_tab = sin * qs
        qq = _dot(_rmsnorm_rows(cq_ref[...], qg_ref[...]).astype(BF16), wq_ref[...])
        for h in range(MLA_HEADS):
            qa = qq[:, h * LANES:(h + 1) * LANES]
            qb = qq[:, hw + h * LANES:hw + (h + 1) * LANES]
            q_out[h] = (qa * cq_tab + qb * sq_tab).astype(BF16)


def _mla_proj_lat(main, misc, cos_t, sin_t, qg, kvg, wq, wkv, tm):
    bsz, t, _ = main.shape
    c2 = lambda b, i: (0, 0)
    kern = functools.partial(_mla_proj_kernel, rotate=True, with_q=True)
    return pl.pallas_call(
        kern,
        grid=(bsz, t // tm),
        in_specs=[pl.BlockSpec((None, tm, MLA_Q_RANK), lambda b, i: (b, i, COL_CQ // MLA_Q_RANK)),
                  pl.BlockSpec((None, tm, MLA_KV_RANK), lambda b, i: (b, i, COL_CKV // MLA_KV_RANK)),
                  pl.BlockSpec((None, tm, LANES), lambda b, i: (b, i, 0)),
                  pl.BlockSpec((None, tm, LANES), lambda b, i: (b, i, 1)),
                  pl.BlockSpec((tm, LANES), lambda b, i: (i, 0)),
                  pl.BlockSpec((tm, LANES), lambda b, i: (i, 0)),
                  pl.BlockSpec(qg.shape, c2), pl.BlockSpec(kvg.shape, c2),
                  pl.BlockSpec(wq.shape, c2), pl.BlockSpec(wkv.shape, c2)],
        out_specs=[pl.BlockSpec((None, MLA_HEADS, tm, LANES), lambda b, i: (b, 0, i, 0)),
                   pl.BlockSpec((None, MLA_HEADS, tm, LANES), lambda b, i: (b, 0, i, 0)),
                   pl.BlockSpec((None, MLA_PAIRS, V_TILE, tm), lambda b, i: (b, 0, 0, i))],
        out_shape=(jax.ShapeDtypeStruct((bsz, MLA_HEADS, t, LANES), BF16),
                   jax.ShapeDtypeStruct((bsz, MLA_HEADS, t, LANES), BF16),
                   jax.ShapeDtypeStruct((bsz, MLA_PAIRS, V_TILE, t), BF16)),
        compiler_params=_cparams(("parallel", "parallel")),
        name="mla_proj_lat",
    )(main, main, misc, misc, cos_t, sin_t, qg, kvg, wq, wkv)


def _mla_proj_ctx(main, misc, kvg, wkv):
    bsz, t, _ = main.shape
    c2 = lambda b: (0, 0)
    kern = functools.partial(_mla_proj_kernel, rotate=False, with_q=False)
    return pl.pallas_call(
        kern,
        grid=(bsz,),
        in_specs=[pl.BlockSpec((None, t, MLA_KV_RANK), lambda b: (b, 0, COL_CKV // MLA_KV_RANK)),
                  pl.BlockSpec((None, t, LANES), lambda b: (b, 0, 0)),
                  pl.BlockSpec(kvg.shape, c2), pl.BlockSpec(wkv.shape, c2)],
        out_specs=[pl.BlockSpec((None, MLA_HEADS, t, LANES), lambda b: (b, 0, 0, 0)),
                   pl.BlockSpec((None, MLA_PAIRS, V_TILE, t), lambda b: (b, 0, 0, 0))],
        out_shape=(jax.ShapeDtypeStruct((bsz, MLA_HEADS, t, LANES), BF16),
                   jax.ShapeDtypeStruct((bsz, MLA_PAIRS, V_TILE, t), BF16)),
        compiler_params=_cparams(("parallel",)),
        name="mla_proj_ctx",
    )(main, misc, kvg, wkv)


def _attn_t_kernel(q_ref, kl_ref, kc_ref, vtl_ref, vtc_ref, o_ref, *bufs, tq, tk):
    tb = q_ref.shape[1]
    s_len = kl_ref.shape[1]
    c_len = kc_ref.shape[1]
    n_sub = tb // tq
    chunks = [(kl_ref, vtl_ref, c0, tk, c0) for c0 in range(0, s_len, tk)] + [(kc_ref, vtc_ref, 0, c_len, s_len)]

    def pass1_chunk(buf, hh, sub, c, m):
        k_ref, _, r0, n, row = chunks[c]
        q = q_ref[hh, sub * tq:(sub + 1) * tq, :]
        s_t = _dot_nt(k_ref[hh, r0:r0 + n, :], q)
        buf[row:row + n, :] = s_t
        return jnp.maximum(m, jnp.max(s_t, axis=0, keepdims=True))

    def pass2_chunk(buf, hh, c, m, l, acc):
        _, vt_ref, r0, n, row = chunks[c]
        p_t = jnp.exp2(buf[row:row + n, :] - m)
        l = l + jnp.sum(p_t, axis=0, keepdims=True)
        acc = acc + _dot(vt_ref[hh * MLA_DV:(hh + 1) * MLA_DV, r0:r0 + n], p_t.astype(BF16))
        return l, acc

    per_head = ATTN_PIPES // 2
    per_pipe = n_sub // per_head
    pipes = [(hh, [part * per_pipe + j for j in range(per_pipe)]) for hh in range(2) for part in range(per_head)]
    outs = {}
    m_prev = [None] * len(pipes)
    for step in range(per_pipe + 1):
        m_new = [jnp.full((1, tq), -jnp.inf, F32) for _ in pipes]
        l = [jnp.zeros((1, tq), F32) for _ in pipes]
        acc = [jnp.zeros((MLA_DV, tq), F32) for _ in pipes]
        for c in range(len(chunks)):
            for p, (hh, subs) in enumerate(pipes):
                if step > 0:
                    l[p], acc[p] = pass2_chunk(bufs[2 * p + (step - 1) % 2], hh, c, m_prev[p], l[p], acc[p])
                if step < per_pipe:
                    m_new[p] = pass1_chunk(bufs[2 * p + step % 2], hh, subs[step], c, m_new[p])
        if step > 0:
            for p, (hh, subs) in enumerate(pipes):
                outs[(hh, subs[step - 1])] = acc[p] * (1.0 / l[p])
        m_prev = m_new
    for sub in range(n_sub):
        o_t = jnp.concatenate([outs[(0, sub)], outs[(1, sub)]], axis=0)
        o_ref[sub * tq:(sub + 1) * tq, :] = jnp.transpose(o_t).astype(BF16)


def _attn(q, k_lat, k_ctx, vt_lat, vt_ctx, tb, tq, tk):
    bsz, _, s_len, _ = q.shape
    c_len = k_ctx.shape[2]
    kern = functools.partial(_attn_t_kernel, tq=tq, tk=tk)
    return pl.pallas_call(
        kern,
        grid=(bsz, MLA_PAIRS, s_len // tb),
        in_specs=[pl.BlockSpec((None, 2, tb, LANES), lambda b, p, i: (b, p, i, 0)),
                  pl.BlockSpec((None, 2, s_len, LANES), lambda b, p, i: (b, p, 0, 0)),
                  pl.BlockSpec((None, 2, c_len, LANES), lambda b, p, i: (b, p, 0, 0)),
                  pl.BlockSpec((None, None, V_TILE, s_len), lambda b, p, i: (b, p, 0, 0)),
                  pl.BlockSpec((None, None, V_TILE, c_len), lambda b, p, i: (b, p, 0, 0))],
        out_specs=pl.BlockSpec((None, None, tb, LANES), lambda b, p, i: (b, p, i, 0)),
        out_shape=jax.ShapeDtypeStruct((bsz, MLA_PAIRS, s_len, LANES), BF16),
        scratch_shapes=[pltpu.VMEM((s_len + c_len, tq), F32) for _ in range(ATTN_BUFS)],
        compiler_params=_cparams(("parallel", "parallel", "arbitrary")),
        name="attn",
    )(q, k_lat, k_ctx, vt_lat, vt_ctx)


def _layernorm_rows(z, g, b):
    mu = jnp.mean(z, axis=-1, keepdims=True)
    zc = z - mu
    var = jnp.mean(zc * zc, axis=-1, keepdims=True)
    return (zc * lax.rsqrt(var + EPS)) * g + b


def _outproj_kernel(of_ref, ob_ref, r_ref, ml_ref, x_ref, mod_ref, gg_ref, wo_ref, l1g_ref, l1b_ref,
                    wr_ref, br_ref, x1_ref, h_ref, comb_ref):
    tm = x_ref.shape[0]
    o = of_ref[...].astype(F32) + ob_ref[...].astype(F32)
    r = r_ref[...].astype(F32)
    gg = gg_ref[...]
    mix = []
    for h in range(GLA_HEADS):
        sl = slice(h * GLA_DV, (h + 1) * GLA_DV)
        oh = o[:, sl]
        ms = jnp.mean(oh * oh, axis=-1, keepdims=True)
        rh = r[:, sl]
        mix.append(((oh * lax.rsqrt(ms + EPS)) * gg * (rh * _sigmoid(rh))).astype(BF16))
    mix += [ml_ref[p] for p in range(MLA_PAIRS)]
    y = _dot(jnp.concatenate(mix, axis=1), wo_ref[...])
    gate1 = mod_ref[2:3, :]
    x1 = _layernorm_rows(DEEPNORM_ALPHA * x_ref[...] + gate1 * y, l1g_ref[...], l1b_ref[...])
    x1_ref[...] = x1
    hmod = x1 * (1.0 + mod_ref[4:5, :]) + mod_ref[3:4, :]
    h_ref[...] = hmod.astype(BF16)

    h_hi = hmod.astype(BF16)
    h_lo = (hmod - h_hi.astype(F32)).astype(BF16)
    wr = wr_ref[...]
    w_hi = wr.astype(BF16)
    w_lo = (wr - w_hi.astype(F32)).astype(BF16)
    pp = _dot(jnp.concatenate([h_hi, h_lo], axis=0), jnp.concatenate([w_hi, w_lo], axis=1))
    logits = ((pp[0:tm, 0:LANES] + pp[0:tm, LANES:2 * LANES])
              + (pp[tm:2 * tm, 0:LANES] + pp[tm:2 * tm, LANES:2 * LANES]) + br_ref[...])

    lane = lax.broadcasted_iota(jnp.int32, (tm, LANES), 1).astype(F32)
    neg = -jnp.inf
    far = float(LANES)
    gl = jnp.where(lane < N_GROUPS, logits, neg)
    gmax = jnp.max(gl, axis=-1, keepdims=True)
    gsum = jnp.sum(jnp.exp(gl - gmax), axis=-1, keepdims=True)
    p_g = 1.0 / gsum
    g_top = jnp.min(jnp.where(gl == gmax, lane, far), axis=-1, keepdims=True)
    e0 = ROUTER_LANE0 + g_top * EXPERTS_PER_GROUP
    el = jnp.where((lane >= e0) & (lane < e0 + EXPERTS_PER_GROUP), logits, neg)
    e1max = jnp.max(el, axis=-1, keepdims=True)
    i1 = jnp.min(jnp.where(el == e1max, lane, far), axis=-1, keepdims=True)
    el2 = jnp.where(lane == i1, neg, el)
    e2max = jnp.max(el2, axis=-1, keepdims=True)
    i2 = jnp.min(jnp.where(el2 == e2max, lane, far), axis=-1, keepdims=True)
    t = jnp.exp(e2max - e1max)
    w1 = p_g / (1.0 + t)
    w2 = w1 * t
    comb_ref[...] = jnp.where(lane == i1, w1, jnp.where(lane == i2, w2, jnp.where(lane == g_top, 1.0, 0.0)))


def _outproj(o_f, o_b, main, mlat, x, mod, gg, wo, l1g, l1b, wr, br, tm):
    bsz, t, d = x.shape
    c2 = lambda b, i: (0, 0)
    row = lambda b, i: (b, i, 0)
    return pl.pallas_call(
        _outproj_kernel,
        grid=(bsz, t // tm),
        in_specs=[pl.BlockSpec((None, tm, GLA_WIDTH), row),
                  pl.BlockSpec((None, tm, GLA_WIDTH), row),
                  pl.BlockSpec((None, tm, GLA_WIDTH), lambda b, i: (b, i, COL_R // GLA_WIDTH)),
                  pl.BlockSpec((None, MLA_PAIRS, tm, LANES), lambda b, i: (b, 0, i, 0)),
                  pl.BlockSpec((None, tm, d), row),
                  pl.BlockSpec((None, 6, d), lambda b, i: (b, 0, 0)),
                  pl.BlockSpec(gg.shape, c2), pl.BlockSpec(wo.shape, c2),
                  pl.BlockSpec(l1g.shape, c2), pl.BlockSpec(l1b.shape, c2),
                  pl.BlockSpec(wr.shape, c2), pl.BlockSpec(br.shape, c2)],
        out_specs=[pl.BlockSpec((None, tm, d), row),
                   pl.BlockSpec((None, tm, d), row),
                   pl.BlockSpec((None, tm, LANES), row)],
        out_shape=(jax.ShapeDtypeStruct((bsz, t, d), F32),
                   jax.ShapeDtypeStruct((bsz, t, d), BF16),
                   jax.ShapeDtypeStruct((bsz, t, LANES), F32)),
        compiler_params=_cparams(("parallel", "parallel")),
        name="outproj",
    )(o_f, o_b, main, mlat, x, mod, gg, wo, l1g, l1b, wr, br)


def _moe_kernel(h_ref, comb_ref, x1_ref, mod_ref, wgu_ref, wd_ref, l2g_ref, l2b_ref, o_ref,
                hs_scr, cs_scr, acc_scr):
    tm = MOE_SORT_TILE
    n_sub = h_ref.shape[0] // tm
    rb = MOE_ROW_BLOCK
    lane = lax.broadcasted_iota(jnp.int32, (tm, LANES), 1)
    ri = lax.broadcasted_iota(jnp.int32, (tm, tm), 0)
    ci = lax.broadcasted_iota(jnp.int32, (tm, tm), 1)
    lower = jnp.where(ci < ri, 1.0, 0.0).astype(BF16)
    ci_f = ci.astype(F32)

    pts, tot_i, off_i = [], [], []
    for s in range(n_sub):
        comb = comb_ref[s * tm:(s + 1) * tm, :]
        onehot = jnp.where(lane < N_GROUPS, comb, 0.0)
        before = _dot(lower, onehot.astype(BF16))
        rank = jnp.sum(before * onehot, axis=-1, keepdims=True)
        totals = jnp.broadcast_to(jnp.sum(onehot, axis=0, keepdims=True), (8, LANES))
        offs = pltpu.roll(totals, 1, 1) + pltpu.roll(totals, 2, 1) + pltpu.roll(totals, 3, 1)
        pos = jnp.sum(onehot * offs[0:1, :], axis=-1, keepdims=True) + rank
        pt = jnp.where(ci_f == pos, 1.0, 0.0).astype(BF16)
        hs_scr[s, 0:tm, :] = _dot_tn(pt, h_ref[s * tm:(s + 1) * tm, :]).astype(BF16)
        hs_scr[s, tm:tm + rb, :] = jnp.zeros((rb, hs_scr.shape[2]), BF16)
        c_hi = comb.astype(BF16)
        c_lo = (comb - c_hi.astype(F32)).astype(BF16)
        cs_scr[s, 0:tm, :] = _dot_tn(pt, c_hi) + _dot_tn(pt, c_lo)
        cs_scr[s, tm:tm + rb, :] = jnp.zeros((rb, LANES), F32)
        acc_scr[s] = jnp.zeros(acc_scr.shape[1:], BF16)
        pts.append(pt)
        tot_i.append(totals.astype(jnp.int32))
        off_i.append(offs.astype(jnp.int32))

    lane_r = lax.broadcasted_iota(jnp.int32, (n_sub * rb, LANES), 1)
    for g in range(N_GROUPS):
        firsts, counts = [], []
        for s in range(n_sub):
            n_g = tot_i[s][0, g]
            start = off_i[s][0, g]
            first = (start // BF16_SUBLANES) * BF16_SUBLANES
            firsts.append(first)
            counts.append(jnp.where(n_g > 0, (start + n_g - first + rb - 1) // rb, 0))
        n_blocks = functools.reduce(jnp.maximum, counts)

        def block(k, carry, g=g, firsts=firsts, counts=counts):
            rows = [pl.ds(pl.multiple_of(jnp.where(k < counts[s], firsts[s] + k * rb, tm), BF16_SUBLANES), rb)
                    for s in range(n_sub)]
            hb = jnp.concatenate([hs_scr[s, rows[s], :] for s in range(n_sub)], axis=0)
            cb = jnp.concatenate([cs_scr[s, rows[s], :] for s in range(n_sub)], axis=0)
            parts = []
            for j in range(EXPERTS_PER_GROUP):
                e = g * EXPERTS_PER_GROUP + j
                w = jnp.sum(jnp.where(lane_r == ROUTER_LANE0 + e, cb, 0.0), axis=-1, keepdims=True)
                gu = _dot(hb, wgu_ref[e])
                gj = gu[:, 0:D_EXPERT]
                uj = gu[:, D_EXPERT:2 * D_EXPERT]
                parts.append(((gj * _sigmoid(gj)) * uj * w).astype(BF16))
            out = _dot(jnp.concatenate(parts, axis=1), wd_ref[g])
            for s in range(n_sub):
                acc_scr[s, rows[s], :] = (acc_scr[s, rows[s], :].astype(F32) + out[s * rb:(s + 1) * rb, :]).astype(BF16)
            return carry

        lax.fori_loop(0, n_blocks, block, 0)

    gate2 = mod_ref[5:6, :]
    for s in range(n_sub):
        y = _dot(pts[s], acc_scr[s, 0:tm, :])
        z = DEEPNORM_ALPHA * x1_ref[s * tm:(s + 1) * tm, :] + gate2 * y
        o_ref[s * tm:(s + 1) * tm, :] = _layernorm_rows(z, l2g_ref[...], l2b_ref[...])


def _moe(h, comb, x1, mod, wgu, wd, l2g, l2b, tm):
    bsz, t, d = x1.shape
    row = lambda b, i: (b, i, 0)
    c2 = lambda b, i: (0, 0)
    c3 = lambda b, i: (0, 0, 0)
    resident = pl.Buffered(1)
    return pl.pallas_call(
        _moe_kernel,
        grid=(bsz, t // tm),
        in_specs=[pl.BlockSpec((None, tm, d), row),
                  pl.BlockSpec((None, tm, LANES), row),
                  pl.BlockSpec((None, tm, d), row),
                  pl.BlockSpec((None, 6, d), lambda b, i: (b, 0, 0)),
                  pl.BlockSpec(wgu.shape, c3, pipeline_mode=resident),
                  pl.BlockSpec(wd.shape, c3, pipeline_mode=resident),
                  pl.BlockSpec(l2g.shape, c2), pl.BlockSpec(l2b.shape, c2)],
        out_specs=pl.BlockSpec((None, tm, d), row),
        out_shape=jax.ShapeDtypeStruct((bsz, t, d), F32),
        scratch_shapes=[pltpu.VMEM((tm // MOE_SORT_TILE, MOE_SORT_TILE + MOE_ROW_BLOCK, d), BF16),
                        pltpu.VMEM((tm // MOE_SORT_TILE, MOE_SORT_TILE + MOE_ROW_BLOCK, LANES), F32),
                        pltpu.VMEM((tm // MOE_SORT_TILE, MOE_SORT_TILE + MOE_ROW_BLOCK, d), BF16)],
        compiler_params=pltpu.CompilerParams(dimension_semantics=("parallel", "parallel"),
                                             vmem_limit_bytes=MOE_VMEM_LIMIT),
        name="moe",
    )(h, comb, x1, mod, wgu, wd, l2g, l2b)


_ROPE_SWAP = np.concatenate([np.arange(8, 16), np.arange(0, 8), np.arange(24, 32), np.arange(16, 24)])


def _pack_w_in(w_in):
    d = w_in.shape[0]
    o_q, o_k, o_v, o_gf, o_gb, o_r, o_cq, o_ckv, o_kr = 0, 256, 512, 1024, 1040, 1056, 1568, 1824, 1952
    z = lambda n: jnp.zeros((d, n), w_in.dtype)
    kr = w_in[:, o_kr:o_kr + MLA_ROPE]
    cols = [w_in[:, o_q:o_k], w_in[:, o_k:o_v], w_in[:, o_v:o_gf], w_in[:, o_r:o_cq], w_in[:, o_cq:o_ckv],
            w_in[:, o_ckv:o_kr],
            w_in[:, o_gf:o_gb], w_in[:, o_gb:o_r], z(32), kr, z(32),
            z(64), kr[:, _ROPE_SWAP], z(32)]
    return jnp.concatenate(cols, axis=1).astype(BF16)


def _pack_gate_w(w_gk, lane0):
    out = jnp.zeros((LANES, w_gk.shape[1]), w_gk.dtype)
    return out.at[lane0:lane0 + GLA_RANK].set(w_gk).astype(BF16)


def _pack_w_uq(w_uq):
    r = w_uq.shape[0]
    w = w_uq.reshape(r, MLA_HEADS, MLA_NOPE + MLA_ROPE)
    nope, rope = w[..., :MLA_NOPE], w[..., MLA_NOPE:]
    z32 = jnp.zeros((r, MLA_HEADS, 32), w_uq.dtype)
    z64 = jnp.zeros((r, MLA_HEADS, 64), w_uq.dtype)
    a = jnp.concatenate([nope, rope, z32], axis=-1).reshape(r, MLA_HEADS * LANES)
    b = jnp.concatenate([z64, rope[..., _ROPE_SWAP], z32], axis=-1).reshape(r, MLA_HEADS * LANES)
    return jnp.concatenate([a, b], axis=1).astype(BF16)


def _pack_w_ukv(w_ukv):
    r = w_ukv.shape[0]
    w = w_ukv.reshape(r, MLA_HEADS, MLA_NOPE + MLA_DV)
    kn, v = w[..., :MLA_NOPE], w[..., MLA_NOPE:]
    k_t = jnp.concatenate([kn, jnp.zeros_like(kn)], axis=-1).reshape(r, MLA_HEADS * LANES)
    v_t = v.reshape(r, MLA_HEADS * MLA_DV)
    return jnp.concatenate([k_t, v_t], axis=1).astype(BF16)


def _pack_experts(w_gate, w_up, w_down):
    n_e, d, de = w_gate.shape
    wgu = jnp.concatenate([w_gate, w_up], axis=-1).astype(BF16)
    wd = w_down.reshape(N_GROUPS, EXPERTS_PER_GROUP * de, d).astype(BF16)
    return wgu, wd


def _pack_router(w_rg, b_rg, w_re, b_re):
    d = w_rg.shape[0]
    pad = LANES - N_GROUPS - N_EXPERTS
    w = jnp.concatenate([w_rg, w_re, jnp.zeros((d, pad), w_rg.dtype)], axis=1)
    b = jnp.concatenate([b_rg, b_re, jnp.zeros((pad,), b_rg.dtype)]).reshape(1, LANES)
    return w, b


def kernel(x, c, ctx, c_ctx, w_ada, b_ada, w_in, w_gk_f, b_gk_f, w_gk_b, b_gk_b, gla_norm_g, mla_q_norm_g, w_uq, mla_kv_norm_g, w_ukv, w_o, ln1_g, ln1_b, w_router_group, b_router_group, w_router_expert, b_router_expert, w_expert_gate, w_expert_up, w_expert_down, ln2_g, ln2_b):
    bsz, seq, d = x.shape
    l = 0

    pad_rows = -(bsz + 1) % BF16_SUBLANES
    c_all = jnp.concatenate([c, c_ctx[None, :], jnp.zeros((pad_rows, d), c.dtype)], axis=0)
    mod = _ada(c_all, w_ada[l], b_ada[l])
    mod_lat = mod[:bsz].reshape(bsz, 6, d)
    mod_ctx = mod[bsz:bsz + 1].reshape(1, 6, d)

    w_in_p = _pack_w_in(w_in[l])
    main_lat, misc_lat = _inproj(x, mod_lat, w_in_p, True, TOKEN_TILE)
    main_ctx, misc_ctx = _inproj(ctx, mod_ctx, w_in_p, False, ctx.shape[1])

    wgf = _pack_gate_w(w_gk_f[l], 0)
    wgb = _pack_gate_w(w_gk_b[l], GLA_RANK)
    bgf = b_gk_f[l].reshape(1, GLA_QK)
    bgb = b_gk_b[l].reshape(1, GLA_QK)
    s_zero = jnp.zeros((bsz, 2, GLA_WIDTH, GLA_QK), F32)
    _, _, s_ctx = _gla(main_ctx, misc_ctx, wgf, bgf, wgb, bgb, s_zero, ctx.shape[1])
    o_f, o_b, _ = _gla(main_lat, misc_lat, wgf, bgf, wgb, bgb, s_ctx, GLA_TILE)

    cos_t, sin_t = _rope_tables(seq)
    qg = mla_q_norm_g[l].reshape(1, MLA_Q_RANK)
    kvg = mla_kv_norm_g[l].reshape(1, MLA_KV_RANK)
    wq_p = _pack_w_uq(w_uq[l])
    wkv_p = _pack_w_ukv(w_ukv[l])
    q_lat, k_lat, v_lat = _mla_proj_lat(main_lat, misc_lat, cos_t, sin_t, qg, kvg, wq_p, wkv_p, TOKEN_TILE)
    k_ctx, v_ctx = _mla_proj_ctx(main_ctx, misc_ctx, kvg, wkv_p)
    m_lat = _attn(q_lat, k_lat, k_ctx, v_lat, v_ctx, ATTN_TB, ATTN_TQ, ATTN_TK)

    gg = gla_norm_g[l].reshape(1, GLA_DV)
    w_r, b_r = _pack_router(w_router_group[l], b_router_group[l], w_router_expert[l], b_router_expert[l])
    x1, h, comb = _outproj(o_f, o_b, main_lat, m_lat, x, mod_lat, gg, w_o[l].astype(BF16),
                           ln1_g[l].reshape(1, d), ln1_b[l].reshape(1, d), w_r, b_r, OUTPROJ_TILE)

    wgu, wd = _pack_experts(w_expert_gate[l], w_expert_up[l], w_expert_down[l])
    return _moe(h, comb, x1, mod_lat, wgu, wd, ln2_g[l].reshape(1, d), ln2_b[l].reshape(1, d), MOE_TILE)
```
